```python
import math
import jax
import jax.numpy as jnp
from jax import lax
import numpy as np

D_MODEL = 2048
BATCH = 1
SEQ = 8192
DEPTH = 4

GRID_W = 64
CTX_LEN = 256
HEAD_DIM = 128
ROPE_THETA = 10000.0
ROPE_FREQS = HEAD_DIM // 4
BLK = 128
WINDOW = 128
A_HEADS = 8
A_KV = 2
B_HEADS = 4
B_DK = 128
B_DV = 256
MLSTM_CHUNK = 128
C_HEADS = 16
C_KV = 4
D_FF = 256 * ((8 * D_MODEL // 3 + 255) // 256)
N_EXPERTS = 8
TOP_K = 2
D_FF_EXPERT = 7 * D_MODEL // 2
N_EVEN = (DEPTH + 1) // 2
N_ODD = DEPTH // 2
EPS = 1e-6
AB_SPLITS = (A_HEADS * HEAD_DIM, A_KV * HEAD_DIM, A_KV * HEAD_DIM,
             B_HEADS * B_DK, B_HEADS * B_DK, B_HEADS * B_DV, B_HEADS * B_DV,
             2 * B_HEADS, 2 * B_HEADS)
AB_IN = sum(AB_SPLITS)
AB_MIX = A_HEADS * HEAD_DIM + B_HEADS * B_DV
C_SPLITS = (C_HEADS * HEAD_DIM, C_KV * HEAD_DIM, C_KV * HEAD_DIM)
C_IN = sum(C_SPLITS)
C_MIX = C_HEADS * HEAD_DIM

kernel_name = 'hybrid_swa_mlstm_qknorm_moe_dit_trunk'


def split_cols(y, sizes):
    offs = [int(o) for o in np.cumsum(sizes)[:-1]]
    return jnp.split(y, offs, axis=-1)


def heads(t, n):
    return t.reshape(t.shape[0], t.shape[1], n, -1)


def rmsnorm(x, g):
    x32 = x.astype(jnp.float32)
    y = x32 * lax.rsqrt(jnp.mean(x32 * x32, axis=-1, keepdims=True) + EPS)
    return (y * g.astype(jnp.float32)).astype(x.dtype)


def axial_rope_tables(S):
    rows = S // GRID_W
    row = jnp.repeat(jnp.arange(rows), GRID_W).astype(jnp.float32)
    col = jnp.tile(jnp.arange(GRID_W), rows).astype(jnp.float32)
    inv = ROPE_THETA ** (-jnp.arange(ROPE_FREQS, dtype=jnp.float32) / ROPE_FREQS)
    ang = jnp.concatenate([row[:, None] * inv, col[:, None] * inv], axis=-1)
    return jnp.cos(ang), jnp.sin(ang)


def apply_rope(x, cos, sin):
    f = ROPE_FREQS
    half = HEAD_DIM // 2
    c = cos[:, None, :].astype(x.dtype)
    s = sin[:, None, :].astype(x.dtype)

    def rot(seg, cc, ss):
        x1, x2 = seg[..., :f], seg[..., f:]
        return jnp.concatenate([x1 * cc - x2 * ss, x2 * cc + x1 * ss], axis=-1)

    return jnp.concatenate([rot(x[..., :half], c[..., :f], s[..., :f]),
                            rot(x[..., half:], c[..., f:], s[..., f:])], axis=-1)


def ctx_attn(q, k, v, sink):
    B, T, Hq, dh = q.shape
    Hkv = k.shape[2]
    G = Hq // Hkv
    qg = q.reshape(B, T, Hkv, G, dh) * (dh ** -0.5)
    s = jnp.einsum('bqhgd,bkhd->bhgqk', qg, k).astype(jnp.float32)
    if sink is not None:
        sk = jnp.broadcast_to(sink.astype(jnp.float32).reshape(1, Hkv, G, 1, 1), s.shape[:-1] + (1,))
        s = jnp.concatenate([s, sk], axis=-1)
    p = jax.nn.softmax(s, axis=-1)[..., :k.shape[1]]
    o = jnp.einsum('bhgqk,bkhd->bqhgd', p.astype(v.dtype), v)
    return o.reshape(B, T, Hq * dh)


def window_attn(q, k, v, kc, vc, sink):
    B, S, Hq, dh = q.shape
    Hkv = k.shape[2]
    G = Hq // Hkv
    nb = S // BLK
    Tc = kc.shape[1]
    qb = q.reshape(B, nb, BLK, Hkv, G, dh) * (dh ** -0.5)
    pad = ((0, 0), (BLK, BLK), (0, 0), (0, 0))
    kp = jnp.pad(k, pad).reshape(B, nb + 2, BLK, Hkv, dh)
    vp = jnp.pad(v, pad).reshape(B, nb + 2, BLK, Hkv, dh)
    kw = jnp.concatenate([kp[:, :-2], kp[:, 1:-1], kp[:, 2:]], axis=2)
    vw = jnp.concatenate([vp[:, :-2], vp[:, 1:-1], vp[:, 2:]], axis=2)
    s_loc = jnp.einsum('bnqhgd,bnkhd->bhgnqk', qb, kw).astype(jnp.float32)
    qi = jnp.arange(BLK)
    kj = jnp.arange(3 * BLK)
    rel = kj[None, :] - BLK - qi[:, None]
    jabs = jnp.arange(nb)[:, None] * BLK - BLK + kj[None, :]
    mask = (jnp.abs(rel) <= WINDOW)[None] & ((jabs >= 0) & (jabs < S))[:, None, :]
    s_loc = jnp.where(mask, s_loc, -jnp.inf)
    s_ctx = jnp.einsum('bnqhgd,bkhd->bhgnqk', qb, kc).astype(jnp.float32)
    s_snk = jnp.broadcast_to(sink.astype(jnp.float32).reshape(1, Hkv, G, 1, 1, 1), s_loc.shape[:-1] + (1,))
    p = jax.nn.softmax(jnp.concatenate([s_loc, s_ctx, s_snk], axis=-1), axis=-1)
    p_loc = p[..., :3 * BLK].astype(v.dtype)
    p_ctx = p[..., 3 * BLK:3 * BLK + Tc].astype(v.dtype)
    o = (jnp.einsum('bhgnqk,bnkhd->bnqhgd', p_loc, vw)
         + jnp.einsum('bhgnqk,bkhd->bnqhgd', p_ctx, vc))
    return o.reshape(B, S, Hq * dh)


def dense_block_attn(q, kall, vall):
    B, S, Hq, dh = q.shape
    Hkv = kall.shape[2]
    G = Hq // Hkv
    nb = S // BLK
    qb = (q * (dh ** -0.5)).reshape(B, nb, BLK, Hkv, G, dh).transpose(1, 0, 2, 3, 4, 5)

    def one(qblk):
        s = jnp.einsum('bqhgd,bkhd->bhgqk', qblk, kall).astype(jnp.float32)
        p = jax.nn.softmax(s, axis=-1).astype(vall.dtype)
        return jnp.einsum('bhgqk,bkhd->bqhgd', p, vall)

    o = lax.map(one, qb)
    return o.transpose(1, 0, 2, 3, 4, 5).reshape(B, S, Hq * dh)


def mlstm_chunkwise(q, k, v, ig, lf, state):
    B, T, H, dk = q.shape
    L = MLSTM_CHUNK
    nc = T // L

    def chunks(t):
        return t.reshape(B, nc, L, H, -1).transpose(1, 0, 3, 2, 4)

    def gchunks(t):
        return t.reshape(B, nc, L, H).transpose(1, 0, 3, 2)

    tril = jnp.tril(jnp.ones((L, L), dtype=bool))

    def step(carry, xs):
        C, n, m = carry
        qc, kc, vc, ic, fc = xs
        b = jnp.cumsum(fc, axis=-1)
        dlog = jnp.where(tril, b[..., :, None] - b[..., None, :] + ic[..., None, :], -jnp.inf)
        inter = b + m[..., None]
        m_t = jnp.maximum(inter, jnp.max(dlog, axis=-1))
        dw = jnp.exp(dlog - m_t[..., None])
        iw = jnp.exp(inter - m_t)
        sc = jnp.einsum('bhtd,bhsd->bhts', qc, kc) * dw
        num = jnp.einsum('bhts,bhsv->bhtv', sc, vc) + iw[..., None] * jnp.einsum('bhtd,bhdv->bhtv', qc, C)
        den = jnp.sum(sc, axis=-1) + iw * jnp.einsum('bhtd,bhd->bht', qc, n)
        h = num / jnp.maximum(jnp.abs(den), jnp.exp(-m_t))[..., None]
        bL = b[..., -1]
        glog = bL[..., None] - b + ic
        m_new = jnp.maximum(bL + m, jnp.max(glog, axis=-1))
        w = jnp.exp(glog - m_new[..., None])
        decay = jnp.exp(bL + m - m_new)
        C_new = decay[..., None, None] * C + jnp.einsum('bhs,bhsd,bhsv->bhdv', w, kc, vc)
        n_new = decay[..., None] * n + jnp.einsum('bhs,bhsd->bhd', w, kc)
        return (C_new, n_new, m_new), h

    xs = (chunks(q) * (dk ** -0.5), chunks(k), chunks(v), gchunks(ig), gchunks(lf))
    state, hs = lax.scan(step, state, xs)
    return hs.transpose(1, 0, 3, 2, 4).reshape(B, T, H, -1), state


def mlstm_bidir(q_l, k_l, v_l, ig_l, lf_l, q_c, k_c, v_c, ig_c, lf_c):
    B, _, H, dk = q_l.shape
    dv = v_l.shape[-1]
    zero = (jnp.zeros((B, H, dk, dv), jnp.float32), jnp.zeros((B, H, dk), jnp.float32),
            jnp.zeros((B, H), jnp.float32))

    def flip(t):
        return jnp.flip(t, axis=1)

    h_cf, st_f = mlstm_chunkwise(q_c, k_c, v_c, ig_c[:, :, 0], lf_c[:, :, 0], zero)
    h_lf, _ = mlstm_chunkwise(q_l, k_l, v_l, ig_l[:, :, 0], lf_l[:, :, 0], st_f)
    h_cb, st_b = mlstm_chunkwise(flip(q_c), flip(k_c), flip(v_c), flip(ig_c[:, :, 1]), flip(lf_c[:, :, 1]), zero)
    h_lb, _ = mlstm_chunkwise(flip(q_l), flip(k_l), flip(v_l), flip(ig_l[:, :, 1]), flip(lf_l[:, :, 1]), st_b)
    return h_lf + flip(h_lb), h_cf + flip(h_cb)


def mixer_ab(hl, hc, w_in, w_out, sink, ig_b, fg_b, out_g, cos, sin, need_ctx):
    f32 = jnp.float32
    qa, ka, va, qb, kb, vb, ob, ib, fb = split_cols(hl @ w_in, AB_SPLITS)
    qa_c, ka_c, va_c, qb_c, kb_c, vb_c, ob_c, ib_c, fb_c = split_cols(hc @ w_in, AB_SPLITS)
    qa_l = apply_rope(heads(qa, A_HEADS), cos, sin)
    ka_l = apply_rope(heads(ka, A_KV), cos, sin)
    ka_c, va_c = heads(ka_c, A_KV), heads(va_c, A_KV)
    ya_l = window_attn(qa_l, ka_l, heads(va, A_KV), ka_c, va_c, sink)

    def gates(i_raw, f_raw):
        B_, T_ = i_raw.shape[:2]
        ig = i_raw.reshape(B_, T_, 2, B_HEADS).astype(f32) + ig_b.astype(f32)
        lf = jax.nn.log_sigmoid(f_raw.reshape(B_, T_, 2, B_HEADS).astype(f32) + fg_b.astype(f32))
        return ig, lf

    ig_l, lf_l = gates(ib, fb)
    ig_c, lf_c = gates(ib_c, fb_c)
    h_l, h_c = mlstm_bidir(heads(qb, B_HEADS).astype(f32), heads(kb, B_HEADS).astype(f32),
                           heads(vb, B_HEADS).astype(f32), ig_l, lf_l,
                           heads(qb_c, B_HEADS).astype(f32), heads(kb_c, B_HEADS).astype(f32),
                           heads(vb_c, B_HEADS).astype(f32), ig_c, lf_c)
    g_heads = out_g.reshape(B_HEADS, B_DV)

    def b_out(h, o):
        hn = rmsnorm(h, g_heads).reshape(h.shape[0], h.shape[1], B_HEADS * B_DV).astype(o.dtype)
        return jax.nn.sigmoid(o) * hn

    yl = jnp.concatenate([ya_l, b_out(h_l, ob)], axis=-1) @ w_out
    yc = None
    if need_ctx:
        ya_c = ctx_attn(heads(qa_c, A_HEADS), ka_c, va_c, sink)
        yc = jnp.concatenate([ya_c, b_out(h_c, ob_c)], axis=-1) @ w_out
    return yl, yc


def mixer_c(hl, hc, w_qkv, w_out, qk_g, cos, sin, need_ctx):
    ql, kl, vl = split_cols(hl @ w_qkv, C_SPLITS)
    qc, kc, vc = split_cols(hc @ w_qkv, C_SPLITS)
    ql = apply_rope(rmsnorm(heads(ql, C_HEADS), qk_g[0]), cos, sin)
    kl = apply_rope(rmsnorm(heads(kl, C_KV), qk_g[1]), cos, sin)
    qc = rmsnorm(heads(qc, C_HEADS), qk_g[0])
    kc = rmsnorm(heads(kc, C_KV), qk_g[1])
    vl, vc = heads(vl, C_KV), heads(vc, C_KV)
    kall = jnp.concatenate([kl, kc], axis=1)
    vall = jnp.concatenate([vl, vc], axis=1)
    yl = dense_block_attn(ql, kall, vall) @ w_out
    yc = ctx_attn(qc, kc, vc, None) @ w_out if need_ctx else None
    return yl, yc


def swiglu(h, wg, wu, wd):
    return (jax.nn.silu(h @ wg) * (h @ wu)) @ wd


def moe_swiglu(h, router, wg, wu, wd):
    logits = (h @ router).astype(jnp.float32)
    vals, idx = lax.top_k(logits, TOP_K)
    w = jax.nn.softmax(vals, axis=-1)
    gates = jnp.sum(jax.nn.one_hot(idx, N_EXPERTS, dtype=jnp.float32) * w[..., None], axis=-2)
    y = jnp.zeros_like(h)
    for e in range(N_EXPERTS):
        y = y + gates[..., e:e + 1].astype(h.dtype) * swiglu(h, wg[e], wu[e], wd[e])
    return y


def setup_inputs(seed: int = 0) -> dict:
    key = jax.random.key(seed)
    ks = iter(jax.random.split(key, 32))
    D = D_MODEL

    def nrm(shape, scale):
        return jax.random.normal(next(ks), shape, jnp.float32) * scale

    return {
        'x': nrm((BATCH, SEQ, D), 1.0),
        'c': nrm((BATCH, D), 1.0),
        'ctx': nrm((BATCH, CTX_LEN, D), 1.0),
        'c_ctx': nrm((D,), 1.0),
        'ada_w': nrm((DEPTH, D, 6 * D), 0.5 * D ** -0.5),
        'ada_b': nrm((DEPTH, 6 * D), 0.02),
        'norm_g': 1.0 + nrm((DEPTH, 4, D), 0.02),
        'ab_w_in': nrm((N_EVEN, D, AB_IN), D ** -0.5),
        'ab_w_out': nrm((N_EVEN, AB_MIX, D), AB_MIX ** -0.5),
        'a_sink': nrm((N_EVEN, A_HEADS), 0.5),
        'b_ig_bias': nrm((N_EVEN, 2, B_HEADS), 0.1),
        'b_fg_bias': 3.0 + 3.0 * jax.random.uniform(next(ks), (N_EVEN, 2, B_HEADS), jnp.float32),
        'b_norm_g': 1.0 + nrm((N_EVEN, B_HEADS * B_DV), 0.02),
        'c_w_qkv': nrm((N_ODD, D, C_IN), D ** -0.5),
        'c_w_out': nrm((N_ODD, C_MIX, D), C_MIX ** -0.5),
        'c_qk_g': 1.0 + nrm((N_ODD, 2, HEAD_DIM), 0.02),
        'ffn_w_gate': nrm((N_EVEN, D, D_FF), D ** -0.5),
        'ffn_w_up': nrm((N_EVEN, D, D_FF), D ** -0.5),
        'ffn_w_down': nrm((N_EVEN, D_FF, D), D_FF ** -0.5),
        'moe_router': nrm((N_ODD, D, N_EXPERTS), D ** -0.5),
        'moe_w_gate': nrm((N_ODD, N_EXPERTS, D, D_FF_EXPERT), D ** -0.5),
        'moe_w_up': nrm((N_ODD, N_EXPERTS, D, D_FF_EXPERT), D ** -0.5),
        'moe_w_down': nrm((N_ODD, N_EXPERTS, D_FF_EXPERT, D), D_FF_EXPERT ** -0.5),
    }


def reference(x, c, ctx, c_ctx, ada_w, ada_b, norm_g, ab_w_in, ab_w_out, a_sink, b_ig_bias, b_fg_bias,
              b_norm_g, c_w_qkv, c_w_out, c_qk_g, ffn_w_gate, ffn_w_up, ffn_w_down, moe_router,
              moe_w_gate, moe_w_up, moe_w_down):
    S = x.shape[1]
    cos, sin = axial_rope_tables(S)
    xl, xc = x, ctx
    s_lat = jax.nn.silu(c)
    s_ctx = jax.nn.silu(c_ctx)[None]
    for layer in range(DEPTH):
        last = layer == DEPTH - 1
        j = layer // 2
        ml = [m[:, None, :] for m in jnp.split(s_lat @ ada_w[layer] + ada_b[layer], 6, axis=-1)]
        mc = [m[:, None, :] for m in jnp.split(s_ctx @ ada_w[layer] + ada_b[layer], 6, axis=-1)]
        hl = rmsnorm(xl, norm_g[layer, 0]) * (1 + ml[1]) + ml[0]
        hc = rmsnorm(xc, norm_g[layer, 0]) * (1 + mc[1]) + mc[0]
        if layer % 2 == 0:
            yl, yc = mixer_ab(hl, hc, ab_w_in[j], ab_w_out[j], a_sink[j], b_ig_bias[j], b_fg_bias[j],
                              b_norm_g[j], cos, sin, not last)
        else:
            yl, yc = mixer_c(hl, hc, c_w_qkv[j], c_w_out[j], c_qk_g[j], cos, sin, not last)
        xl = xl + ml[2] * rmsnorm(yl, norm_g[layer, 1])
        if not last:
            xc = xc + mc[2] * rmsnorm(yc, norm_g[layer, 1])
        hl = rmsnorm(xl, norm_g[layer, 2]) * (1 + ml[4]) + ml[3]
        if last:
            h = hl
        else:
            hc = rmsnorm(xc, norm_g[layer, 2]) * (1 + mc[4]) + mc[3]
            h = jnp.concatenate([hc, hl], axis=1)
        if layer % 2 == 0:
            f = swiglu(h, ffn_w_gate[j], ffn_w_up[j], ffn_w_down[j])
        else:
            f = moe_swiglu(h, moe_router[j], moe_w_gate[j], moe_w_up[j], moe_w_down[j])
        xl = xl + ml[5] * rmsnorm(f[:, f.shape[1] - S:], norm_g[layer, 3])
        if not last:
            xc = xc + mc[5] * rmsnorm(f[:, :f.shape[1] - S], norm_g[layer, 3])
    return xl
```

```python
import functools

import jax
import jax.numpy as jnp
import numpy as np
from jax import lax
from jax.experimental import pallas as pl
from jax.experimental.pallas import tpu as pltpu

F32 = jnp.float32
BF16 = jnp.bfloat16
I32 = jnp.int32

EPS = 1e-6
GRID_W = 64
HEAD_DIM = 128
ROPE_THETA = 10000.0
ROPE_FREQS = HEAD_DIM // 4
BLK = 128
A_HEADS = 8
A_KV = 2
B_HEADS = 4
B_DK = 128
B_DV = 256
MLSTM_CHUNK = 128
C_HEADS = 16
C_KV = 4
N_EXPERTS = 8

A_Q = A_HEADS * HEAD_DIM
A_QKV = A_Q + 2 * A_KV * HEAD_DIM
B_Q = B_HEADS * B_DK
B_V = B_HEADS * B_DV
B_QKVO = 2 * B_Q + 2 * B_V
C_Q = C_HEADS * HEAD_DIM
C_KVW = C_KV * HEAD_DIM

TM = 256
VMEM_LIMIT = 56 * 1024 * 1024
NEG_INF = float("-inf")


def _params(n_axes, vmem=VMEM_LIMIT):
    return pltpu.CompilerParams(dimension_semantics=("arbitrary",) * n_axes, vmem_limit_bytes=vmem)


def _rms(x, g):
    return x * lax.rsqrt(jnp.mean(x * x, axis=-1, keepdims=True) + EPS) * g


def _dot(a, b):
    return jnp.dot(a, b, preferred_element_type=F32)


def _dot_nt(a, b):
    return lax.dot_general(a, b, (((1,), (1,)), ((), ())), preferred_element_type=F32)


def _ada_kernel(s_ref, w_ref, b_ref, o_ref):
    s = s_ref[...]
    s = s * jax.nn.sigmoid(s)
    o_ref[...] = _dot(s.astype(BF16), w_ref[...].astype(BF16)) + b_ref[...]


def _ada_mods(c, c_ctx, ada_w, ada_b):
    depth, d, d6 = ada_w.shape
    tn = 1536 if d6 % 1536 == 0 else d6
    s = jnp.zeros((8, d), F32).at[0].set(c[0]).at[1].set(c_ctx)
    out = pl.pallas_call(
        _ada_kernel,
        grid=(depth, d6 // tn),
        in_specs=[pl.BlockSpec((8, d), lambda l, n: (0, 0)),
                  pl.BlockSpec((None, d, tn), lambda l, n: (l, 0, n)),
                  pl.BlockSpec((None, 1, tn), lambda l, n: (l, 0, n))],
        out_specs=pl.BlockSpec((None, 8, tn), lambda l, n: (l, 0, n)),
        out_shape=jax.ShapeDtypeStruct((depth, 8, d6), F32),
        compiler_params=_params(2),
        name="ada_mods",
    )(s, ada_w, ada_b.reshape(depth, 1, d6))
    return out[:, :2].reshape(depth, 2, 6, d)


def _mod_spec(d, layer, n_ctx_tiles):
    return pl.BlockSpec((None, None, 6, d), lambda i: (layer, jnp.where(i < n_ctx_tiles, 1, 0), 0, 0))


def _prenorm_kernel(x_ref, g_ref, mod_ref, h_ref):
    hn = _rms(x_ref[...], g_ref[0:1, :])
    h_ref[...] = (hn * (1.0 + mod_ref[1:2, :]) + mod_ref[0:1, :]).astype(h_ref.dtype)


def _prenorm(x, norm_g, mods, layer, n_ctx_tiles):
    m, d = x.shape
    return pl.pallas_call(
        _prenorm_kernel,
        grid=(m // TM,),
        in_specs=[pl.BlockSpec((TM, d), lambda i: (i, 0)),
                  pl.BlockSpec((None, 4, d), lambda i: (layer, 0, 0)),
                  _mod_spec(d, layer, n_ctx_tiles)],
        out_specs=pl.BlockSpec((TM, d), lambda i: (i, 0)),
        out_shape=jax.ShapeDtypeStruct((m, d), BF16),
        compiler_params=_params(1),
        name="prenorm",
    )(x, norm_g, mods)


def _post_pre_math(x, y, gpost_ref, gpre_ref, mpost_ref, mpre_ref, xo_ref, h_ref, sub):
    g_row = 1 + 2 * sub
    gate_row = 2 + 3 * sub
    xn = x + mpost_ref[gate_row:gate_row + 1, :] * _rms(y, gpost_ref[g_row:g_row + 1, :])
    xo_ref[...] = xn
    if h_ref is not None:
        nsub = 1 - sub
        hn = _rms(xn, gpre_ref[2 * nsub:2 * nsub + 1, :])
        h_ref[...] = (hn * (1.0 + mpre_ref[3 * nsub + 1:3 * nsub + 2, :])
                      + mpre_ref[3 * nsub:3 * nsub + 1, :]).astype(h_ref.dtype)


def _post_pre_kernel(x_ref, y_ref, gpost_ref, gpre_ref, mpost_ref, mpre_ref, xo_ref, *h_ref, sub):
    _post_pre_math(x_ref[...], y_ref[...], gpost_ref, gpre_ref, mpost_ref, mpre_ref, xo_ref,
                   h_ref[0] if h_ref else None, sub)


def _post_pre(x, y, norm_g, mods, layer, sub, n_ctx_tiles, h_dtype):
    m, d = x.shape
    pre_layer = layer if sub == 0 else min(layer + 1, norm_g.shape[0] - 1)
    row = pl.BlockSpec((TM, d), lambda i: (i, 0))
    out_shape = [jax.ShapeDtypeStruct((m, d), F32)]
    out_specs = [row]
    if h_dtype is not None:
        out_shape.append(jax.ShapeDtypeStruct((m, d), h_dtype))
        out_specs.append(row)
    res = pl.pallas_call(
        functools.partial(_post_pre_kernel, sub=sub),
        grid=(m // TM,),
        in_specs=[row, row,
                  pl.BlockSpec((None, 4, d), lambda i: (layer, 0, 0)),
                  pl.BlockSpec((None, 4, d), lambda i: (pre_layer, 0, 0)),
                  _mod_spec(d, layer, n_ctx_tiles),
                  _mod_spec(d, pre_layer, n_ctx_tiles)],
        out_specs=out_specs,
        out_shape=out_shape,
        compiler_params=_params(1),
        name="post_pre",
    )(x, y, norm_g, norm_g, mods, mods)
    return res if h_dtype is not None else (res[0], None)


def _rope(y, cos, sin):
    lane = lax.broadcasted_iota(I32, y.shape, 1)
    lower = (lane & (2 * ROPE_FREQS - 1)) < ROPE_FREQS
    partner = jnp.where(lower, pltpu.roll(y, HEAD_DIM - ROPE_FREQS, 1), pltpu.roll(y, ROPE_FREQS, 1))
    return y * cos + partner * sin


def _proj_kernel(x_ref, w_ref, cos_ref, sin_ref, g_ref, o_ref, wb_ref, *, segments, tn):
    n = pl.program_id(0)

    @pl.when(pl.program_id(1) == 0)
    def _():
        wb_ref[...] = w_ref[...].astype(BF16)

    y = _dot(x_ref[...], wb_ref[...])
    for (t0, t1, kind, grow, scale) in segments:
        @pl.when((n >= t0) & (n < t1))
        def _(kind=kind, grow=grow, scale=scale):
            if kind == "plain":
                o_ref[...] = y.astype(o_ref.dtype)
                return
            for hh in range(tn // HEAD_DIM):
                sl = slice(hh * HEAD_DIM, (hh + 1) * HEAD_DIM)
                yh = y[:, sl]
                if kind == "norm_rope":
                    yh = _rms(yh, g_ref[grow:grow + 1, :])
                yh = _rope(yh, cos_ref[...], sin_ref[...])
                if scale != 1.0:
                    yh = yh * scale
                o_ref[:, sl] = yh.astype(o_ref.dtype)


def _proj(h, w3, layer_idx, col0, ncols, tn, segments, out_dtype, cos_t, sin_t, gains):
    m, k = h.shape
    assert col0 % tn == 0 and ncols % tn == 0
    n0 = col0 // tn
    return pl.pallas_call(
        functools.partial(_proj_kernel, segments=tuple(segments), tn=tn),
        grid=(ncols // tn, m // TM),
        in_specs=[pl.BlockSpec((TM, k), lambda n, i: (i, 0)),
                  pl.BlockSpec((None, k, tn), lambda n, i: (layer_idx, 0, n + n0)),
                  pl.BlockSpec((TM, HEAD_DIM), lambda n, i: (i, 0)),
                  pl.BlockSpec((TM, HEAD_DIM), lambda n, i: (i, 0)),
                  pl.BlockSpec((2, HEAD_DIM), lambda n, i: (0, 0))],
        out_specs=pl.BlockSpec((TM, tn), lambda n, i: (i, n)),
        out_shape=jax.ShapeDtypeStruct((m, ncols), out_dtype),
        scratch_shapes=[pltpu.VMEM((k, tn), BF16)],
        compiler_params=_params(2),
        name="proj",
    )(h, w3, cos_t, sin_t, gains)


def _win_kernel(sink_ref, q_ref, kp_ref, kc_ref, kn_ref, kx_ref, vp_ref, vc_ref, vn_ref, vx_ref, o_ref, *, group):
    kvh = pl.program_id(0)
    i = pl.program_id(1)
    nb = pl.num_programs(1)
    qi = lax.broadcasted_iota(I32, (BLK, BLK), 0)
    kj = lax.broadcasted_iota(I32, (BLK, BLK), 1)
    mask_p = (kj - qi) >= jnp.where(i > 0, 0, BLK)
    mask_n = (qi - kj) >= jnp.where(i < nb - 1, 0, BLK)
    kp, kc, kn, kx = kp_ref[...], kc_ref[...], kn_ref[...], kx_ref[...]
    vp, vc, vn, vx = vp_ref[...], vc_ref[...], vn_ref[...], vx_ref[...]
    for g in range(group):
        sl = slice(g * HEAD_DIM, (g + 1) * HEAD_DIM)
        q = q_ref[:, sl]
        sp = jnp.where(mask_p, _dot_nt(q, kp), NEG_INF)
        sc = _dot_nt(q, kc)
        sn = jnp.where(mask_n, _dot_nt(q, kn), NEG_INF)
        sx = _dot_nt(q, kx)
        snk = sink_ref[kvh * group + g]
        mx = jnp.maximum(jnp.maximum(jnp.max(sp, -1, keepdims=True), jnp.max(sc, -1, keepdims=True)),
                         jnp.maximum(jnp.max(sn, -1, keepdims=True), jnp.max(sx, -1, keepdims=True)))
        mx = jnp.maximum(mx, snk)
        pp, pc, pn, px = jnp.exp(sp - mx), jnp.exp(sc - mx), jnp.exp(sn - mx), jnp.exp(sx - mx)
        den = (jnp.sum(pp, -1, keepdims=True) + jnp.sum(pc, -1, keepdims=True)
               + jnp.sum(pn, -1, keepdims=True) + jnp.sum(px, -1, keepdims=True) + jnp.exp(snk - mx))
        o = (_dot(pp.astype(BF16), vp) + _dot(pc.astype(BF16), vc)
             + _dot(pn.astype(BF16), vn) + _dot(px.astype(BF16), vx))
        o_ref[:, sl] = (o / den).astype(o_ref.dtype)


def _window_attn(qkv, sink, n_ctx):
    m = qkv.shape[0]
    nb = (m - n_ctx) // BLK
    cb = n_ctx // BLK
    group = A_HEADS // A_KV
    kcol = A_Q // HEAD_DIM
    vcol = kcol + A_KV

    def kv_spec(col, shift):
        return pl.BlockSpec((BLK, HEAD_DIM),
                            lambda h, i, s: (cb + jnp.clip(i + shift, 0, nb - 1), col + h))

    def ctx_spec(col):
        return pl.BlockSpec((n_ctx, HEAD_DIM), lambda h, i, s: (0, col + h))

    grid_spec = pltpu.PrefetchScalarGridSpec(
        num_scalar_prefetch=1,
        grid=(A_KV, nb),
        in_specs=[pl.BlockSpec((BLK, group * HEAD_DIM), lambda h, i, s: (cb + i, h)),
                  kv_spec(kcol, -1), kv_spec(kcol, 0), kv_spec(kcol, 1), ctx_spec(kcol),
                  kv_spec(vcol, -1), kv_spec(vcol, 0), kv_spec(vcol, 1), ctx_spec(vcol)],
        out_specs=pl.BlockSpec((BLK, group * HEAD_DIM), lambda h, i, s: (i, h)),
    )
    return pl.pallas_call(
        functools.partial(_win_kernel, group=group),
        grid_spec=grid_spec,
        out_shape=jax.ShapeDtypeStruct((m - n_ctx, A_Q), BF16),
        compiler_params=_params(2),
        name="window_attn",
    )(sink, qkv, qkv, qkv, qkv, qkv, qkv, qkv, qkv, qkv)


def _flash_kernel(sink_ref, q_ref, k_ref, v_ref, o_ref, m_ref, l_ref, acc_ref, *, group, tq, tk, n_keys, has_sink):
    kvh = pl.program_id(0)
    rows = group * tq
    q = jnp.concatenate([q_ref[:, g * HEAD_DIM:(g + 1) * HEAD_DIM] for g in range(group)], axis=0)
    m_ref[...] = jnp.full((rows, 1), NEG_INF, F32)
    l_ref[...] = jnp.zeros((rows, 1), F32)
    acc_ref[...] = jnp.zeros((rows, HEAD_DIM), F32)

    def body(j, carry):
        off = pl.multiple_of(j * tk, tk)
        k = k_ref[pl.ds(off, tk), :]
        v = v_ref[pl.ds(off, tk), :]
        s = _dot_nt(q, k)
        m_prev = m_ref[...]
        m_new = jnp.maximum(m_prev, jnp.max(s, -1, keepdims=True))
        alpha = jnp.exp(m_prev - m_new)
        p = jnp.exp(s - m_new)
        l_ref[...] = alpha * l_ref[...] + jnp.sum(p, -1, keepdims=True)
        acc_ref[...] = alpha * acc_ref[...] + _dot(p.astype(BF16), v)
        m_ref[...] = m_new
        return carry

    lax.fori_loop(0, n_keys // tk, body, 0)
    for g in range(group):
        rs = slice(g * tq, (g + 1) * tq)
        m_fin = m_ref[rs, :]
        l_fin = l_ref[rs, :]
        acc = acc_ref[rs, :]
        if has_sink:
            snk = sink_ref[kvh * group + g]
            m_tot = jnp.maximum(m_fin, snk)
            scale = jnp.exp(m_fin - m_tot)
            l_fin = l_fin * scale + jnp.exp(snk - m_tot)
            acc = acc * scale
        o_ref[:, g * HEAD_DIM:(g + 1) * HEAD_DIM] = (acc / l_fin).astype(o_ref.dtype)


def _flash_attn(q_arr, q_col0, kv_arr, k_col0, v_col0, n_kv, group, sink, q_row0, n_q, n_keys, tq, tk):
    assert n_q % tq == 0 and q_row0 % tq == 0 and n_keys % tk == 0
    has_sink = sink is not None
    if sink is None:
        sink = jnp.zeros((n_kv * group,), F32)
    qb0 = q_row0 // tq
    qc0 = q_col0 // (group * HEAD_DIM)
    kc0 = k_col0 // HEAD_DIM
    vc0 = v_col0 // HEAD_DIM
    rows = group * tq
    grid_spec = pltpu.PrefetchScalarGridSpec(
        num_scalar_prefetch=1,
        grid=(n_kv, n_q // tq),
        in_specs=[pl.BlockSpec((tq, group * HEAD_DIM), lambda h, i, s: (qb0 + i, qc0 + h)),
                  pl.BlockSpec((n_keys, HEAD_DIM), lambda h, i, s: (0, kc0 + h)),
                  pl.BlockSpec((n_keys, HEAD_DIM), lambda h, i, s: (0, vc0 + h))],
        out_specs=pl.BlockSpec((tq, group * HEAD_DIM), lambda h, i, s: (i, h)),
        scratch_shapes=[pltpu.VMEM((rows, 1), F32), pltpu.VMEM((rows, 1), F32),
                        pltpu.VMEM((rows, HEAD_DIM), F32)],
    )
    return pl.pallas_call(
        functools.partial(_flash_kernel, group=group, tq=tq, tk=tk, n_keys=n_keys, has_sink=has_sink),
        grid_spec=grid_spec,
        out_shape=jax.ShapeDtypeStruct((n_q, n_kv * group * HEAD_DIM), BF16),
        compiler_params=_params(2),
        name="flash_attn",
    )(sink, q_arr, kv_arr, kv_arr)


def _log_sigmoid(x):
    return jnp.minimum(x, 0.0) - jnp.log(1.0 + jnp.exp(-jnp.abs(x)))


def _dot_exact(a, b):
    return jnp.dot(a, b, preferred_element_type=F32, precision=lax.Precision.HIGHEST)


def _mlstm_head(q, k, v, i_col, i_row, b_col, b_row, b_tot, mask, c_ref, n_ref, m_ref):
    m_prev = m_ref[:, 0:1]
    c_prev = c_ref[...]
    n_prev = n_ref[...]
    qs = q * (B_DK ** -0.5)
    qb = qs.astype(BF16)
    dlog = jnp.where(mask, b_col - b_row + i_row, NEG_INF)
    inter = b_col + m_prev
    m_t = jnp.maximum(inter, jnp.max(dlog, -1, keepdims=True))
    dw = jnp.exp(dlog - m_t)
    iw = jnp.exp(inter - m_t)
    sc = _dot_nt(qb, k.astype(BF16)) * dw
    num = _dot(sc.astype(BF16), v.astype(BF16)) + iw * _dot(qb, c_prev.astype(BF16))
    den = jnp.sum(sc, -1, keepdims=True) + iw * jnp.sum(qs * n_prev, -1, keepdims=True)
    h = num / jnp.maximum(jnp.abs(den), jnp.exp(-m_t))
    glog_col = b_tot - b_col + i_col
    glog_row = b_tot - b_row + i_row
    m_new = jnp.maximum(b_tot + m_prev, jnp.max(glog_row, -1, keepdims=True))
    decay = jnp.exp(b_tot + m_prev - m_new)
    wk = jnp.exp(glog_col - m_new) * k
    c_ref[...] = decay * c_prev + lax.dot_general(wk.astype(BF16), v.astype(BF16), (((0,), (0,)), ((), ())),
                                                  preferred_element_type=F32)
    n_ref[...] = decay * n_prev + jnp.sum(wk, 0, keepdims=True)
    m_ref[...] = jnp.broadcast_to(m_new, m_ref.shape)
    return h


def _mlstm_kernel(brow_ref, bcol_ref,
                  qf_ref, kf_ref, vf_ref, gf_ref, gtf_ref,
                  qb_ref, kb_ref, vb_ref, gb_ref, gtb_ref,
                  of_ref, ob_ref, c_ref, n_ref, m_ref):
    L = MLSTM_CHUNK

    @pl.when(pl.program_id(0) == 0)
    def _():
        c_ref[...] = jnp.zeros(c_ref.shape, F32)
        n_ref[...] = jnp.zeros(n_ref.shape, F32)
        m_ref[...] = jnp.zeros(m_ref.shape, F32)

    r = lax.broadcasted_iota(I32, (L, L), 0)
    cidx = lax.broadcasted_iota(I32, (L, L), 1)
    lane = lax.broadcasted_iota(I32, (L, HEAD_DIM), 1)
    sub = lax.broadcasted_iota(I32, (2 * 2 * B_HEADS, L), 0)
    for d, (q_ref, k_ref, v_ref, g_ref, gt_ref, o_ref) in enumerate(
            ((qf_ref, kf_ref, vf_ref, gf_ref, gtf_ref, of_ref), (qb_ref, kb_ref, vb_ref, gb_ref, gtb_ref, ob_ref))):
        mask = (cidx <= r) if d == 0 else (cidx >= r)
        gc = g_ref[...] + brow_ref[...]
        gc = jnp.where(lane < 2 * B_HEADS, gc, _log_sigmoid(gc))
        gr = gt_ref[...] + bcol_ref[...]
        gr = jnp.where(sub < 2 * B_HEADS, gr, _log_sigmoid(gr))
        mask_t = (r <= cidx) if d == 0 else (r >= cidx)
        bc_all = _dot_exact(mask.astype(F32), gc)
        br_all = _dot_exact(gr, mask_t.astype(F32))
        edge = L - 1 if d == 0 else 0
        for hd in range(B_HEADS):
            ci = d * B_HEADS + hd
            cf = 2 * B_HEADS + ci
            b_col = bc_all[:, cf:cf + 1]
            h = _mlstm_head(q_ref[:, hd * B_DK:(hd + 1) * B_DK], k_ref[:, hd * B_DK:(hd + 1) * B_DK],
                            v_ref[:, hd * B_DV:(hd + 1) * B_DV],
                            gc[:, ci:ci + 1], gr[ci:ci + 1, :], b_col, br_all[cf:cf + 1, :],
                            b_col[edge:edge + 1, :], mask, c_ref.at[ci], n_ref.at[ci], m_ref.at[ci])
            o_ref[:, hd * B_DV:(hd + 1) * B_DV] = h


def _mlstm(qkvo, gates, gates_t, bias_row, bias_col, n_ctx):
    m = qkvo.shape[0]
    L = MLSTM_CHUNK
    nc = m // L
    ncc = n_ctx // L
    kq = B_Q // B_Q
    kv = (2 * B_Q) // B_V

    def fw(s):
        return s

    def bw(s):
        return jnp.where(s < ncc, ncc - 1 - s, nc - 1 + ncc - s)

    def specs(order):
        return [pl.BlockSpec((L, B_Q), lambda s: (order(s), 0)),
                pl.BlockSpec((L, B_Q), lambda s: (order(s), kq)),
                pl.BlockSpec((L, B_V), lambda s: (order(s), kv)),
                pl.BlockSpec((L, HEAD_DIM), lambda s: (order(s), 0)),
                pl.BlockSpec((4 * B_HEADS, L), lambda s: (0, order(s)))]

    nst = 2 * B_HEADS
    return pl.pallas_call(
        _mlstm_kernel,
        grid=(nc,),
        in_specs=[pl.BlockSpec((1, HEAD_DIM), lambda s: (0, 0)),
                  pl.BlockSpec((4 * B_HEADS, 1), lambda s: (0, 0))] + specs(fw) + specs(bw),
        out_specs=[pl.BlockSpec((L, B_V), lambda s: (fw(s), 0)),
                   pl.BlockSpec((L, B_V), lambda s: (bw(s), 0))],
        out_shape=[jax.ShapeDtypeStruct((m, B_V), F32), jax.ShapeDtypeStruct((m, B_V), F32)],
        scratch_shapes=[pltpu.VMEM((nst, B_DK, B_DV), F32), pltpu.VMEM((nst, 1, B_DK), F32),
                        pltpu.VMEM((nst, 1, HEAD_DIM), F32)],
        compiler_params=_params(1),
        name="mlstm",
    )(bias_row, bias_col, qkvo, qkvo, qkvo, gates, gates_t, qkvo, qkvo, qkvo, gates, gates_t)


def _bout_kernel(hf_ref, hb_ref, o_ref, g_ref, y_ref):
    for hd in range(B_HEADS):
        sl = slice(hd * B_DV, (hd + 1) * B_DV)
        hn = _rms(hf_ref[:, sl] + hb_ref[:, sl], g_ref[:, sl])
        y_ref[:, sl] = (jax.nn.sigmoid(o_ref[:, sl]) * hn).astype(y_ref.dtype)


def _b_out(hf, hb, qkvo, out_g):
    m = hf.shape[0]
    ocol = (2 * B_Q + B_V) // B_V
    row = pl.BlockSpec((TM, B_V), lambda i: (i, 0))
    return pl.pallas_call(
        _bout_kernel,
        grid=(m // TM,),
        in_specs=[row, row, pl.BlockSpec((TM, B_V), lambda i: (i, ocol)),
                  pl.BlockSpec((1, B_V), lambda i: (0, 0))],
        out_specs=row,
        out_shape=jax.ShapeDtypeStruct((m, B_V), BF16),
        compiler_params=_params(1),
        name="mlstm_out",
    )(hf, hb, qkvo, out_g)


def _glu_kernel(te_ref, tf_ref, tv_ref, x_ref, wg_ref, wu_ref, o_ref, wgb_ref, wub_ref):
    t = pl.program_id(1)

    @pl.when(tf_ref[t] == 1)
    def _():
        wgb_ref[...] = wg_ref[...].astype(BF16)
        wub_ref[...] = wu_ref[...].astype(BF16)

    @pl.when(tv_ref[t] == 1)
    def _():
        x = x_ref[...]
        a = _dot(x, wgb_ref[...])
        b = _dot(x, wub_ref[...])
        o_ref[...] = (a * jax.nn.sigmoid(a) * b).astype(o_ref.dtype)

    @pl.when(tv_ref[t] == 0)
    def _():
        o_ref[...] = jnp.zeros(o_ref.shape, o_ref.dtype)


def _glu(x, wg, wu, e0, tile_e, tile_first, tile_valid, tn):
    p, k = x.shape
    f = wg.shape[-1]
    assert f % tn == 0

    def wspec():
        return pl.BlockSpec((None, k, tn), lambda n, t, te, tf, tv: (e0 + te[t], 0, n))

    grid_spec = pltpu.PrefetchScalarGridSpec(
        num_scalar_prefetch=3,
        grid=(f // tn, p // TM),
        in_specs=[pl.BlockSpec((TM, k), lambda n, t, te, tf, tv: (t, 0)), wspec(), wspec()],
        out_specs=pl.BlockSpec((TM, tn), lambda n, t, te, tf, tv: (t, n)),
        scratch_shapes=[pltpu.VMEM((k, tn), BF16), pltpu.VMEM((k, tn), BF16)],
    )
    return pl.pallas_call(
        _glu_kernel,
        grid_spec=grid_spec,
        out_shape=jax.ShapeDtypeStruct((p, f), BF16),
        compiler_params=_params(2),
        name="glu",
    )(tile_e, tile_first, tile_valid, x, wg, wu)


def _down_kernel(te_ref, tf_ref, tv_ref, a_ref, wd_ref, *rest, weighted):
    if weighted:
        rw_ref, o_ref, wdb_ref = rest
    else:
        o_ref, wdb_ref = rest
    t = pl.program_id(1)

    @pl.when(tf_ref[t] == 1)
    def _():
        wdb_ref[...] = wd_ref[...].astype(BF16)

    @pl.when(tv_ref[t] == 1)
    def _():
        y = _dot(a_ref[...], wdb_ref[...])
        if weighted:
            y = y * rw_ref[...]
        o_ref[...] = y

    @pl.when(tv_ref[t] == 0)
    def _():
        o_ref[...] = jnp.zeros(o_ref.shape, o_ref.dtype)


def _down(a, wd, e0, tile_e, tile_first, tile_valid, row_w, tn):
    p, f = a.shape
    d = wd.shape[-1]
    assert d % tn == 0
    weighted = row_w is not None
    in_specs = [pl.BlockSpec((TM, f), lambda n, t, te, tf, tv: (t, 0)),
                pl.BlockSpec((None, f, tn), lambda n, t, te, tf, tv: (e0 + te[t], 0, n))]
    args = [a, wd]
    if weighted:
        in_specs.append(pl.BlockSpec((TM, 1), lambda n, t, te, tf, tv: (t, 0)))
        args.append(row_w)
    grid_spec = pltpu.PrefetchScalarGridSpec(
        num_scalar_prefetch=3,
        grid=(d // tn, p // TM),
        in_specs=in_specs,
        out_specs=pl.BlockSpec((TM, tn), lambda n, t, te, tf, tv: (t, n)),
        scratch_shapes=[pltpu.VMEM((f, tn), BF16)],
    )
    return pl.pallas_call(
        functools.partial(_down_kernel, weighted=weighted),
        grid_spec=grid_spec,
        out_shape=jax.ShapeDtypeStruct((p, d), F32),
        compiler_params=_params(2),
        name="down",
    )(tile_e, tile_first, tile_valid, *args)


def _dense_tiles(n_tiles):
    first = jnp.zeros((n_tiles,), I32).at[0].set(1)
    return jnp.zeros((n_tiles,), I32), first, jnp.ones((n_tiles,), I32)


def _router_kernel(h_ref, r_ref, idx_ref, w_ref):
    logits = _dot(h_ref[...].astype(BF16), r_ref[...].astype(BF16))
    lane = lax.broadcasted_iota(I32, logits.shape, 1).astype(F32)
    big = float(HEAD_DIM)
    lg = jnp.where(lane < N_EXPERTS, logits, NEG_INF)
    m1 = jnp.max(lg, -1, keepdims=True)
    i1 = jnp.min(jnp.where(lg == m1, lane, big), -1, keepdims=True)
    lg2 = jnp.where(lane == i1, NEG_INF, lg)
    m2 = jnp.max(lg2, -1, keepdims=True)
    i2 = jnp.min(jnp.where(lg2 == m2, lane, big), -1, keepdims=True)
    e2 = jnp.exp(m2 - m1)
    den = 1.0 + e2
    idx_ref[...] = jnp.where(lane == 0.0, i1, jnp.where(lane == 1.0, i2, 0.0)).astype(I32)
    w_ref[...] = jnp.where(lane == 0.0, 1.0 / den, jnp.where(lane == 1.0, e2 / den, 0.0))


def _router(h, router_p):
    m, d = h.shape
    row = pl.BlockSpec((TM, HEAD_DIM), lambda i: (i, 0))
    return pl.pallas_call(
        _router_kernel,
        grid=(m // TM,),
        in_specs=[pl.BlockSpec((TM, d), lambda i: (i, 0)), pl.BlockSpec((d, HEAD_DIM), lambda i: (0, 0))],
        out_specs=[row, row],
        out_shape=[jax.ShapeDtypeStruct((m, HEAD_DIM), I32), jax.ShapeDtypeStruct((m, HEAD_DIM), F32)],
        compiler_params=_params(1),
        name="router",
    )(h, router_p)


def _moe_plan(eidx, ew, n_tiles):
    m = eidx.shape[0]
    e_flat = eidx.reshape(-1)
    onehot = (e_flat[:, None] == jnp.arange(N_EXPERTS, dtype=I32)[None, :]).astype(I32)
    csum = jnp.cumsum(onehot, axis=0)
    rank = jnp.take_along_axis(csum, e_flat[:, None], axis=1)[:, 0] - 1
    counts = csum[-1]
    tiles_e = (counts + TM - 1) // TM
    tile_end = jnp.cumsum(tiles_e)
    pos = (tile_end - tiles_e)[e_flat] * TM + rank
    token = jnp.arange(2 * m, dtype=I32) // 2
    src = jnp.zeros((n_tiles * TM,), I32).at[pos].set(token)
    row_w = jnp.zeros((n_tiles * TM,), F32).at[pos].set(ew.reshape(-1))
    tids = jnp.arange(n_tiles, dtype=I32)
    valid = tids < tile_end[-1]
    tile_e = jnp.searchsorted(tile_end, tids, side="right").astype(I32)
    last_e = jnp.max(jnp.where(counts > 0, jnp.arange(N_EXPERTS, dtype=I32), 0))
    tile_e = jnp.where(valid, tile_e, last_e)
    first = jnp.concatenate([jnp.ones((1,), I32), (tile_e[1:] != tile_e[:-1]).astype(I32)])
    pos2 = pos.reshape(m, 2).astype(I32)
    return src, row_w.reshape(-1, 1), tile_e, first, valid.astype(I32), pos2[:, 0], pos2[:, 1]


def _gather_kernel(src_ref, tv_ref, h_hbm, o_ref, buf_ref, sem):
    t = pl.program_id(0)

    def row_copy(row, r):
        return pltpu.make_async_copy(h_hbm.at[pl.ds(row, 1), :], buf_ref.at[pl.ds(r, 1), :], sem)

    @pl.when(tv_ref[t] == 1)
    def _():
        def issue(r, c):
            row_copy(src_ref[t * TM + r], r).start()
            return c

        lax.fori_loop(0, TM, issue, 0)

        def wait(r, c):
            row_copy(0, r).wait()
            return c

        lax.fori_loop(0, TM, wait, 0)
        o_ref[...] = buf_ref[...].astype(o_ref.dtype)

    @pl.when(tv_ref[t] == 0)
    def _():
        o_ref[...] = jnp.zeros(o_ref.shape, o_ref.dtype)


def _gather_rows(h, src, tile_valid):
    d = h.shape[1]
    p = src.shape[0]
    grid_spec = pltpu.PrefetchScalarGridSpec(
        num_scalar_prefetch=2,
        grid=(p // TM,),
        in_specs=[pl.BlockSpec(memory_space=pl.ANY)],
        out_specs=pl.BlockSpec((TM, d), lambda t, s, tv: (t, 0)),
        scratch_shapes=[pltpu.VMEM((TM, d), F32), pltpu.SemaphoreType.DMA(())],
    )
    return pl.pallas_call(
        _gather_kernel,
        grid_spec=grid_spec,
        out_shape=jax.ShapeDtypeStruct((p, d), BF16),
        compiler_params=_params(1),
        name="moe_gather",
    )(src, tile_valid, h)


def _combine_kernel(p0_ref, p1_ref, ys_hbm, x_ref, gpost_ref, gpre_ref, mpost_ref, mpre_ref, xo_ref, *rest,
                    with_pre):
    if with_pre:
        h_ref, buf_ref, sem = rest
    else:
        h_ref = None
        buf_ref, sem = rest
    t = pl.program_id(0)

    def row_copy(row, k, r):
        return pltpu.make_async_copy(ys_hbm.at[pl.ds(row, 1), :], buf_ref.at[k, pl.ds(r, 1), :], sem)

    def issue(r, c):
        row_copy(p0_ref[t * TM + r], 0, r).start()
        row_copy(p1_ref[t * TM + r], 1, r).start()
        return c

    lax.fori_loop(0, TM, issue, 0)

    def wait(r, c):
        row_copy(0, 0, r).wait()
        row_copy(0, 1, r).wait()
        return c

    lax.fori_loop(0, TM, wait, 0)
    _post_pre_math(x_ref[...], buf_ref[0] + buf_ref[1], gpost_ref, gpre_ref, mpost_ref, mpre_ref, xo_ref, h_ref, 1)


def _combine_post_pre(x, ys, pos0, pos1, norm_g, mods, layer, n_ctx_tiles, h_dtype):
    m, d = x.shape
    pre_layer = min(layer + 1, norm_g.shape[0] - 1)
    with_pre = h_dtype is not None
    row = pl.BlockSpec((TM, d), lambda i, a, b: (i, 0))

    def mod_spec(l):
        return pl.BlockSpec((None, None, 6, d), lambda i, a, b: (l, jnp.where(i < n_ctx_tiles, 1, 0), 0, 0))

    out_shape = [jax.ShapeDtypeStruct((m, d), F32)]
    out_specs = [row]
    if with_pre:
        out_shape.append(jax.ShapeDtypeStruct((m, d), h_dtype))
        out_specs.append(row)
    grid_spec = pltpu.PrefetchScalarGridSpec(
        num_scalar_prefetch=2,
        grid=(m // TM,),
        in_specs=[pl.BlockSpec(memory_space=pl.ANY), row,
                  pl.BlockSpec((None, 4, d), lambda i, a, b: (layer, 0, 0)),
                  pl.BlockSpec((None, 4, d), lambda i, a, b: (pre_layer, 0, 0)),
                  mod_spec(layer), mod_spec(pre_layer)],
        out_specs=out_specs,
        scratch_shapes=[pltpu.VMEM((2, TM, d), F32), pltpu.SemaphoreType.DMA(())],
    )
    res = pl.pallas_call(
        functools.partial(_combine_kernel, with_pre=with_pre),
        grid_spec=grid_spec,
        out_shape=out_shape,
        compiler_params=_params(1),
        name="moe_combine",
    )(pos0, pos1, ys, x, norm_g, norm_g, mods, mods)
    return res if with_pre else (res[0], None)


def _rope_tables(n_ctx, s):
    pos = jnp.arange(s)
    row = (pos // GRID_W).astype(F32)
    col = (pos % GRID_W).astype(F32)
    inv = ROPE_THETA ** (-jnp.arange(ROPE_FREQS, dtype=F32) / ROPE_FREQS)
    ar, ac = row[:, None] * inv, col[:, None] * inv
    cos = jnp.concatenate([jnp.cos(ar), jnp.cos(ar), jnp.cos(ac), jnp.cos(ac)], axis=-1)
    sin = jnp.concatenate([-jnp.sin(ar), jnp.sin(ar), -jnp.sin(ac), jnp.sin(ac)], axis=-1)
    cos = jnp.concatenate([jnp.ones((n_ctx, HEAD_DIM), F32), cos], axis=0)
    sin = jnp.concatenate([jnp.zeros((n_ctx, HEAD_DIM), F32), sin], axis=0)
    return cos, sin


def _layer_ab(h, j, ab_w_in, a_sink, b_ig_bias, b_fg_bias, b_norm_g, cos_t, sin_t, n_ctx):
    scale = HEAD_DIM ** -0.5
    no_g = jnp.ones((2, HEAD_DIM), F32)
    tn = 256
    qt, kt = A_Q // tn, (A_Q + A_KV * HEAD_DIM) // tn
    a_qkv = _proj(h, ab_w_in, j, 0, A_QKV, tn,
                  [(0, qt, "rope", 0, scale), (qt, kt, "rope", 0, 1.0), (kt, A_QKV // tn, "plain", 0, 1.0)],
                  BF16, cos_t, sin_t, no_g)
    b_qkvo = _proj(h, ab_w_in, j, A_QKV, B_QKVO, 512, [(0, B_QKVO // 512, "plain", 0, 1.0)],
                   F32, cos_t, sin_t, no_g)
    ngate = 4 * B_HEADS
    w_gate = jnp.pad(ab_w_in[j, :, A_QKV + B_QKVO:], ((0, 0), (0, HEAD_DIM - ngate)))[None]
    gates = _proj(h, w_gate, 0, 0, HEAD_DIM, HEAD_DIM, [(0, 1, "plain", 0, 1.0)], F32, cos_t, sin_t, no_g)

    ya_l = _window_attn(a_qkv, a_sink[j], n_ctx)
    ya_c = _flash_attn(a_qkv, 0, a_qkv, A_Q, A_Q + A_KV * HEAD_DIM, A_KV, A_HEADS // A_KV, a_sink[j],
                       0, n_ctx, n_ctx, n_ctx, n_ctx)
    ya = jnp.concatenate([ya_c, ya_l], axis=0)

    bias = jnp.concatenate([b_ig_bias[j].reshape(-1), b_fg_bias[j].reshape(-1)]).astype(F32)
    bias_row = jnp.pad(bias, (0, HEAD_DIM - ngate))[None, :]
    hf, hb = _mlstm(b_qkvo, gates, jnp.transpose(gates[:, :ngate]), bias_row, bias[:, None], n_ctx)
    yb = _b_out(hf, hb, b_qkvo, b_norm_g[j][None, :])
    return jnp.concatenate([ya, yb], axis=1)


def _layer_c(h, j, c_w_qkv, c_qk_g, cos_t, sin_t, n_ctx, need_ctx):
    scale = HEAD_DIM ** -0.5
    tn = 512
    qt, kt = C_Q // tn, (C_Q + C_KVW) // tn
    qkv = _proj(h, c_w_qkv, j, 0, C_Q + 2 * C_KVW, tn,
                [(0, qt, "norm_rope", 0, scale), (qt, kt, "norm_rope", 1, 1.0), (kt, kt + C_KVW // tn, "plain", 0, 1.0)],
                BF16, cos_t, sin_t, c_qk_g[j])
    m = h.shape[0]
    group = C_HEADS // C_KV
    tk = 768 if m % 768 == 0 else TM
    y_l = _flash_attn(qkv, 0, qkv, C_Q, C_Q + C_KVW, C_KV, group, None, n_ctx, m - n_ctx, m, TM, tk)
    if need_ctx:
        y_c = _flash_attn(qkv, 0, qkv, C_Q, C_Q + C_KVW, C_KV, group, None, 0, n_ctx, n_ctx, n_ctx, n_ctx)
    else:
        y_c = jnp.zeros((n_ctx, C_Q), BF16)
    return jnp.concatenate([y_c, y_l], axis=0)


def kernel(x, c, ctx, c_ctx, ada_w, ada_b, norm_g, ab_w_in, ab_w_out, a_sink, b_ig_bias, b_fg_bias, b_norm_g,
           c_w_qkv, c_w_out, c_qk_g, ffn_w_gate, ffn_w_up, ffn_w_down, moe_router, moe_w_gate, moe_w_up,
           moe_w_down):
    depth = ada_w.shape[0]
    s, d = x.shape[1], x.shape[2]
    n_ctx = ctx.shape[1]
    m = n_ctx + s
    assert x.shape[0] == 1 and n_ctx % TM == 0 and s % TM == 0
    n_ctx_tiles = n_ctx // TM
    n_tiles = m // TM
    no_g = jnp.ones((2, HEAD_DIM), F32)

    mods = _ada_mods(c, c_ctx, ada_w, ada_b)
    cos_t, sin_t = _rope_tables(n_ctx, s)
    xs = jnp.concatenate([ctx[0], x[0]], axis=0)
    h = _prenorm(xs, norm_g, mods, 0, n_ctx_tiles)
    dense_tiles = _dense_tiles(n_tiles)
    n_exp = moe_w_gate.shape[1]
    moe_g = moe_w_gate.reshape((-1,) + moe_w_gate.shape[2:])
    moe_u = moe_w_up.reshape((-1,) + moe_w_up.shape[2:])
    moe_d = moe_w_down.reshape((-1,) + moe_w_down.shape[2:])

    for layer in range(depth):
        last = layer == depth - 1
        j = layer // 2
        even = layer % 2 == 0
        if even:
            mix = _layer_ab(h, j, ab_w_in, a_sink, b_ig_bias, b_fg_bias, b_norm_g, cos_t, sin_t, n_ctx)
            w_out = ab_w_out
        else:
            mix = _layer_c(h, j, c_w_qkv, c_qk_g, cos_t, sin_t, n_ctx, not last)
            w_out = c_w_out
        y = _proj(mix, w_out, j, 0, d, 512 if d % 512 == 0 else d, [(0, max(d // 512, 1), "plain", 0, 1.0)],
                  F32, cos_t, sin_t, no_g)
        xs, h = _post_pre(xs, y, norm_g, mods, layer, 0, n_ctx_tiles, BF16 if even else F32)
        if even:
            f_dim = ffn_w_gate.shape[-1]
            act = _glu(h, ffn_w_gate, ffn_w_up, j, *dense_tiles, 512 if f_dim % 512 == 0 else f_dim)
            f = _down(act, ffn_w_down, j, *dense_tiles, None, 512 if d % 512 == 0 else d)
            xs, h = _post_pre(xs, f, norm_g, mods, layer, 1, n_ctx_tiles, None if last else BF16)
        else:
            router_p = jnp.pad(moe_router[j], ((0, 0), (0, HEAD_DIM - n_exp)))
            eidx, ew = _router(h, router_p)
            n_slots_tiles = 2 * m // TM + n_exp
            src, row_w, tile_e, tile_first, tile_valid, pos0, pos1 = _moe_plan(eidx[:, :2], ew[:, :2], n_slots_tiles)
            x_s = _gather_rows(h, src, tile_valid)
            f_dim = moe_g.shape[-1]
            act = _glu(x_s, moe_g, moe_u, j * n_exp, tile_e, tile_first, tile_valid,
                       512 if f_dim % 512 == 0 else f_dim)
            y_s = _down(act, moe_d, j * n_exp, tile_e, tile_first, tile_valid, row_w, 512 if d % 512 == 0 else d)
            xs, h = _combine_post_pre(xs, y_s, pos0, pos1, norm_g, mods, layer, n_ctx_tiles, None if last else BF16)
    return xs[n_ctx:][None]
```

```python
import functools

import jax
import jax.numpy as jnp
import numpy as np
from jax import lax
from jax.experimental import pallas as pl
from jax.experimental.pallas import tpu as pltpu

F32 = jnp.float32
BF16 = jnp.bfloat16
I32 = jnp.int32

EPS = 1e-6
GRID_W = 64
HEAD_DIM = 128
ROPE_THETA = 10000.0
ROPE_FREQS = HEAD_DIM // 4
BLK = 128
A_HEADS = 8
A_KV = 2
B_HEADS = 4
B_DK = 128
B_DV = 256
MLSTM_CHUNK = 128
C_HEADS = 16
C_KV = 4
N_EXPERTS = 8

A_Q = A_HEADS * HEAD_DIM
A_QKV = A_Q + 2 * A_KV * HEAD_DIM
B_Q = B_HEADS * B_DK
B_V = B_HEADS * B_DV
B_QKVO = 2 * B_Q + 2 * B_V
C_Q = C_HEADS * HEAD_DIM
C_KVW = C_KV * HEAD_DIM

TM = 256
MOE_GROUP_TILES = 2
VMEM_LIMIT = 56 * 1024 * 1024
NEG_INF = float("-inf")
LOG2E = 1.4426950408889634


def _row_block(m):
    return next(r * TM for r in (3, 2, 1) if m % (r * TM) == 0)


def _params(n_axes, vmem=VMEM_LIMIT):
    return pltpu.CompilerParams(dimension_semantics=("arbitrary",) * n_axes, vmem_limit_bytes=vmem)


def _rms(x, g):
    return x * lax.rsqrt(jnp.mean(x * x, axis=-1, keepdims=True) + EPS) * g


def _dot(a, b):
    return jnp.dot(a, b, preferred_element_type=F32)


def _dot_nt(a, b):
    return lax.dot_general(a, b, (((1,), (1,)), ((), ())), preferred_element_type=F32)


def _ada_kernel(s_ref, w_ref, b_ref, o_ref):
    s = s_ref[...]
    s = s * jax.nn.sigmoid(s)
    o_ref[...] = _dot(s.astype(BF16), w_ref[...].astype(BF16)) + b_ref[...]


def _ada_mods(c, c_ctx, ada_w, ada_b):
    depth, d, d6 = ada_w.shape
    tn = 1536 if d6 % 1536 == 0 else d6
    s = jnp.zeros((8, d), F32).at[0].set(c[0]).at[1].set(c_ctx)
    out = pl.pallas_call(
        _ada_kernel,
        grid=(depth, d6 // tn),
        in_specs=[pl.BlockSpec((8, d), lambda l, n: (0, 0)),
                  pl.BlockSpec((None, d, tn), lambda l, n: (l, 0, n)),
                  pl.BlockSpec((None, 1, tn), lambda l, n: (l, 0, n))],
        out_specs=pl.BlockSpec((None, 8, tn), lambda l, n: (l, 0, n)),
        out_shape=jax.ShapeDtypeStruct((depth, 8, d6), F32),
        compiler_params=_params(2),
        name="ada_mods",
    )(s, ada_w, ada_b.reshape(depth, 1, d6))
    return out[:, :2].reshape(depth, 2, 6, d)


def _mod_spec(d, layer, n_ctx_tiles):
    return pl.BlockSpec((None, None, 6, d), lambda i: (layer, jnp.where(i < n_ctx_tiles, 1, 0), 0, 0))


def _prenorm_kernel(x_ref, g_ref, mod_ref, h_ref):
    hn = _rms(x_ref[...], g_ref[0:1, :])
    h_ref[...] = (hn * (1.0 + mod_ref[1:2, :]) + mod_ref[0:1, :]).astype(h_ref.dtype)


def _prenorm(x, norm_g, mods, layer, n_ctx_tiles):
    m, d = x.shape
    return pl.pallas_call(
        _prenorm_kernel,
        grid=(m // TM,),
        in_specs=[pl.BlockSpec((TM, d), lambda i: (i, 0)),
                  pl.BlockSpec((None, 4, d), lambda i: (layer, 0, 0)),
                  _mod_spec(d, layer, n_ctx_tiles)],
        out_specs=pl.BlockSpec((TM, d), lambda i: (i, 0)),
        out_shape=jax.ShapeDtypeStruct((m, d), BF16),
        compiler_params=_params(1),
        name="prenorm",
    )(x, norm_g, mods)


def _post_pre_math(x, y, gpost_ref, gpre_ref, mpost_ref, mpre_ref, xo_ref, h_ref, sub):
    g_row = 1 + 2 * sub
    gate_row = 2 + 3 * sub
    xn = x + mpost_ref[gate_row:gate_row + 1, :] * _rms(y, gpost_ref[g_row:g_row + 1, :])
    xo_ref[...] = xn
    if h_ref is not None:
        nsub = 1 - sub
        hn = _rms(xn, gpre_ref[2 * nsub:2 * nsub + 1, :])
        h_ref[...] = (hn * (1.0 + mpre_ref[3 * nsub + 1:3 * nsub + 2, :])
                      + mpre_ref[3 * nsub:3 * nsub + 1, :]).astype(h_ref.dtype)


def _post_pre_kernel(x_ref, y_ref, gpost_ref, gpre_ref, mpost_ref, mpre_ref, xo_ref, *h_ref, sub):
    _post_pre_math(x_ref[...], y_ref[...], gpost_ref, gpre_ref, mpost_ref, mpre_ref, xo_ref,
                   h_ref[0] if h_ref else None, sub)


def _post_pre(x, y, norm_g, mods, layer, sub, n_ctx_tiles, h_dtype):
    m, d = x.shape
    pre_layer = layer if sub == 0 else min(layer + 1, norm_g.shape[0] - 1)
    row = pl.BlockSpec((TM, d), lambda i: (i, 0))
    out_shape = [jax.ShapeDtypeStruct((m, d), F32)]
    out_specs = [row]
    if h_dtype is not None:
        out_shape.append(jax.ShapeDtypeStruct((m, d), h_dtype))
        out_specs.append(row)
    res = pl.pallas_call(
        functools.partial(_post_pre_kernel, sub=sub),
        grid=(m // TM,),
        in_specs=[row, row,
                  pl.BlockSpec((None, 4, d), lambda i: (layer, 0, 0)),
                  pl.BlockSpec((None, 4, d), lambda i: (pre_layer, 0, 0)),
                  _mod_spec(d, layer, n_ctx_tiles),
                  _mod_spec(d, pre_layer, n_ctx_tiles)],
        out_specs=out_specs,
        out_shape=out_shape,
        compiler_params=_params(1),
        name="post_pre",
    )(x, y, norm_g, norm_g, mods, mods)
    return res if h_dtype is not None else (res[0], None)


def _rope(y, cos, sin):
    lane = lax.broadcasted_iota(I32, y.shape, 1)
    lower = (lane & (2 * ROPE_FREQS - 1)) < ROPE_FREQS
    partner = jnp.where(lower, pltpu.roll(y, HEAD_DIM - ROPE_FREQS, 1), pltpu.roll(y, ROPE_FREQS, 1))
    return y * cos + partner * sin


def _proj_kernel(x_ref, w_ref, cos_ref, sin_ref, g_ref, o_ref, wb_ref, *, segments, tn):
    n = pl.program_id(0)

    @pl.when(pl.program_id(1) == 0)
    def _():
        wb_ref[...] = w_ref[...].astype(BF16)

    y = _dot(x_ref[...], wb_ref[...])
    for (t0, t1, kind, grow, scale) in segments:
        @pl.when((n >= t0) & (n < t1))
        def _(kind=kind, grow=grow, scale=scale):
            if kind == "plain":
                o_ref[...] = y.astype(o_ref.dtype)
                return
            for hh in range(tn // HEAD_DIM):
                sl = slice(hh * HEAD_DIM, (hh + 1) * HEAD_DIM)
                yh = y[:, sl]
                if kind == "norm_rope":
                    yh = _rms(yh, g_ref[grow:grow + 1, :])
                yh = _rope(yh, cos_ref[...], sin_ref[...])
                if scale != 1.0:
                    yh = yh * scale
                o_ref[:, sl] = yh.astype(o_ref.dtype)


def _proj(h, w3, layer_idx, col0, ncols, tn, segments, out_dtype, cos_t, sin_t, gains):
    m, k = h.shape
    assert col0 % tn == 0 and ncols % tn == 0
    n0 = col0 // tn
    tm = _row_block(m)
    return pl.pallas_call(
        functools.partial(_proj_kernel, segments=tuple(segments), tn=tn),
        grid=(ncols // tn, m // tm),
        in_specs=[pl.BlockSpec((tm, k), lambda n, i: (i, 0)),
                  pl.BlockSpec((None, k, tn), lambda n, i: (layer_idx, 0, n + n0)),
                  pl.BlockSpec((tm, HEAD_DIM), lambda n, i: (i, 0)),
                  pl.BlockSpec((tm, HEAD_DIM), lambda n, i: (i, 0)),
                  pl.BlockSpec((2, HEAD_DIM), lambda n, i: (0, 0))],
        out_specs=pl.BlockSpec((tm, tn), lambda n, i: (i, n)),
        out_shape=jax.ShapeDtypeStruct((m, ncols), out_dtype),
        scratch_shapes=[pltpu.VMEM((k, tn), BF16)],
        compiler_params=_params(2),
        name="proj",
    )(h, w3, cos_t, sin_t, gains)


def _win_kernel(sink_ref, q_ref, kp_ref, kc_ref, kn_ref, kx_ref, vp_ref, vc_ref, vn_ref, vx_ref, o_ref, *, group):
    kvh = pl.program_id(0)
    i = pl.program_id(1)
    nb = pl.num_programs(1)
    qi = lax.broadcasted_iota(I32, (BLK, BLK), 0)
    kj = lax.broadcasted_iota(I32, (BLK, BLK), 1)
    mask_p = (kj - qi) >= jnp.where(i > 0, 0, BLK)
    mask_n = (qi - kj) >= jnp.where(i < nb - 1, 0, BLK)
    kp, kc, kn, kx = kp_ref[...], kc_ref[...], kn_ref[...], kx_ref[...]
    vp, vc, vn, vx = vp_ref[...], vc_ref[...], vn_ref[...], vx_ref[...]
    for g in range(group):
        sl = slice(g * HEAD_DIM, (g + 1) * HEAD_DIM)
        q = q_ref[:, sl]
        sp = jnp.where(mask_p, _dot_nt(q, kp), NEG_INF)
        sc = _dot_nt(q, kc)
        sn = jnp.where(mask_n, _dot_nt(q, kn), NEG_INF)
        sx = _dot_nt(q, kx)
        snk = sink_ref[kvh * group + g] * LOG2E
        mx = jnp.maximum(jnp.maximum(jnp.max(sp, -1, keepdims=True), jnp.max(sc, -1, keepdims=True)),
                         jnp.maximum(jnp.max(sn, -1, keepdims=True), jnp.max(sx, -1, keepdims=True)))
        mx = jnp.maximum(mx, snk)
        pp, pc, pn, px = jnp.exp2(sp - mx), jnp.exp2(sc - mx), jnp.exp2(sn - mx), jnp.exp2(sx - mx)
        den = (jnp.sum(pp, -1, keepdims=True) + jnp.sum(pc, -1, keepdims=True)
               + jnp.sum(pn, -1, keepdims=True) + jnp.sum(px, -1, keepdims=True) + jnp.exp2(snk - mx))
        o = (_dot(pp.astype(BF16), vp) + _dot(pc.astype(BF16), vc)
             + _dot(pn.astype(BF16), vn) + _dot(px.astype(BF16), vx))
        o_ref[:, sl] = (o / den).astype(o_ref.dtype)


def _window_attn(qkv, sink, n_ctx):
    m = qkv.shape[0]
    nb = (m - n_ctx) // BLK
    cb = n_ctx // BLK
    group = A_HEADS // A_KV
    kcol = A_Q // HEAD_DIM
    vcol = kcol + A_KV

    def kv_spec(col, shift):
        return pl.BlockSpec((BLK, HEAD_DIM),
                            lambda h, i, s: (cb + jnp.clip(i + shift, 0, nb - 1), col + h))

    def ctx_spec(col):
        return pl.BlockSpec((n_ctx, HEAD_DIM), lambda h, i, s: (0, col + h))

    grid_spec = pltpu.PrefetchScalarGridSpec(
        num_scalar_prefetch=1,
        grid=(A_KV, nb),
        in_specs=[pl.BlockSpec((BLK, group * HEAD_DIM), lambda h, i, s: (cb + i, h)),
                  kv_spec(kcol, -1), kv_spec(kcol, 0), kv_spec(kcol, 1), ctx_spec(kcol),
                  kv_spec(vcol, -1), kv_spec(vcol, 0), kv_spec(vcol, 1), ctx_spec(vcol)],
        out_specs=pl.BlockSpec((BLK, group * HEAD_DIM), lambda h, i, s: (i, h)),
    )
    return pl.pallas_call(
        functools.partial(_win_kernel, group=group),
        grid_spec=grid_spec,
        out_shape=jax.ShapeDtypeStruct((m - n_ctx, A_Q), BF16),
        compiler_params=_params(2),
        name="window_attn",
    )(sink, qkv, qkv, qkv, qkv, qkv, qkv, qkv, qkv, qkv)


def _flash_kernel(sink_ref, q_ref, k_ref, v_ref, o_ref, va_ref, m_ref, acc_ref, *, group, tq, tk, n_keys, has_sink):
    kvh = pl.program_id(0)
    rows = group * tq
    ncol = tk // HEAD_DIM

    @pl.when(pl.program_id(1) == 0)
    def _():
        va_ref[:, :HEAD_DIM] = v_ref[...]
        va_ref[:, HEAD_DIM:] = jnp.ones((n_keys, HEAD_DIM), BF16)

    q = jnp.concatenate([q_ref[:, g * HEAD_DIM:(g + 1) * HEAD_DIM] for g in range(group)], axis=0)
    m_ref[...] = jnp.full((rows, HEAD_DIM), NEG_INF, F32)
    acc_ref[...] = jnp.zeros((rows, 2 * HEAD_DIM), F32)

    def body(j, carry):
        off = pl.multiple_of(j * tk, tk)
        s = _dot_nt(q, k_ref[pl.ds(off, tk), :])
        cols = [s[:, c * HEAD_DIM:(c + 1) * HEAD_DIM] for c in range(ncol)]
        m_prev = m_ref[...]
        m_new = jnp.maximum(m_prev, jnp.max(functools.reduce(jnp.maximum, cols), -1, keepdims=True))
        alpha = jnp.exp2(m_prev - m_new)
        p = jnp.concatenate([jnp.exp2(c - m_new).astype(BF16) for c in cols], axis=1)
        acc_ref[...] = jnp.concatenate([alpha, alpha], axis=1) * acc_ref[...] + _dot(p, va_ref[pl.ds(off, tk), :])
        m_ref[...] = m_new
        return carry

    for j in range(n_keys // tk):
        body(j, 0)
    for g in range(group):
        rs = slice(g * tq, (g + 1) * tq)
        num = acc_ref[rs, :HEAD_DIM]
        den = acc_ref[rs, HEAD_DIM:]
        if has_sink:
            m_fin = m_ref[rs, :]
            snk = sink_ref[kvh * group + g] * LOG2E
            m_tot = jnp.maximum(m_fin, snk)
            scale = jnp.exp2(m_fin - m_tot)
            den = den * scale + jnp.exp2(snk - m_tot)
            num = num * scale
        o_ref[:, g * HEAD_DIM:(g + 1) * HEAD_DIM] = (num / den).astype(o_ref.dtype)


def _flash_attn(q_arr, q_col0, kv_arr, k_col0, v_col0, n_kv, group, sink, q_row0, n_q, n_keys, tq, tk):
    assert n_q % tq == 0 and q_row0 % tq == 0 and n_keys % tk == 0
    has_sink = sink is not None
    if sink is None:
        sink = jnp.zeros((n_kv * group,), F32)
    qb0 = q_row0 // tq
    qc0 = q_col0 // (group * HEAD_DIM)
    kc0 = k_col0 // HEAD_DIM
    vc0 = v_col0 // HEAD_DIM
    rows = group * tq
    grid_spec = pltpu.PrefetchScalarGridSpec(
        num_scalar_prefetch=1,
        grid=(n_kv, n_q // tq),
        in_specs=[pl.BlockSpec((tq, group * HEAD_DIM), lambda h, i, s: (qb0 + i, qc0 + h)),
                  pl.BlockSpec((n_keys, HEAD_DIM), lambda h, i, s: (0, kc0 + h)),
                  pl.BlockSpec((n_keys, HEAD_DIM), lambda h, i, s: (0, vc0 + h))],
        out_specs=pl.BlockSpec((tq, group * HEAD_DIM), lambda h, i, s: (i, h)),
        scratch_shapes=[pltpu.VMEM((n_keys, 2 * HEAD_DIM), BF16), pltpu.VMEM((rows, HEAD_DIM), F32),
                        pltpu.VMEM((rows, 2 * HEAD_DIM), F32)],
    )
    return pl.pallas_call(
        functools.partial(_flash_kernel, group=group, tq=tq, tk=tk, n_keys=n_keys, has_sink=has_sink),
        grid_spec=grid_spec,
        out_shape=jax.ShapeDtypeStruct((n_q, n_kv * group * HEAD_DIM), BF16),
        compiler_params=_params(2),
        name="flash_attn",
    )(sink, q_arr, kv_arr, kv_arr)


def _log_sigmoid(x):
    return jnp.minimum(x, 0.0) - jnp.log(1.0 + jnp.exp(-jnp.abs(x)))


def _dot_exact(a, b):
    return jnp.dot(a, b, preferred_element_type=F32, precision=lax.Precision.HIGHEST)


def _mlstm_head(q, k, v, i_col, i_row, b_col, b_row, b_tot, mask, c_ref, n_ref, m_ref):
    m_prev = m_ref[:, 0:1]
    c_prev = c_ref[...]
    n_prev = n_ref[...]
    qs = q * (B_DK ** -0.5)
    qb = qs.astype(BF16)
    dlog = jnp.where(mask, b_col - b_row + i_row, NEG_INF)
    inter = b_col + m_prev
    m_t = jnp.maximum(inter, jnp.max(dlog, -1, keepdims=True))
    dw = jnp.exp(dlog - m_t)
    iw = jnp.exp(inter - m_t)
    sc = _dot_nt(qb, k.astype(BF16)) * dw
    num = _dot(sc.astype(BF16), v.astype(BF16)) + iw * _dot(qb, c_prev.astype(BF16))
    den = jnp.sum(sc, -1, keepdims=True) + iw * jnp.sum(qs * n_prev, -1, keepdims=True)
    h = num / jnp.maximum(jnp.abs(den), jnp.exp(-m_t))
    glog_col = b_tot - b_col + i_col
    glog_row = b_tot - b_row + i_row
    m_new = jnp.maximum(b_tot + m_prev, jnp.max(glog_row, -1, keepdims=True))
    decay = jnp.exp(b_tot + m_prev - m_new)
    wk = jnp.exp(glog_col - m_new) * k
    c_ref[...] = decay * c_prev + lax.dot_general(wk.astype(BF16), v.astype(BF16), (((0,), (0,)), ((), ())),
                                                  preferred_element_type=F32)
    n_ref[...] = decay * n_prev + jnp.sum(wk, 0, keepdims=True)
    m_ref[...] = jnp.broadcast_to(m_new, m_ref.shape)
    return h


def _mlstm_kernel(brow_ref, bcol_ref,
                  qf_ref, kf_ref, vf_ref, gf_ref, gtf_ref,
                  qb_ref, kb_ref, vb_ref, gb_ref, gtb_ref,
                  of_ref, ob_ref, c_ref, n_ref, m_ref):
    L = MLSTM_CHUNK

    @pl.when(pl.program_id(0) == 0)
    def _():
        c_ref[...] = jnp.zeros(c_ref.shape, F32)
        n_ref[...] = jnp.zeros(n_ref.shape, F32)
        m_ref[...] = jnp.zeros(m_ref.shape, F32)

    r = lax.broadcasted_iota(I32, (L, L), 0)
    cidx = lax.broadcasted_iota(I32, (L, L), 1)
    lane = lax.broadcasted_iota(I32, (L, HEAD_DIM), 1)
    sub = lax.broadcasted_iota(I32, (2 * 2 * B_HEADS, L), 0)
    for d, (q_ref, k_ref, v_ref, g_ref, gt_ref, o_ref) in enumerate(
            ((qf_ref, kf_ref, vf_ref, gf_ref, gtf_ref, of_ref), (qb_ref, kb_ref, vb_ref, gb_ref, gtb_ref, ob_ref))):
        mask = (cidx <= r) if d == 0 else (cidx >= r)
        gc = g_ref[...] + brow_ref[...]
        gc = jnp.where(lane < 2 * B_HEADS, gc, _log_sigmoid(gc))
        gr = gt_ref[...] + bcol_ref[...]
        gr = jnp.where(sub < 2 * B_HEADS, gr, _log_sigmoid(gr))
        mask_t = (r <= cidx) if d == 0 else (r >= cidx)
        bc_all = _dot_exact(mask.astype(F32), gc)
        br_all = _dot_exact(gr, mask_t.astype(F32))
        edge = L - 1 if d == 0 else 0
        for hd in range(B_HEADS):
            ci = d * B_HEADS + hd
            cf = 2 * B_HEADS + ci
            b_col = bc_all[:, cf:cf + 1]
            h = _mlstm_head(q_ref[:, hd * B_DK:(hd + 1) * B_DK], k_ref[:, hd * B_DK:(hd + 1) * B_DK],
                            v_ref[:, hd * B_DV:(hd + 1) * B_DV],
                            gc[:, ci:ci + 1], gr[ci:ci + 1, :], b_col, br_all[cf:cf + 1, :],
                            b_col[edge:edge + 1, :], mask, c_ref.at[ci], n_ref.at[ci], m_ref.at[ci])
            o_ref[:, hd * B_DV:(hd + 1) * B_DV] = h


def _mlstm(qkvo, gates, gates_t, bias_row, bias_col, n_ctx):
    m = qkvo.shape[0]
    L = MLSTM_CHUNK
    nc = m // L
    ncc = n_ctx // L
    kq = B_Q // B_Q
    kv = (2 * B_Q) // B_V

    def fw(s):
        return s

    def bw(s):
        return jnp.where(s < ncc, ncc - 1 - s, nc - 1 + ncc - s)

    def specs(order):
        return [pl.BlockSpec((L, B_Q), lambda s: (order(s), 0)),
                pl.BlockSpec((L, B_Q), lambda s: (order(s), kq)),
                pl.BlockSpec((L, B_V), lambda s: (order(s), kv)),
                pl.BlockSpec((L, HEAD_DIM), lambda s: (order(s), 0)),
                pl.BlockSpec((4 * B_HEADS, L), lambda s: (0, order(s)))]

    nst = 2 * B_HEADS
    return pl.pallas_call(
        _mlstm_kernel,
        grid=(nc,),
        in_specs=[pl.BlockSpec((1, HEAD_DIM), lambda s: (0, 0)),
                  pl.BlockSpec((4 * B_HEADS, 1), lambda s: (0, 0))] + specs(fw) + specs(bw),
        out_specs=[pl.BlockSpec((L, B_V), lambda s: (fw(s), 0)),
                   pl.BlockSpec((L, B_V), lambda s: (bw(s), 0))],
        out_shape=[jax.ShapeDtypeStruct((m, B_V), F32), jax.ShapeDtypeStruct((m, B_V), F32)],
        scratch_shapes=[pltpu.VMEM((nst, B_DK, B_DV), F32), pltpu.VMEM((nst, 1, B_DK), F32),
                        pltpu.VMEM((nst, 1, HEAD_DIM), F32)],
        compiler_params=_params(1),
        name="mlstm",
    )(bias_row, bias_col, qkvo, qkvo, qkvo, gates, gates_t, qkvo, qkvo, qkvo, gates, gates_t)


def _bout_kernel(hf_ref, hb_ref, o_ref, g_ref, y_ref):
    for hd in range(B_HEADS):
        sl = slice(hd * B_DV, (hd + 1) * B_DV)
        hn = _rms(hf_ref[:, sl] + hb_ref[:, sl], g_ref[:, sl])
        y_ref[:, sl] = (jax.nn.sigmoid(o_ref[:, sl]) * hn).astype(y_ref.dtype)


def _b_out(hf, hb, qkvo, out_g):
    m = hf.shape[0]
    ocol = (2 * B_Q + B_V) // B_V
    row = pl.BlockSpec((TM, B_V), lambda i: (i, 0))
    return pl.pallas_call(
        _bout_kernel,
        grid=(m // TM,),
        in_specs=[row, row, pl.BlockSpec((TM, B_V), lambda i: (i, ocol)),
                  pl.BlockSpec((1, B_V), lambda i: (0, 0))],
        out_specs=row,
        out_shape=jax.ShapeDtypeStruct((m, B_V), BF16),
        compiler_params=_params(1),
        name="mlstm_out",
    )(hf, hb, qkvo, out_g)


def _for_row_tiles(gn, n_sub, o_ref, compute):
    def body(r, c):
        off = pl.multiple_of(r * TM, TM)
        o_ref[pl.ds(off, TM), :] = compute(off).astype(o_ref.dtype)
        return c

    lax.fori_loop(0, gn, body, 0)

    def zero(r, c):
        off = pl.multiple_of(r * TM, TM)
        o_ref[pl.ds(off, TM), :] = jnp.zeros((TM, o_ref.shape[1]), o_ref.dtype)
        return c

    lax.fori_loop(gn, n_sub, zero, 0)


def _glu_kernel(ge_ref, gf_ref, gn_ref, x_ref, wg_ref, wu_ref, o_ref, wgb_ref, wub_ref, *, n_sub):
    g = pl.program_id(1)

    @pl.when(gf_ref[g] == 1)
    def _():
        wgb_ref[...] = wg_ref[...].astype(BF16)
        wub_ref[...] = wu_ref[...].astype(BF16)

    def compute(off):
        x = x_ref[pl.ds(off, TM), :]
        a = _dot(x, wgb_ref[...])
        b = _dot(x, wub_ref[...])
        return a * jax.nn.sigmoid(a) * b

    _for_row_tiles(gn_ref[g], n_sub, o_ref, compute)


def _glu(x, wg, wu, e0, grp_e, grp_first, grp_n, n_sub, tn):
    p, k = x.shape
    f = wg.shape[-1]
    rg = n_sub * TM
    assert f % tn == 0 and p % rg == 0

    def wspec():
        return pl.BlockSpec((None, k, tn), lambda n, g, ge, gf, gn: (e0 + ge[g], 0, n))

    grid_spec = pltpu.PrefetchScalarGridSpec(
        num_scalar_prefetch=3,
        grid=(f // tn, p // rg),
        in_specs=[pl.BlockSpec((rg, k), lambda n, g, ge, gf, gn: (g, 0)), wspec(), wspec()],
        out_specs=pl.BlockSpec((rg, tn), lambda n, g, ge, gf, gn: (g, n)),
        scratch_shapes=[pltpu.VMEM((k, tn), BF16), pltpu.VMEM((k, tn), BF16)],
    )
    return pl.pallas_call(
        functools.partial(_glu_kernel, n_sub=n_sub),
        grid_spec=grid_spec,
        out_shape=jax.ShapeDtypeStruct((p, f), BF16),
        compiler_params=_params(2),
        name="glu",
    )(grp_e, grp_first, grp_n, x, wg, wu)


def _down_kernel(ge_ref, gf_ref, gn_ref, a_ref, wd_ref, *rest, weighted, n_sub):
    if weighted:
        rw_ref, o_ref, wdb_ref = rest
    else:
        o_ref, wdb_ref = rest
    g = pl.program_id(1)

    @pl.when(gf_ref[g] == 1)
    def _():
        wdb_ref[...] = wd_ref[...].astype(BF16)

    def compute(off):
        y = _dot(a_ref[pl.ds(off, TM), :], wdb_ref[...])
        if weighted:
            y = y * rw_ref[pl.ds(off, TM), :]
        return y

    _for_row_tiles(gn_ref[g], n_sub, o_ref, compute)


def _down(a, wd, e0, grp_e, grp_first, grp_n, n_sub, row_w, tn):
    p, f = a.shape
    d = wd.shape[-1]
    rg = n_sub * TM
    assert d % tn == 0 and p % rg == 0
    weighted = row_w is not None
    in_specs = [pl.BlockSpec((rg, f), lambda n, g, ge, gf, gn: (g, 0)),
                pl.BlockSpec((None, f, tn), lambda n, g, ge, gf, gn: (e0 + ge[g], 0, n))]
    args = [a, wd]
    if weighted:
        in_specs.append(pl.BlockSpec((rg, 1), lambda n, g, ge, gf, gn: (g, 0)))
        args.append(row_w)
    grid_spec = pltpu.PrefetchScalarGridSpec(
        num_scalar_prefetch=3,
        grid=(d // tn, p // rg),
        in_specs=in_specs,
        out_specs=pl.BlockSpec((rg, tn), lambda n, g, ge, gf, gn: (g, n)),
        scratch_shapes=[pltpu.VMEM((f, tn), BF16)],
    )
    return pl.pallas_call(
        functools.partial(_down_kernel, weighted=weighted, n_sub=n_sub),
        grid_spec=grid_spec,
        out_shape=jax.ShapeDtypeStruct((p, d), F32),
        compiler_params=_params(2),
        name="down",
    )(grp_e, grp_first, grp_n, *args)


def _dense_groups(m):
    n_sub = _row_block(m) // TM
    n_groups = m // (n_sub * TM)
    first = jnp.zeros((n_groups,), I32).at[0].set(1)
    return (jnp.zeros((n_groups,), I32), first, jnp.full((n_groups,), n_sub, I32)), n_sub


def _router_kernel(h_ref, r_ref, idx_ref, w_ref):
    logits = _dot(h_ref[...].astype(BF16), r_ref[...].astype(BF16))
    lane = lax.broadcasted_iota(I32, logits.shape, 1).astype(F32)
    big = float(HEAD_DIM)
    lg = jnp.where(lane < N_EXPERTS, logits, NEG_INF)
    m1 = jnp.max(lg, -1, keepdims=True)
    i1 = jnp.min(jnp.where(lg == m1, lane, big), -1, keepdims=True)
    lg2 = jnp.where(lane == i1, NEG_INF, lg)
    m2 = jnp.max(lg2, -1, keepdims=True)
    i2 = jnp.min(jnp.where(lg2 == m2, lane, big), -1, keepdims=True)
    e2 = jnp.exp(m2 - m1)
    den = 1.0 + e2
    idx_ref[...] = jnp.where(lane == 0.0, i1, jnp.where(lane == 1.0, i2, 0.0)).astype(I32)
    w_ref[...] = jnp.where(lane == 0.0, 1.0 / den, jnp.where(lane == 1.0, e2 / den, 0.0))


def _router(h, router_p):
    m, d = h.shape
    row = pl.BlockSpec((TM, HEAD_DIM), lambda i: (i, 0))
    return pl.pallas_call(
        _router_kernel,
        grid=(m // TM,),
        in_specs=[pl.BlockSpec((TM, d), lambda i: (i, 0)), pl.BlockSpec((d, HEAD_DIM), lambda i: (0, 0))],
        out_specs=[row, row],
        out_shape=[jax.ShapeDtypeStruct((m, HEAD_DIM), I32), jax.ShapeDtypeStruct((m, HEAD_DIM), F32)],
        compiler_params=_params(1),
        name="router",
    )(h, router_p)


def _moe_plan(eidx, ew, n_groups, n_sub):
    m = eidx.shape[0]
    rg = n_sub * TM
    e_flat = eidx.reshape(-1)
    experts = jnp.arange(N_EXPERTS, dtype=I32)
    onehot = (e_flat[:, None] == experts[None, :]).astype(I32)
    csum = jnp.cumsum(onehot, axis=0)
    rank = jnp.sum(csum * onehot, axis=1) - 1
    counts = csum[-1]
    groups_e = (counts + rg - 1) // rg
    grp_end = jnp.cumsum(groups_e)
    grp_start = grp_end - groups_e
    pos = jnp.sum(onehot * grp_start[None, :], axis=1) * rg + rank
    token = jnp.arange(2 * m, dtype=I32) // 2
    src = jnp.zeros((n_groups * rg,), I32).at[pos].set(token)
    row_w = jnp.zeros((n_groups * rg,), F32).at[pos].set(ew.reshape(-1))
    gids = jnp.arange(n_groups, dtype=I32)
    valid = gids < grp_end[-1]
    grp_e = jnp.sum((gids[:, None] >= grp_end[None, :]).astype(I32), axis=1)
    grp_e = jnp.where(valid, grp_e, jnp.max(jnp.where(counts > 0, experts, 0)))
    first = jnp.concatenate([jnp.ones((1,), I32), (grp_e[1:] != grp_e[:-1]).astype(I32)])
    mine = (grp_e[:, None] == experts[None, :]).astype(I32)
    rows_left = jnp.sum(mine * counts[None, :], axis=1) - (gids - jnp.sum(mine * grp_start[None, :], axis=1)) * rg
    grp_n = jnp.where(valid, jnp.clip((rows_left + TM - 1) // TM, 0, n_sub), 0).astype(I32)
    tile_valid = (jnp.arange(n_groups * n_sub, dtype=I32) % n_sub < jnp.repeat(grp_n, n_sub)).astype(I32)
    pos2 = pos.reshape(m, 2).astype(I32)
    return src, row_w.reshape(-1, 1), (grp_e, first, grp_n), tile_valid, pos2[:, 0], pos2[:, 1]


def _gather_kernel(src_ref, tv_ref, h_hbm, o_ref, buf_ref, sem):
    t = pl.program_id(0)

    def row_copy(row, r):
        return pltpu.make_async_copy(h_hbm.at[pl.ds(row, 1), :], buf_ref.at[pl.ds(r, 1), :], sem)

    @pl.when(tv_ref[t] == 1)
    def _():
        def issue(r, c):
            row_copy(src_ref[t * TM + r], r).start()
            return c

        lax.fori_loop(0, TM, issue, 0)

        def wait(r, c):
            row_copy(0, r).wait()
            return c

        lax.fori_loop(0, TM, wait, 0)
        o_ref[...] = buf_ref[...].astype(o_ref.dtype)

    @pl.when(tv_ref[t] == 0)
    def _():
        o_ref[...] = jnp.zeros(o_ref.shape, o_ref.dtype)


def _gather_rows(h, src, tile_valid):
    d = h.shape[1]
    p = src.shape[0]
    grid_spec = pltpu.PrefetchScalarGridSpec(
        num_scalar_prefetch=2,
        grid=(p // TM,),
        in_specs=[pl.BlockSpec(memory_space=pl.ANY)],
        out_specs=pl.BlockSpec((TM, d), lambda t, s, tv: (t, 0)),
        scratch_shapes=[pltpu.VMEM((TM, d), F32), pltpu.SemaphoreType.DMA(())],
    )
    return pl.pallas_call(
        _gather_kernel,
        grid_spec=grid_spec,
        out_shape=jax.ShapeDtypeStruct((p, d), BF16),
        compiler_params=_params(1),
        name="moe_gather",
    )(src, tile_valid, h)


def _combine_kernel(p0_ref, p1_ref, ys_hbm, x_ref, gpost_ref, gpre_ref, mpost_ref, mpre_ref, xo_ref, *rest,
                    with_pre):
    if with_pre:
        h_ref, buf_ref, sem = rest
    else:
        h_ref = None
        buf_ref, sem = rest
    t = pl.program_id(0)

    def row_copy(row, k, r):
        return pltpu.make_async_copy(ys_hbm.at[pl.ds(row, 1), :], buf_ref.at[k, pl.ds(r, 1), :], sem)

    def issue(r, c):
        row_copy(p0_ref[t * TM + r], 0, r).start()
        row_copy(p1_ref[t * TM + r], 1, r).start()
        return c

    lax.fori_loop(0, TM, issue, 0)

    def wait(r, c):
        row_copy(0, 0, r).wait()
        row_copy(0, 1, r).wait()
        return c

    lax.fori_loop(0, TM, wait, 0)
    _post_pre_math(x_ref[...], buf_ref[0] + buf_ref[1], gpost_ref, gpre_ref, mpost_ref, mpre_ref, xo_ref, h_ref, 1)


def _combine_post_pre(x, ys, pos0, pos1, norm_g, mods, layer, n_ctx_tiles, h_dtype):
    m, d = x.shape
    pre_layer = min(layer + 1, norm_g.shape[0] - 1)
    with_pre = h_dtype is not None
    row = pl.BlockSpec((TM, d), lambda i, a, b: (i, 0))

    def mod_spec(l):
        return pl.BlockSpec((None, None, 6, d), lambda i, a, b: (l, jnp.where(i < n_ctx_tiles, 1, 0), 0, 0))

    out_shape = [jax.ShapeDtypeStruct((m, d), F32)]
    out_specs = [row]
    if with_pre:
        out_shape.append(jax.ShapeDtypeStruct((m, d), h_dtype))
        out_specs.append(row)
    grid_spec = pltpu.PrefetchScalarGridSpec(
        num_scalar_prefetch=2,
        grid=(m // TM,),
        in_specs=[pl.BlockSpec(memory_space=pl.ANY), row,
                  pl.BlockSpec((None, 4, d), lambda i, a, b: (layer, 0, 0)),
                  pl.BlockSpec((None, 4, d), lambda i, a, b: (pre_layer, 0, 0)),
                  mod_spec(layer), mod_spec(pre_layer)],
        out_specs=out_specs,
        scratch_shapes=[pltpu.VMEM((2, TM, d), F32), pltpu.SemaphoreType.DMA(())],
    )
    res = pl.pallas_call(
        functools.partial(_combine_kernel, with_pre=with_pre),
        grid_spec=grid_spec,
        out_shape=out_shape,
        compiler_params=_params(1),
        name="moe_combine",
    )(pos0, pos1, ys, x, norm_g, norm_g, mods, mods)
    return res if with_pre else (res[0], None)


def _rope_tables(n_ctx, s):
    pos = jnp.arange(s)
    row = (pos // GRID_W).astype(F32)
    col = (pos % GRID_W).astype(F32)
    inv = ROPE_THETA ** (-jnp.arange(ROPE_FREQS, dtype=F32) / ROPE_FREQS)
    ar, ac = row[:, None] * inv, col[:, None] * inv
    cos = jnp.concatenate([jnp.cos(ar), jnp.cos(ar), jnp.cos(ac), jnp.cos(ac)], axis=-1)
    sin = jnp.concatenate([-jnp.sin(ar), jnp.sin(ar), -jnp.sin(ac), jnp.sin(ac)], axis=-1)
    cos = jnp.concatenate([jnp.ones((n_ctx, HEAD_DIM), F32), cos], axis=0)
    sin = jnp.concatenate([jnp.zeros((n_ctx, HEAD_DIM), F32), sin], axis=0)
    return cos, sin


def _layer_ab(h, j, ab_w_in, a_sink, b_ig_bias, b_fg_bias, b_norm_g, cos_t, sin_t, n_ctx):
    scale = HEAD_DIM ** -0.5 * LOG2E
    no_g = jnp.ones((2, HEAD_DIM), F32)
    tn = 256
    qt, kt = A_Q // tn, (A_Q + A_KV * HEAD_DIM) // tn
    a_qkv = _proj(h, ab_w_in, j, 0, A_QKV, tn,
                  [(0, qt, "rope", 0, scale), (qt, kt, "rope", 0, 1.0), (kt, A_QKV // tn, "plain", 0, 1.0)],
                  BF16, cos_t, sin_t, no_g)
    b_qkvo = _proj(h, ab_w_in, j, A_QKV, B_QKVO, 512, [(0, B_QKVO // 512, "plain", 0, 1.0)],
                   F32, cos_t, sin_t, no_g)
    ngate = 4 * B_HEADS
    w_gate = jnp.pad(ab_w_in[j, :, A_QKV + B_QKVO:], ((0, 0), (0, HEAD_DIM - ngate)))[None]
    gates = _proj(h, w_gate, 0, 0, HEAD_DIM, HEAD_DIM, [(0, 1, "plain", 0, 1.0)], F32, cos_t, sin_t, no_g)

    ya_l = _window_attn(a_qkv, a_sink[j], n_ctx)
    ya_c = _flash_attn(a_qkv, 0, a_qkv, A_Q, A_Q + A_KV * HEAD_DIM, A_KV, A_HEADS // A_KV, a_sink[j],
                       0, n_ctx, n_ctx, n_ctx, n_ctx)
    ya = jnp.concatenate([ya_c, ya_l], axis=0)

    bias = jnp.concatenate([b_ig_bias[j].reshape(-1), b_fg_bias[j].reshape(-1)]).astype(F32)
    bias_row = jnp.pad(bias, (0, HEAD_DIM - ngate))[None, :]
    hf, hb = _mlstm(b_qkvo, gates, jnp.transpose(gates[:, :ngate]), bias_row, bias[:, None], n_ctx)
    yb = _b_out(hf, hb, b_qkvo, b_norm_g[j][None, :])
    return jnp.concatenate([ya, yb], axis=1)


def _layer_c(h, j, c_w_qkv, c_qk_g, cos_t, sin_t, n_ctx, need_ctx):
    scale = HEAD_DIM ** -0.5 * LOG2E
    tn = 512
    qt, kt = C_Q // tn, (C_Q + C_KVW) // tn
    qkv = _proj(h, c_w_qkv, j, 0, C_Q + 2 * C_KVW, tn,
                [(0, qt, "norm_rope", 0, scale), (qt, kt, "norm_rope", 1, 1.0), (kt, kt + C_KVW // tn, "plain", 0, 1.0)],
                BF16, cos_t, sin_t, c_qk_g[j])
    m = h.shape[0]
    group = C_HEADS // C_KV
    tk = 768 if m % 768 == 0 else TM
    y_l = _flash_attn(qkv, 0, qkv, C_Q, C_Q + C_KVW, C_KV, group, None, n_ctx, m - n_ctx, m, TM, tk)
    if need_ctx:
        y_c = _flash_attn(qkv, 0, qkv, C_Q, C_Q + C_KVW, C_KV, group, None, 0, n_ctx, n_ctx, n_ctx, n_ctx)
    else:
        y_c = jnp.zeros((n_ctx, C_Q), BF16)
    return jnp.concatenate([y_c, y_l], axis=0)


def kernel(x, c, ctx, c_ctx, ada_w, ada_b, norm_g, ab_w_in, ab_w_out, a_sink, b_ig_bias, b_fg_bias, b_norm_g,
           c_w_qkv, c_w_out, c_qk_g, ffn_w_gate, ffn_w_up, ffn_w_down, moe_router, moe_w_gate, moe_w_up,
           moe_w_down):
    depth = ada_w.shape[0]
    s, d = x.shape[1], x.shape[2]
    n_ctx = ctx.shape[1]
    m = n_ctx + s
    assert x.shape[0] == 1 and n_ctx % TM == 0 and s % TM == 0
    n_ctx_tiles = n_ctx // TM
    n_tiles = m // TM
    no_g = jnp.ones((2, HEAD_DIM), F32)

    mods = _ada_mods(c, c_ctx, ada_w, ada_b)
    cos_t, sin_t = _rope_tables(n_ctx, s)
    xs = jnp.concatenate([ctx[0], x[0]], axis=0)
    h = _prenorm(xs, norm_g, mods, 0, n_ctx_tiles)
    dense_groups, dense_sub = _dense_groups(m)
    n_exp = moe_w_gate.shape[1]
    moe_g = moe_w_gate.reshape((-1,) + moe_w_gate.shape[2:])
    moe_u = moe_w_up.reshape((-1,) + moe_w_up.shape[2:])
    moe_d = moe_w_down.reshape((-1,) + moe_w_down.shape[2:])

    for layer in range(depth):
        last = layer == depth - 1
        j = layer // 2
        even = layer % 2 == 0
        if even:
            mix = _layer_ab(h, j, ab_w_in, a_sink, b_ig_bias, b_fg_bias, b_norm_g, cos_t, sin_t, n_ctx)
            w_out = ab_w_out
        else:
            mix = _layer_c(h, j, c_w_qkv, c_qk_g, cos_t, sin_t, n_ctx, not last)
            w_out = c_w_out
        y = _proj(mix, w_out, j, 0, d, 512 if d % 512 == 0 else d, [(0, max(d // 512, 1), "plain", 0, 1.0)],
                  F32, cos_t, sin_t, no_g)
        xs, h = _post_pre(xs, y, norm_g, mods, layer, 0, n_ctx_tiles, BF16 if even else F32)
        if even:
            f_dim = ffn_w_gate.shape[-1]
            act = _glu(h, ffn_w_gate, ffn_w_up, j, *dense_groups, dense_sub, 512 if f_dim % 512 == 0 else f_dim)
            f = _down(act, ffn_w_down, j, *dense_groups, dense_sub, None, 512 if d % 512 == 0 else d)
            xs, h = _post_pre(xs, f, norm_g, mods, layer, 1, n_ctx_tiles, None if last else BF16)
        else:
            router_p = jnp.pad(moe_router[j], ((0, 0), (0, HEAD_DIM - n_exp)))
            eidx, ew = _router(h, router_p)
            n_sub = MOE_GROUP_TILES
            n_groups = -(-2 * m // (n_sub * TM)) + n_exp
            src, row_w, groups, tile_valid, pos0, pos1 = _moe_plan(eidx[:, :2], ew[:, :2], n_groups, n_sub)
            x_s = _gather_rows(h, src, tile_valid)
            f_dim = moe_g.shape[-1]
            act = _glu(x_s, moe_g, moe_u, j * n_exp, *groups, n_sub, 512 if f_dim % 512 == 0 else f_dim)
            y_s = _down(act, moe_d, j * n_exp, *groups, n_sub, row_w, 512 if d % 512 == 0 else d)
            xs, h = _combine_post_pre(xs, y_s, pos0, pos1, norm_g, mods, layer, n_ctx_tiles, None if last else BF16)
    return xs[n_ctx:][None]
```

```python
import functools

import jax
import jax.numpy as jnp
import numpy as np
from jax import lax
from jax.experimental import pallas as pl
from jax.experimental.pallas import tpu as pltpu

F32 = jnp.float32
BF16 = jnp.bfloat16
I32 = jnp.int32

EPS = 1e-6
GRID_W = 64
HEAD_DIM = 128
ROPE_THETA = 10000.0
ROPE_FREQS = HEAD_DIM // 4
BLK = 128
A_HEADS = 8
A_KV = 2
B_HEADS = 4
B_DK = 128
B_DV = 256
MLSTM_CHUNK = 128
C_HEADS = 16
C_KV = 4
N_EXPERTS = 8

A_Q = A_HEADS * HEAD_DIM
A_QKV = A_Q + 2 * A_KV * HEAD_DIM
B_Q = B_HEADS * B_DK
B_V = B_HEADS * B_DV
B_QKVO = 2 * B_Q + 2 * B_V
C_Q = C_HEADS * HEAD_DIM
C_KVW = C_KV * HEAD_DIM

TM = 256
MOE_GROUP_TILES = 9
FFN_SLICE = 256
VMEM_LIMIT = 56 * 1024 * 1024
NEG_INF = float("-inf")
LOG2E = 1.4426950408889634


def _row_block(m):
    return next(r * TM for r in (3, 2, 1) if m % (r * TM) == 0)


def _params(n_axes, vmem=VMEM_LIMIT):
    return pltpu.CompilerParams(dimension_semantics=("arbitrary",) * n_axes, vmem_limit_bytes=vmem)


def _rms(x, g):
    return x * lax.rsqrt(jnp.mean(x * x, axis=-1, keepdims=True) + EPS) * g


def _dot(a, b):
    return jnp.dot(a, b, preferred_element_type=F32)


def _dot_nt(a, b):
    return lax.dot_general(a, b, (((1,), (1,)), ((), ())), preferred_element_type=F32)


def _ada_kernel(s_ref, w_ref, b_ref, o_ref):
    s = s_ref[...]
    s = s * jax.nn.sigmoid(s)
    o_ref[...] = _dot(s.astype(BF16), w_ref[...].astype(BF16)) + b_ref[...]


def _ada_mods(c, c_ctx, ada_w, ada_b):
    depth, d, d6 = ada_w.shape
    tn = 1536 if d6 % 1536 == 0 else d6
    s = jnp.zeros((8, d), F32).at[0].set(c[0]).at[1].set(c_ctx)
    out = pl.pallas_call(
        _ada_kernel,
        grid=(depth, d6 // tn),
        in_specs=[pl.BlockSpec((8, d), lambda l, n: (0, 0)),
                  pl.BlockSpec((None, d, tn), lambda l, n: (l, 0, n)),
                  pl.BlockSpec((None, 1, tn), lambda l, n: (l, 0, n))],
        out_specs=pl.BlockSpec((None, 8, tn), lambda l, n: (l, 0, n)),
        out_shape=jax.ShapeDtypeStruct((depth, 8, d6), F32),
        compiler_params=_params(2),
        name="ada_mods",
    )(s, ada_w, ada_b.reshape(depth, 1, d6))
    return out[:, :2].reshape(depth, 2, 6, d)


def _mod_spec(d, layer, n_ctx_tiles):
    return pl.BlockSpec((None, None, 6, d), lambda i: (layer, jnp.where(i < n_ctx_tiles, 1, 0), 0, 0))


def _prenorm_kernel(x_ref, g_ref, mod_ref, h_ref):
    hn = _rms(x_ref[...], g_ref[0:1, :])
    h_ref[...] = (hn * (1.0 + mod_ref[1:2, :]) + mod_ref[0:1, :]).astype(h_ref.dtype)


def _prenorm(x, norm_g, mods, layer, n_ctx_tiles):
    m, d = x.shape
    return pl.pallas_call(
        _prenorm_kernel,
        grid=(m // TM,),
        in_specs=[pl.BlockSpec((TM, d), lambda i: (i, 0)),
                  pl.BlockSpec((None, 4, d), lambda i: (layer, 0, 0)),
                  _mod_spec(d, layer, n_ctx_tiles)],
        out_specs=pl.BlockSpec((TM, d), lambda i: (i, 0)),
        out_shape=jax.ShapeDtypeStruct((m, d), BF16),
        compiler_params=_params(1),
        name="prenorm",
    )(x, norm_g, mods)


def _post_pre_math(x, y, gpost_ref, gpre_ref, mpost_ref, mpre_ref, xo_ref, h_ref, sub):
    g_row = 1 + 2 * sub
    gate_row = 2 + 3 * sub
    xn = x + mpost_ref[gate_row:gate_row + 1, :] * _rms(y, gpost_ref[g_row:g_row + 1, :])
    xo_ref[...] = xn
    if h_ref is not None:
        nsub = 1 - sub
        hn = _rms(xn, gpre_ref[2 * nsub:2 * nsub + 1, :])
        h_ref[...] = (hn * (1.0 + mpre_ref[3 * nsub + 1:3 * nsub + 2, :])
                      + mpre_ref[3 * nsub:3 * nsub + 1, :]).astype(h_ref.dtype)


def _post_pre_kernel(x_ref, y_ref, gpost_ref, gpre_ref, mpost_ref, mpre_ref, xo_ref, *h_ref, sub):
    _post_pre_math(x_ref[...], y_ref[...], gpost_ref, gpre_ref, mpost_ref, mpre_ref, xo_ref,
                   h_ref[0] if h_ref else None, sub)


def _post_pre(x, y, norm_g, mods, layer, sub, n_ctx_tiles, h_dtype):
    m, d = x.shape
    pre_layer = layer if sub == 0 else min(layer + 1, norm_g.shape[0] - 1)
    row = pl.BlockSpec((TM, d), lambda i: (i, 0))
    out_shape = [jax.ShapeDtypeStruct((m, d), F32)]
    out_specs = [row]
    if h_dtype is not None:
        out_shape.append(jax.ShapeDtypeStruct((m, d), h_dtype))
        out_specs.append(row)
    res = pl.pallas_call(
        functools.partial(_post_pre_kernel, sub=sub),
        grid=(m // TM,),
        in_specs=[row, row,
                  pl.BlockSpec((None, 4, d), lambda i: (layer, 0, 0)),
                  pl.BlockSpec((None, 4, d), lambda i: (pre_layer, 0, 0)),
                  _mod_spec(d, layer, n_ctx_tiles),
                  _mod_spec(d, pre_layer, n_ctx_tiles)],
        out_specs=out_specs,
        out_shape=out_shape,
        compiler_params=_params(1),
        name="post_pre",
    )(x, y, norm_g, norm_g, mods, mods)
    return res if h_dtype is not None else (res[0], None)


def _rope(y, cos, sin):
    lane = lax.broadcasted_iota(I32, y.shape, 1)
    lower = (lane & (2 * ROPE_FREQS - 1)) < ROPE_FREQS
    partner = jnp.where(lower, pltpu.roll(y, HEAD_DIM - ROPE_FREQS, 1), pltpu.roll(y, ROPE_FREQS, 1))
    return y * cos + partner * sin


def _proj_kernel(x_ref, w_ref, cos_ref, sin_ref, g_ref, o_ref, wb_ref, *, segments, tn):
    n = pl.program_id(0)

    @pl.when(pl.program_id(1) == 0)
    def _():
        wb_ref[...] = w_ref[...].astype(BF16)

    y = _dot(x_ref[...], wb_ref[...])
    for (t0, t1, kind, grow, scale) in segments:
        @pl.when((n >= t0) & (n < t1))
        def _(kind=kind, grow=grow, scale=scale):
            if kind == "plain":
                o_ref[...] = y.astype(o_ref.dtype)
                return
            for hh in range(tn // HEAD_DIM):
                sl = slice(hh * HEAD_DIM, (hh + 1) * HEAD_DIM)
                yh = y[:, sl]
                if kind == "norm_rope":
                    yh = _rms(yh, g_ref[grow:grow + 1, :])
                yh = _rope(yh, cos_ref[...], sin_ref[...])
                if scale != 1.0:
                    yh = yh * scale
                o_ref[:, sl] = yh.astype(o_ref.dtype)


def _proj(h, w3, layer_idx, col0, ncols, tn, segments, out_dtype, cos_t, sin_t, gains):
    m, k = h.shape
    assert col0 % tn == 0 and ncols % tn == 0
    n0 = col0 // tn
    tm = _row_block(m)
    return pl.pallas_call(
        functools.partial(_proj_kernel, segments=tuple(segments), tn=tn),
        grid=(ncols // tn, m // tm),
        in_specs=[pl.BlockSpec((tm, k), lambda n, i: (i, 0)),
                  pl.BlockSpec((None, k, tn), lambda n, i: (layer_idx, 0, n + n0)),
                  pl.BlockSpec((tm, HEAD_DIM), lambda n, i: (i, 0)),
                  pl.BlockSpec((tm, HEAD_DIM), lambda n, i: (i, 0)),
                  pl.BlockSpec((2, HEAD_DIM), lambda n, i: (0, 0))],
        out_specs=pl.BlockSpec((tm, tn), lambda n, i: (i, n)),
        out_shape=jax.ShapeDtypeStruct((m, ncols), out_dtype),
        scratch_shapes=[pltpu.VMEM((k, tn), BF16)],
        compiler_params=_params(2),
        name="proj",
    )(h, w3, cos_t, sin_t, gains)


WIN_QB = 2


def _win_kernel(sink_ref, q_ref, kp_ref, kc_ref, kn_ref, kx_ref, vp_ref, vc_ref, vn_ref, vx_ref, o_ref, *, group):
    kvh = pl.program_id(0)
    i = pl.program_id(1)
    nq = WIN_QB * BLK
    nk = (WIN_QB + 2) * BLK
    qi = lax.broadcasted_iota(I32, (group * nq, nk), 0) & (nq - 1)
    kj = lax.broadcasted_iota(I32, (group * nq, nk), 1)
    rel = kj - BLK - qi
    lo = jnp.where(i > 0, 0, BLK)
    hi = jnp.where(i < pl.num_programs(1) - 1, nk, nk - BLK)
    mask = (rel >= -BLK) & (rel <= BLK) & (kj >= lo) & (kj < hi)
    k_win = jnp.concatenate([kp_ref[...], kc_ref[...], kn_ref[...]], axis=0)
    v_win = jnp.concatenate([vp_ref[...], vc_ref[...], vn_ref[...]], axis=0)
    q = jnp.concatenate([q_ref[:, g * HEAD_DIM:(g + 1) * HEAD_DIM] for g in range(group)], axis=0)
    snk = jnp.concatenate([jnp.full((nq, 1), sink_ref[kvh * group + g] * LOG2E, F32) for g in range(group)], axis=0)
    s_loc = jnp.where(mask, _dot_nt(q, k_win), NEG_INF)
    s_ctx = _dot_nt(q, kx_ref[...])
    mx = jnp.maximum(jnp.maximum(jnp.max(s_loc, -1, keepdims=True), jnp.max(s_ctx, -1, keepdims=True)), snk)
    p_loc = jnp.exp2(s_loc - mx)
    p_ctx = jnp.exp2(s_ctx - mx)
    den = jnp.sum(p_loc, -1, keepdims=True) + jnp.sum(p_ctx, -1, keepdims=True) + jnp.exp2(snk - mx)
    o = (_dot(p_loc.astype(BF16), v_win) + _dot(p_ctx.astype(BF16), vx_ref[...])) / den
    for g in range(group):
        o_ref[:, g * HEAD_DIM:(g + 1) * HEAD_DIM] = o[g * nq:(g + 1) * nq, :].astype(o_ref.dtype)


def _window_attn(qkv, sink, n_ctx):
    m = qkv.shape[0]
    nb = (m - n_ctx) // BLK
    cb = n_ctx // BLK
    assert nb % WIN_QB == 0 and cb % WIN_QB == 0
    group = A_HEADS // A_KV
    kcol = A_Q // HEAD_DIM
    vcol = kcol + A_KV

    def edge_spec(col, block):
        return pl.BlockSpec((BLK, HEAD_DIM),
                            lambda h, i, s: (cb + jnp.clip(i * WIN_QB + block, 0, nb - 1), col + h))

    def own_spec(col):
        return pl.BlockSpec((WIN_QB * BLK, HEAD_DIM), lambda h, i, s: (cb // WIN_QB + i, col + h))

    def ctx_spec(col):
        return pl.BlockSpec((n_ctx, HEAD_DIM), lambda h, i, s: (0, col + h))

    grid_spec = pltpu.PrefetchScalarGridSpec(
        num_scalar_prefetch=1,
        grid=(A_KV, nb // WIN_QB),
        in_specs=[pl.BlockSpec((WIN_QB * BLK, group * HEAD_DIM), lambda h, i, s: (cb // WIN_QB + i, h)),
                  edge_spec(kcol, -1), own_spec(kcol), edge_spec(kcol, WIN_QB), ctx_spec(kcol),
                  edge_spec(vcol, -1), own_spec(vcol), edge_spec(vcol, WIN_QB), ctx_spec(vcol)],
        out_specs=pl.BlockSpec((WIN_QB * BLK, group * HEAD_DIM), lambda h, i, s: (i, h)),
    )
    return pl.pallas_call(
        functools.partial(_win_kernel, group=group),
        grid_spec=grid_spec,
        out_shape=jax.ShapeDtypeStruct((m - n_ctx, A_Q), BF16),
        compiler_params=_params(2),
        name="window_attn",
    )(sink, qkv, qkv, qkv, qkv, qkv, qkv, qkv, qkv, qkv)


def _flash_kernel(sink_ref, q_ref, k_ref, v_ref, o_ref, va_ref, m_ref, acc_ref, *, group, tq, tk, n_keys, has_sink):
    kvh = pl.program_id(0)
    rows = group * tq
    ncol = tk // HEAD_DIM

    @pl.when(pl.program_id(1) == 0)
    def _():
        va_ref[:, :HEAD_DIM] = v_ref[...]
        va_ref[:, HEAD_DIM:] = jnp.ones((n_keys, HEAD_DIM), BF16)

    q = jnp.concatenate([q_ref[:, g * HEAD_DIM:(g + 1) * HEAD_DIM] for g in range(group)], axis=0)
    m_ref[...] = jnp.full((rows, HEAD_DIM), NEG_INF, F32)
    acc_ref[...] = jnp.zeros((rows, 2 * HEAD_DIM), F32)

    def body(j, carry):
        off = pl.multiple_of(j * tk, tk)
        s = _dot_nt(q, k_ref[pl.ds(off, tk), :])
        cols = [s[:, c * HEAD_DIM:(c + 1) * HEAD_DIM] for c in range(ncol)]
        m_prev = m_ref[...]
        m_new = jnp.maximum(m_prev, jnp.max(functools.reduce(jnp.maximum, cols), -1, keepdims=True))
        alpha = jnp.exp2(m_prev - m_new)
        p = jnp.concatenate([jnp.exp2(c - m_new).astype(BF16) for c in cols], axis=1)
        acc_ref[...] = jnp.concatenate([alpha, alpha], axis=1) * acc_ref[...] + _dot(p, va_ref[pl.ds(off, tk), :])
        m_ref[...] = m_new
        return carry

    for j in range(n_keys // tk):
        body(j, 0)
    for g in range(group):
        rs = slice(g * tq, (g + 1) * tq)
        num = acc_ref[rs, :HEAD_DIM]
        den = acc_ref[rs, HEAD_DIM:]
        if has_sink:
            m_fin = m_ref[rs, :]
            snk = sink_ref[kvh * group + g] * LOG2E
            m_tot = jnp.maximum(m_fin, snk)
            scale = jnp.exp2(m_fin - m_tot)
            den = den * scale + jnp.exp2(snk - m_tot)
            num = num * scale
        o_ref[:, g * HEAD_DIM:(g + 1) * HEAD_DIM] = (num / den).astype(o_ref.dtype)


def _flash_attn(q_arr, q_col0, kv_arr, k_col0, v_col0, n_kv, group, sink, q_row0, n_q, n_keys, tq, tk):
    assert n_q % tq == 0 and q_row0 % tq == 0 and n_keys % tk == 0
    has_sink = sink is not None
    if sink is None:
        sink = jnp.zeros((n_kv * group,), F32)
    qb0 = q_row0 // tq
    qc0 = q_col0 // (group * HEAD_DIM)
    kc0 = k_col0 // HEAD_DIM
    vc0 = v_col0 // HEAD_DIM
    rows = group * tq
    grid_spec = pltpu.PrefetchScalarGridSpec(
        num_scalar_prefetch=1,
        grid=(n_kv, n_q // tq),
        in_specs=[pl.BlockSpec((tq, group * HEAD_DIM), lambda h, i, s: (qb0 + i, qc0 + h)),
                  pl.BlockSpec((n_keys, HEAD_DIM), lambda h, i, s: (0, kc0 + h)),
                  pl.BlockSpec((n_keys, HEAD_DIM), lambda h, i, s: (0, vc0 + h))],
        out_specs=pl.BlockSpec((tq, group * HEAD_DIM), lambda h, i, s: (i, h)),
        scratch_shapes=[pltpu.VMEM((n_keys, 2 * HEAD_DIM), BF16), pltpu.VMEM((rows, HEAD_DIM), F32),
                        pltpu.VMEM((rows, 2 * HEAD_DIM), F32)],
    )
    return pl.pallas_call(
        functools.partial(_flash_kernel, group=group, tq=tq, tk=tk, n_keys=n_keys, has_sink=has_sink),
        grid_spec=grid_spec,
        out_shape=jax.ShapeDtypeStruct((n_q, n_kv * group * HEAD_DIM), BF16),
        compiler_params=_params(2),
        name="flash_attn",
    )(sink, q_arr, kv_arr, kv_arr)


def _log_sigmoid(x):
    return jnp.minimum(x, 0.0) - jnp.log(1.0 + jnp.exp(-jnp.abs(x)))


def _dot_exact(a, b):
    return jnp.dot(a, b, preferred_element_type=F32, precision=lax.Precision.HIGHEST)


def _mlstm_head(q, k, v, i_col, i_row, b_col, b_row, b_tot, mask, c_ref, n_ref, m_ref):
    m_prev = m_ref[:, 0:1]
    c_prev = c_ref[...]
    n_prev = n_ref[...]
    qs = q * (B_DK ** -0.5)
    qb = qs.astype(BF16)
    dlog = jnp.where(mask, b_col - b_row + i_row, NEG_INF)
    inter = b_col + m_prev
    m_t = jnp.maximum(inter, jnp.max(dlog, -1, keepdims=True))
    dw = jnp.exp(dlog - m_t)
    iw = jnp.exp(inter - m_t)
    sc = _dot_nt(qb, k.astype(BF16)) * dw
    num = _dot(sc.astype(BF16), v.astype(BF16)) + iw * _dot(qb, c_prev.astype(BF16))
    den = jnp.sum(sc, -1, keepdims=True) + iw * jnp.sum(qs * n_prev, -1, keepdims=True)
    h = num / jnp.maximum(jnp.abs(den), jnp.exp(-m_t))
    glog_col = b_tot - b_col + i_col
    glog_row = b_tot - b_row + i_row
    m_new = jnp.maximum(b_tot + m_prev, jnp.max(glog_row, -1, keepdims=True))
    decay = jnp.exp(b_tot + m_prev - m_new)
    wk = jnp.exp(glog_col - m_new) * k
    c_ref[...] = decay * c_prev + lax.dot_general(wk.astype(BF16), v.astype(BF16), (((0,), (0,)), ((), ())),
                                                  preferred_element_type=F32)
    n_ref[...] = decay * n_prev + jnp.sum(wk, 0, keepdims=True)
    m_ref[...] = jnp.broadcast_to(m_new, m_ref.shape)
    return h


def _mlstm_kernel(brow_ref, bcol_ref,
                  qf_ref, kf_ref, vf_ref, gf_ref, gtf_ref,
                  qb_ref, kb_ref, vb_ref, gb_ref, gtb_ref,
                  of_ref, ob_ref, c_ref, n_ref, m_ref):
    L = MLSTM_CHUNK

    @pl.when(pl.program_id(0) == 0)
    def _():
        c_ref[...] = jnp.zeros(c_ref.shape, F32)
        n_ref[...] = jnp.zeros(n_ref.shape, F32)
        m_ref[...] = jnp.zeros(m_ref.shape, F32)

    r = lax.broadcasted_iota(I32, (L, L), 0)
    cidx = lax.broadcasted_iota(I32, (L, L), 1)
    lane = lax.broadcasted_iota(I32, (L, HEAD_DIM), 1)
    sub = lax.broadcasted_iota(I32, (2 * 2 * B_HEADS, L), 0)
    for d, (q_ref, k_ref, v_ref, g_ref, gt_ref, o_ref) in enumerate(
            ((qf_ref, kf_ref, vf_ref, gf_ref, gtf_ref, of_ref), (qb_ref, kb_ref, vb_ref, gb_ref, gtb_ref, ob_ref))):
        mask = (cidx <= r) if d == 0 else (cidx >= r)
        gc = g_ref[...] + brow_ref[...]
        gc = jnp.where(lane < 2 * B_HEADS, gc, _log_sigmoid(gc))
        gr = gt_ref[...] + bcol_ref[...]
        gr = jnp.where(sub < 2 * B_HEADS, gr, _log_sigmoid(gr))
        mask_t = (r <= cidx) if d == 0 else (r >= cidx)
        bc_all = _dot_exact(mask.astype(F32), gc)
        br_all = _dot_exact(gr, mask_t.astype(F32))
        edge = L - 1 if d == 0 else 0
        for hd in range(B_HEADS):
            ci = d * B_HEADS + hd
            cf = 2 * B_HEADS + ci
            b_col = bc_all[:, cf:cf + 1]
            h = _mlstm_head(q_ref[:, hd * B_DK:(hd + 1) * B_DK], k_ref[:, hd * B_DK:(hd + 1) * B_DK],
                            v_ref[:, hd * B_DV:(hd + 1) * B_DV],
                            gc[:, ci:ci + 1], gr[ci:ci + 1, :], b_col, br_all[cf:cf + 1, :],
                            b_col[edge:edge + 1, :], mask, c_ref.at[ci], n_ref.at[ci], m_ref.at[ci])
            o_ref[:, hd * B_DV:(hd + 1) * B_DV] = h


def _mlstm(qkvo, gates, gates_t, bias_row, bias_col, n_ctx):
    m = qkvo.shape[0]
    L = MLSTM_CHUNK
    nc = m // L
    ncc = n_ctx // L
    kq = B_Q // B_Q
    kv = (2 * B_Q) // B_V

    def fw(s):
        return s

    def bw(s):
        return jnp.where(s < ncc, ncc - 1 - s, nc - 1 + ncc - s)

    def specs(order):
        return [pl.BlockSpec((L, B_Q), lambda s: (order(s), 0)),
                pl.BlockSpec((L, B_Q), lambda s: (order(s), kq)),
                pl.BlockSpec((L, B_V), lambda s: (order(s), kv)),
                pl.BlockSpec((L, HEAD_DIM), lambda s: (order(s), 0)),
                pl.BlockSpec((4 * B_HEADS, L), lambda s: (0, order(s)))]

    nst = 2 * B_HEADS
    return pl.pallas_call(
        _mlstm_kernel,
        grid=(nc,),
        in_specs=[pl.BlockSpec((1, HEAD_DIM), lambda s: (0, 0)),
                  pl.BlockSpec((4 * B_HEADS, 1), lambda s: (0, 0))] + specs(fw) + specs(bw),
        out_specs=[pl.BlockSpec((L, B_V), lambda s: (fw(s), 0)),
                   pl.BlockSpec((L, B_V), lambda s: (bw(s), 0))],
        out_shape=[jax.ShapeDtypeStruct((m, B_V), F32), jax.ShapeDtypeStruct((m, B_V), F32)],
        scratch_shapes=[pltpu.VMEM((nst, B_DK, B_DV), F32), pltpu.VMEM((nst, 1, B_DK), F32),
                        pltpu.VMEM((nst, 1, HEAD_DIM), F32)],
        compiler_params=_params(1),
        name="mlstm",
    )(bias_row, bias_col, qkvo, qkvo, qkvo, gates, gates_t, qkvo, qkvo, qkvo, gates, gates_t)


def _bout_kernel(hf_ref, hb_ref, o_ref, g_ref, y_ref):
    for hd in range(B_HEADS):
        sl = slice(hd * B_DV, (hd + 1) * B_DV)
        hn = _rms(hf_ref[:, sl] + hb_ref[:, sl], g_ref[:, sl])
        y_ref[:, sl] = (jax.nn.sigmoid(o_ref[:, sl]) * hn).astype(y_ref.dtype)


def _b_out(hf, hb, qkvo, out_g):
    m = hf.shape[0]
    ocol = (2 * B_Q + B_V) // B_V
    row = pl.BlockSpec((TM, B_V), lambda i: (i, 0))
    return pl.pallas_call(
        _bout_kernel,
        grid=(m // TM,),
        in_specs=[row, row, pl.BlockSpec((TM, B_V), lambda i: (i, ocol)),
                  pl.BlockSpec((1, B_V), lambda i: (0, 0))],
        out_specs=row,
        out_shape=jax.ShapeDtypeStruct((m, B_V), BF16),
        compiler_params=_params(1),
        name="mlstm_out",
    )(hf, hb, qkvo, out_g)


def _ffn_kernel(ge_ref, gb_ref, gn_ref, x_ref, wg_ref, wu_ref, wd_ref, o_ref, wgb_ref, wub_ref, wdb_ref, *, n_sub):
    g = pl.program_id(0)
    j = pl.program_id(1)
    gn = gn_ref[g]

    @pl.when(gn > 0)
    def _():
        wgb_ref[...] = wg_ref[...].astype(BF16)
        wub_ref[...] = wu_ref[...].astype(BF16)
        wdb_ref[...] = wd_ref[...].astype(BF16)

    def rows_step(off, rows, first):
        x = x_ref[pl.ds(off, rows), :]
        a = _dot(x, wgb_ref[...])
        b = _dot(x, wub_ref[...])
        y = _dot((a * jax.nn.sigmoid(a) * b).astype(BF16), wdb_ref[...])
        if first:
            o_ref[pl.ds(off, rows), :] = y
        else:
            o_ref[pl.ds(off, rows), :] += y

    def used_tiles(first):
        def pair(rp, c):
            rows_step(pl.multiple_of(rp * (2 * TM), 2 * TM), 2 * TM, first)
            return c

        lax.fori_loop(0, lax.shift_right_logical(gn, 1), pair, 0)

        @pl.when((gn & 1) == 1)
        def _():
            rows_step(pl.multiple_of((gn - 1) * TM, TM), TM, first)

    @pl.when(j == 0)
    def _():
        used_tiles(True)

        def zero(r, c):
            off = pl.multiple_of(r * TM, TM)
            o_ref[pl.ds(off, TM), :] = jnp.zeros((TM, o_ref.shape[1]), F32)
            return c

        lax.fori_loop(gn, n_sub, zero, 0)

    @pl.when(j > 0)
    def _():
        used_tiles(False)


def _ffn(x, wg, wu, wd, e0, grp_e, grp_b, grp_n, n_sub, tf):
    p, d = x.shape
    f = wg.shape[-1]
    rg = n_sub * TM
    assert f % tf == 0 and p % rg == 0
    once = pl.Buffered(1)
    nj = f // tf

    def slice_of(j, g, gn):
        return jnp.where(gn[g] > 0, j, nj - 1)

    grid_spec = pltpu.PrefetchScalarGridSpec(
        num_scalar_prefetch=3,
        grid=(grp_e.shape[0], nj),
        in_specs=[pl.BlockSpec((rg, d), lambda g, j, ge, gb, gn: (gb[g], 0), pipeline_mode=once),
                  pl.BlockSpec((None, d, tf), lambda g, j, ge, gb, gn: (e0 + ge[g], 0, slice_of(j, g, gn))),
                  pl.BlockSpec((None, d, tf), lambda g, j, ge, gb, gn: (e0 + ge[g], 0, slice_of(j, g, gn))),
                  pl.BlockSpec((None, tf, d), lambda g, j, ge, gb, gn: (e0 + ge[g], slice_of(j, g, gn), 0))],
        out_specs=pl.BlockSpec((rg, d), lambda g, j, ge, gb, gn: (g, 0), pipeline_mode=once),
        scratch_shapes=[pltpu.VMEM((d, tf), BF16), pltpu.VMEM((d, tf), BF16), pltpu.VMEM((tf, d), BF16)],
    )
    return pl.pallas_call(
        functools.partial(_ffn_kernel, n_sub=n_sub),
        grid_spec=grid_spec,
        out_shape=jax.ShapeDtypeStruct((p, d), F32),
        compiler_params=_params(2),
        name="ffn",
    )(grp_e, grp_b, grp_n, x, wg, wu, wd)


def _dense_groups(m):
    n_sub = next(r for r in (11, 3, 2, 1) if m % (r * TM) == 0)
    n_groups = m // (n_sub * TM)
    return (jnp.zeros((n_groups,), I32), jnp.arange(n_groups, dtype=I32), jnp.full((n_groups,), n_sub, I32)), n_sub


def _router_kernel(h_ref, r_ref, idx_ref, w_ref):
    logits = _dot(h_ref[...].astype(BF16), r_ref[...].astype(BF16))
    lane = lax.broadcasted_iota(I32, logits.shape, 1).astype(F32)
    big = float(HEAD_DIM)
    lg = jnp.where(lane < N_EXPERTS, logits, NEG_INF)
    m1 = jnp.max(lg, -1, keepdims=True)
    i1 = jnp.min(jnp.where(lg == m1, lane, big), -1, keepdims=True)
    lg2 = jnp.where(lane == i1, NEG_INF, lg)
    m2 = jnp.max(lg2, -1, keepdims=True)
    i2 = jnp.min(jnp.where(lg2 == m2, lane, big), -1, keepdims=True)
    e2 = jnp.exp(m2 - m1)
    den = 1.0 + e2
    idx_ref[...] = jnp.where(lane == 0.0, i1, jnp.where(lane == 1.0, i2, 0.0)).astype(I32)
    w_ref[...] = jnp.where(lane == 0.0, 1.0 / den, jnp.where(lane == 1.0, e2 / den, 0.0))


def _router(h, router_p):
    m, d = h.shape
    row = pl.BlockSpec((TM, HEAD_DIM), lambda i: (i, 0))
    return pl.pallas_call(
        _router_kernel,
        grid=(m // TM,),
        in_specs=[pl.BlockSpec((TM, d), lambda i: (i, 0)), pl.BlockSpec((d, HEAD_DIM), lambda i: (0, 0))],
        out_specs=[row, row],
        out_shape=[jax.ShapeDtypeStruct((m, HEAD_DIM), I32), jax.ShapeDtypeStruct((m, HEAD_DIM), F32)],
        compiler_params=_params(1),
        name="router",
    )(h, router_p)


def _moe_plan(eidx, ew, n_groups, n_sub):
    m = eidx.shape[0]
    rg = n_sub * TM
    e_flat = eidx.reshape(-1)
    experts = jnp.arange(N_EXPERTS, dtype=I32)
    onehot = (e_flat[:, None] == experts[None, :]).astype(I32)
    csum = jnp.cumsum(onehot, axis=0)
    rank = jnp.sum(csum * onehot, axis=1) - 1
    counts = csum[-1]
    groups_e = (counts + rg - 1) // rg
    grp_end = jnp.cumsum(groups_e)
    grp_start = grp_end - groups_e
    pos = jnp.sum(onehot * grp_start[None, :], axis=1) * rg + rank
    token = jnp.arange(2 * m, dtype=I32) // 2
    src = jnp.zeros((n_groups * rg,), I32).at[pos].set(token)
    gids = jnp.arange(n_groups, dtype=I32)
    valid = gids < grp_end[-1]
    grp_e = jnp.sum((gids[:, None] >= grp_end[None, :]).astype(I32), axis=1)
    grp_e = jnp.where(valid, grp_e, jnp.max(jnp.where(counts > 0, experts, 0)))
    grp_b = jnp.minimum(gids, grp_end[-1] - 1).astype(I32)
    mine = (grp_e[:, None] == experts[None, :]).astype(I32)
    rows_left = jnp.sum(mine * counts[None, :], axis=1) - (gids - jnp.sum(mine * grp_start[None, :], axis=1)) * rg
    grp_n = jnp.where(valid, jnp.clip((rows_left + TM - 1) // TM, 0, n_sub), 0).astype(I32)
    tile_valid = (jnp.arange(n_groups * n_sub, dtype=I32) % n_sub < jnp.repeat(grp_n, n_sub)).astype(I32)
    pos2 = pos.reshape(m, 2).astype(I32)
    return src, (grp_e.astype(I32), grp_b, grp_n), tile_valid, pos2[:, 0], pos2[:, 1]


ROW_DMA_UNROLL = 8


def _gather_kernel(src_ref, tv_ref, h_hbm, o_ref, buf_ref, sem):
    t = pl.program_id(0)
    nt = pl.num_programs(0)

    def row_copy(row, slot, r):
        return pltpu.make_async_copy(h_hbm.at[pl.ds(row, 1), :], buf_ref.at[slot, pl.ds(r, 1), :], sem.at[slot])

    def fetch(tile):
        slot = tile % 2

        @pl.when(tv_ref[tile] == 1)
        def _():
            def issue(r, c):
                row_copy(src_ref[tile * TM + r], slot, r).start()
                return c

            lax.fori_loop(0, TM, issue, 0, unroll=ROW_DMA_UNROLL)

    @pl.when(t == 0)
    def _():
        fetch(t)

    @pl.when(t + 1 < nt)
    def _():
        fetch(t + 1)

    @pl.when(tv_ref[t] == 1)
    def _():
        slot = t % 2

        def wait(r, c):
            row_copy(0, slot, r).wait()
            return c

        lax.fori_loop(0, TM, wait, 0, unroll=ROW_DMA_UNROLL)
        o_ref[...] = buf_ref[slot].astype(o_ref.dtype)

    @pl.when(tv_ref[t] == 0)
    def _():
        o_ref[...] = jnp.zeros(o_ref.shape, o_ref.dtype)


def _gather_rows(h, src, tile_valid):
    d = h.shape[1]
    p = src.shape[0]
    grid_spec = pltpu.PrefetchScalarGridSpec(
        num_scalar_prefetch=2,
        grid=(p // TM,),
        in_specs=[pl.BlockSpec(memory_space=pl.ANY)],
        out_specs=pl.BlockSpec((TM, d), lambda t, s, tv: (t, 0)),
        scratch_shapes=[pltpu.VMEM((2, TM, d), F32), pltpu.SemaphoreType.DMA((2,))],
    )
    return pl.pallas_call(
        _gather_kernel,
        grid_spec=grid_spec,
        out_shape=jax.ShapeDtypeStruct((p, d), BF16),
        compiler_params=_params(1),
        name="moe_gather",
    )(src, tile_valid, h)


def _combine_kernel(p0_ref, p1_ref, ys_hbm, ew_ref, x_ref, gpost_ref, gpre_ref, mpost_ref, mpre_ref, xo_ref, *rest,
                    with_pre, tile0):
    if with_pre:
        h_ref, buf_ref, sem = rest
    else:
        h_ref = None
        buf_ref, sem = rest
    t = pl.program_id(0)
    nt = pl.num_programs(0)

    def row_copy(row, slot, k, r):
        return pltpu.make_async_copy(ys_hbm.at[pl.ds(row, 1), :], buf_ref.at[slot, k, pl.ds(r, 1), :], sem.at[slot])

    def fetch(tile):
        slot = tile % 2
        base = (tile + tile0) * TM

        def issue(r, c):
            row_copy(p0_ref[base + r], slot, 0, r).start()
            row_copy(p1_ref[base + r], slot, 1, r).start()
            return c

        lax.fori_loop(0, TM, issue, 0, unroll=ROW_DMA_UNROLL)

    @pl.when(t == 0)
    def _():
        fetch(t)

    @pl.when(t + 1 < nt)
    def _():
        fetch(t + 1)

    slot = t % 2

    def wait(r, c):
        row_copy(0, slot, 0, r).wait()
        row_copy(0, slot, 1, r).wait()
        return c

    lax.fori_loop(0, TM, wait, 0, unroll=ROW_DMA_UNROLL)
    f = ew_ref[:, 0:1] * buf_ref[slot, 0] + ew_ref[:, 1:2] * buf_ref[slot, 1]
    _post_pre_math(x_ref[...], f, gpost_ref, gpre_ref, mpost_ref, mpre_ref, xo_ref, h_ref, 1)


def _combine_post_pre(x, ys, pos0, pos1, ew, norm_g, mods, layer, n_ctx_tiles, h_dtype, tile0):
    m, d = x.shape
    pre_layer = min(layer + 1, norm_g.shape[0] - 1)
    with_pre = h_dtype is not None
    row_in = pl.BlockSpec((TM, d), lambda i, a, b: (i + tile0, 0))
    row_out = pl.BlockSpec((TM, d), lambda i, a, b: (i, 0))
    m_out = m - tile0 * TM

    def mod_spec(l):
        return pl.BlockSpec((None, None, 6, d), lambda i, a, b: (l, jnp.where(i + tile0 < n_ctx_tiles, 1, 0), 0, 0))

    out_shape = [jax.ShapeDtypeStruct((m_out, d), F32)]
    out_specs = [row_out]
    if with_pre:
        out_shape.append(jax.ShapeDtypeStruct((m_out, d), h_dtype))
        out_specs.append(row_out)
    grid_spec = pltpu.PrefetchScalarGridSpec(
        num_scalar_prefetch=2,
        grid=(m_out // TM,),
        in_specs=[pl.BlockSpec(memory_space=pl.ANY),
                  pl.BlockSpec((TM, HEAD_DIM), lambda i, a, b: (i + tile0, 0)), row_in,
                  pl.BlockSpec((None, 4, d), lambda i, a, b: (layer, 0, 0)),
                  pl.BlockSpec((None, 4, d), lambda i, a, b: (pre_layer, 0, 0)),
                  mod_spec(layer), mod_spec(pre_layer)],
        out_specs=out_specs,
        scratch_shapes=[pltpu.VMEM((2, 2, TM, d), F32), pltpu.SemaphoreType.DMA((2,))],
    )
    res = pl.pallas_call(
        functools.partial(_combine_kernel, with_pre=with_pre, tile0=tile0),
        grid_spec=grid_spec,
        out_shape=out_shape,
        compiler_params=_params(1),
        name="moe_combine",
    )(pos0, pos1, ys, ew, x, norm_g, norm_g, mods, mods)
    return res if with_pre else (res[0], None)


def _rope_tables(n_ctx, s):
    pos = jnp.arange(s)
    row = (pos // GRID_W).astype(F32)
    col = (pos % GRID_W).astype(F32)
    inv = ROPE_THETA ** (-jnp.arange(ROPE_FREQS, dtype=F32) / ROPE_FREQS)
    ar, ac = row[:, None] * inv, col[:, None] * inv
    cos = jnp.concatenate([jnp.cos(ar), jnp.cos(ar), jnp.cos(ac), jnp.cos(ac)], axis=-1)
    sin = jnp.concatenate([-jnp.sin(ar), jnp.sin(ar), -jnp.sin(ac), jnp.sin(ac)], axis=-1)
    cos = jnp.concatenate([jnp.ones((n_ctx, HEAD_DIM), F32), cos], axis=0)
    sin = jnp.concatenate([jnp.zeros((n_ctx, HEAD_DIM), F32), sin], axis=0)
    return cos, sin


def _layer_ab(h, j, ab_w_in, a_sink, b_ig_bias, b_fg_bias, b_norm_g, cos_t, sin_t, n_ctx):
    scale = HEAD_DIM ** -0.5 * LOG2E
    no_g = jnp.ones((2, HEAD_DIM), F32)
    tn = 256
    qt, kt = A_Q // tn, (A_Q + A_KV * HEAD_DIM) // tn
    a_qkv = _proj(h, ab_w_in, j, 0, A_QKV, tn,
                  [(0, qt, "rope", 0, scale), (qt, kt, "rope", 0, 1.0), (kt, A_QKV // tn, "plain", 0, 1.0)],
                  BF16, cos_t, sin_t, no_g)
    b_qkvo = _proj(h, ab_w_in, j, A_QKV, B_QKVO, 512, [(0, B_QKVO // 512, "plain", 0, 1.0)],
                   F32, cos_t, sin_t, no_g)
    ngate = 4 * B_HEADS
    w_gate = jnp.pad(ab_w_in[j, :, A_QKV + B_QKVO:], ((0, 0), (0, HEAD_DIM - ngate)))[None]
    gates = _proj(h, w_gate, 0, 0, HEAD_DIM, HEAD_DIM, [(0, 1, "plain", 0, 1.0)], F32, cos_t, sin_t, no_g)

    ya_l = _window_attn(a_qkv, a_sink[j], n_ctx)
    ya_c = _flash_attn(a_qkv, 0, a_qkv, A_Q, A_Q + A_KV * HEAD_DIM, A_KV, A_HEADS // A_KV, a_sink[j],
                       0, n_ctx, n_ctx, n_ctx, n_ctx)
    ya = jnp.concatenate([ya_c, ya_l], axis=0)

    bias = jnp.concatenate([b_ig_bias[j].reshape(-1), b_fg_bias[j].reshape(-1)]).astype(F32)
    bias_row = jnp.pad(bias, (0, HEAD_DIM - ngate))[None, :]
    hf, hb = _mlstm(b_qkvo, gates, jnp.transpose(gates[:, :ngate]), bias_row, bias[:, None], n_ctx)
    yb = _b_out(hf, hb, b_qkvo, b_norm_g[j][None, :])
    return jnp.concatenate([ya, yb], axis=1)


def _layer_c(h, j, c_w_qkv, c_qk_g, cos_t, sin_t, n_ctx, need_ctx):
    scale = HEAD_DIM ** -0.5 * LOG2E
    tn = 512
    qt, kt = C_Q // tn, (C_Q + C_KVW) // tn
    qkv = _proj(h, c_w_qkv, j, 0, C_Q + 2 * C_KVW, tn,
                [(0, qt, "norm_rope", 0, scale), (qt, kt, "norm_rope", 1, 1.0), (kt, kt + C_KVW // tn, "plain", 0, 1.0)],
                BF16, cos_t, sin_t, c_qk_g[j])
    m = h.shape[0]
    group = C_HEADS // C_KV
    tk = 768 if m % 768 == 0 else TM
    y_l = _flash_attn(qkv, 0, qkv, C_Q, C_Q + C_KVW, C_KV, group, None, n_ctx, m - n_ctx, m, TM, tk)
    if need_ctx:
        y_c = _flash_attn(qkv, 0, qkv, C_Q, C_Q + C_KVW, C_KV, group, None, 0, n_ctx, n_ctx, n_ctx, n_ctx)
    else:
        y_c = jnp.zeros((n_ctx, C_Q), BF16)
    return jnp.concatenate([y_c, y_l], axis=0)


def kernel(x, c, ctx, c_ctx, ada_w, ada_b, norm_g, ab_w_in, ab_w_out, a_sink, b_ig_bias, b_fg_bias, b_norm_g,
           c_w_qkv, c_w_out, c_qk_g, ffn_w_gate, ffn_w_up, ffn_w_down, moe_router, moe_w_gate, moe_w_up,
           moe_w_down):
    depth = ada_w.shape[0]
    s, d = x.shape[1], x.shape[2]
    n_ctx = ctx.shape[1]
    m = n_ctx + s
    assert x.shape[0] == 1 and n_ctx % TM == 0 and s % TM == 0
    n_ctx_tiles = n_ctx // TM
    n_tiles = m // TM
    no_g = jnp.ones((2, HEAD_DIM), F32)

    mods = _ada_mods(c, c_ctx, ada_w, ada_b)
    cos_t, sin_t = _rope_tables(n_ctx, s)
    xs = jnp.concatenate([ctx[0], x[0]], axis=0)
    h = _prenorm(xs, norm_g, mods, 0, n_ctx_tiles)
    dense_groups, dense_sub = _dense_groups(m)
    n_exp = moe_w_gate.shape[1]
    moe_g = moe_w_gate.reshape((-1,) + moe_w_gate.shape[2:])
    moe_u = moe_w_up.reshape((-1,) + moe_w_up.shape[2:])
    moe_d = moe_w_down.reshape((-1,) + moe_w_down.shape[2:])

    for layer in range(depth):
        last = layer == depth - 1
        j = layer // 2
        even = layer % 2 == 0
        if even:
            mix = _layer_ab(h, j, ab_w_in, a_sink, b_ig_bias, b_fg_bias, b_norm_g, cos_t, sin_t, n_ctx)
            w_out = ab_w_out
        else:
            mix = _layer_c(h, j, c_w_qkv, c_qk_g, cos_t, sin_t, n_ctx, not last)
            w_out = c_w_out
        y = _proj(mix, w_out, j, 0, d, 512 if d % 512 == 0 else d, [(0, max(d // 512, 1), "plain", 0, 1.0)],
                  F32, cos_t, sin_t, no_g)
        xs, h = _post_pre(xs, y, norm_g, mods, layer, 0, n_ctx_tiles, BF16 if even else F32)
        if even:
            f_dim = ffn_w_gate.shape[-1]
            f = _ffn(h, ffn_w_gate, ffn_w_up, ffn_w_down, j, *dense_groups, dense_sub, FFN_SLICE)
            xs, h = _post_pre(xs, f, norm_g, mods, layer, 1, n_ctx_tiles, None if last else BF16)
        else:
            router_p = jnp.pad(moe_router[j], ((0, 0), (0, HEAD_DIM - n_exp)))
            eidx, ew = _router(h, router_p)
            n_sub = MOE_GROUP_TILES
            n_groups = -(-2 * m // (n_sub * TM)) + n_exp
            src, groups, tile_valid, pos0, pos1 = _moe_plan(eidx[:, :2], ew[:, :2], n_groups, n_sub)
            x_s = _gather_rows(h, src, tile_valid)
            y_s = _ffn(x_s, moe_g, moe_u, moe_d, j * n_exp, *groups, n_sub, FFN_SLICE)
            xs, h = _combine_post_pre(xs, y_s, pos0, pos1, ew, norm_g, mods, layer, n_ctx_tiles,
                                      None if last else BF16, n_ctx_tiles if last else 0)
    return (xs if xs.shape[0] == s else xs[n_ctx:])[None]
```

```python
import functools

import jax
import jax.numpy as jnp
import numpy as np
from jax import lax
from jax.experimental import pallas as pl
from jax.experimental.pallas import tpu as pltpu

F32 = jnp.float32
BF16 = jnp.bfloat16
I32 = jnp.int32

EPS = 1e-6
GRID_W = 64
HEAD_DIM = 128
ROPE_THETA = 10000.0
ROPE_FREQS = HEAD_DIM // 4
BLK = 128
A_HEADS = 8
A_KV = 2
B_HEADS = 4
B_DK = 128
B_DV = 256
MLSTM_CHUNK = 128
C_HEADS = 16
C_KV = 4
N_EXPERTS = 8

A_Q = A_HEADS * HEAD_DIM
A_QKV = A_Q + 2 * A_KV * HEAD_DIM
B_Q = B_HEADS * B_DK
B_V = B_HEADS * B_DV
B_QKVO = 2 * B_Q + 2 * B_V
C_Q = C_HEADS * HEAD_DIM
C_KVW = C_KV * HEAD_DIM

TM = 256
MOE_GROUP_TILES = 9
PROJ_BLOCKS = ((1408, 352), (768, 384))
FFN_SLICE = 256
VMEM_LIMIT = 56 * 1024 * 1024
NEG_INF = float("-inf")
LOG2E = 1.4426950408889634


def _row_block(m):
    return next(r * TM for r in (3, 2, 1) if m % (r * TM) == 0)


def _params(n_axes, vmem=VMEM_LIMIT):
    return pltpu.CompilerParams(dimension_semantics=("arbitrary",) * n_axes, vmem_limit_bytes=vmem)


def _rms(x, g):
    return x * lax.rsqrt(jnp.mean(x * x, axis=-1, keepdims=True) + EPS) * g


def _dot(a, b):
    return jnp.dot(a, b, preferred_element_type=F32)


def _dot_nt(a, b):
    return lax.dot_general(a, b, (((1,), (1,)), ((), ())), preferred_element_type=F32)


def _ada_kernel(s_ref, w_ref, b_ref, o_ref):
    s = s_ref[...]
    s = s * jax.nn.sigmoid(s)
    o_ref[...] = _dot(s.astype(BF16), w_ref[...].astype(BF16)) + b_ref[...]


def _ada_mods(c, c_ctx, ada_w, ada_b):
    depth, d, d6 = ada_w.shape
    tn = 1536 if d6 % 1536 == 0 else d6
    s = jnp.zeros((8, d), F32).at[0].set(c[0]).at[1].set(c_ctx)
    out = pl.pallas_call(
        _ada_kernel,
        grid=(depth, d6 // tn),
        in_specs=[pl.BlockSpec((8, d), lambda l, n: (0, 0)),
                  pl.BlockSpec((None, d, tn), lambda l, n: (l, 0, n)),
                  pl.BlockSpec((None, 1, tn), lambda l, n: (l, 0, n))],
        out_specs=pl.BlockSpec((None, 8, tn), lambda l, n: (l, 0, n)),
        out_shape=jax.ShapeDtypeStruct((depth, 8, d6), F32),
        compiler_params=_params(2),
        name="ada_mods",
    )(s, ada_w, ada_b.reshape(depth, 1, d6))
    return out[:, :2].reshape(depth, 2, 6, d)


def _mod_spec(d, layer, n_ctx_tiles):
    return pl.BlockSpec((None, None, 6, d), lambda i: (layer, jnp.where(i < n_ctx_tiles, 1, 0), 0, 0))


def _prenorm_kernel(x_ref, g_ref, mod_ref, h_ref):
    hn = _rms(x_ref[...], g_ref[0:1, :])
    h_ref[...] = (hn * (1.0 + mod_ref[1:2, :]) + mod_ref[0:1, :]).astype(h_ref.dtype)


def _prenorm(x, norm_g, mods, layer, n_ctx_tiles):
    m, d = x.shape
    return pl.pallas_call(
        _prenorm_kernel,
        grid=(m // TM,),
        in_specs=[pl.BlockSpec((TM, d), lambda i: (i, 0)),
                  pl.BlockSpec((None, 4, d), lambda i: (layer, 0, 0)),
                  _mod_spec(d, layer, n_ctx_tiles)],
        out_specs=pl.BlockSpec((TM, d), lambda i: (i, 0)),
        out_shape=jax.ShapeDtypeStruct((m, d), BF16),
        compiler_params=_params(1),
        name="prenorm",
    )(x, norm_g, mods)


def _post_pre_math(x, y, gpost_ref, gpre_ref, mpost_ref, mpre_ref, xo_ref, h_ref, sub):
    g_row = 1 + 2 * sub
    gate_row = 2 + 3 * sub
    xn = x + mpost_ref[gate_row:gate_row + 1, :] * _rms(y, gpost_ref[g_row:g_row + 1, :])
    xo_ref[...] = xn
    if h_ref is not None:
        nsub = 1 - sub
        hn = _rms(xn, gpre_ref[2 * nsub:2 * nsub + 1, :])
        h_ref[...] = (hn * (1.0 + mpre_ref[3 * nsub + 1:3 * nsub + 2, :])
                      + mpre_ref[3 * nsub:3 * nsub + 1, :]).astype(h_ref.dtype)


def _post_pre_kernel(x_ref, y_ref, gpost_ref, gpre_ref, mpost_ref, mpre_ref, xo_ref, *h_ref, sub):
    _post_pre_math(x_ref[...], y_ref[...], gpost_ref, gpre_ref, mpost_ref, mpre_ref, xo_ref,
                   h_ref[0] if h_ref else None, sub)


def _post_pre(x, y, norm_g, mods, layer, sub, n_ctx_tiles, h_dtype):
    m, d = x.shape
    pre_layer = layer if sub == 0 else min(layer + 1, norm_g.shape[0] - 1)
    row = pl.BlockSpec((TM, d), lambda i: (i, 0))
    out_shape = [jax.ShapeDtypeStruct((m, d), F32)]
    out_specs = [row]
    if h_dtype is not None:
        out_shape.append(jax.ShapeDtypeStruct((m, d), h_dtype))
        out_specs.append(row)
    res = pl.pallas_call(
        functools.partial(_post_pre_kernel, sub=sub),
        grid=(m // TM,),
        in_specs=[row, row,
                  pl.BlockSpec((None, 4, d), lambda i: (layer, 0, 0)),
                  pl.BlockSpec((None, 4, d), lambda i: (pre_layer, 0, 0)),
                  _mod_spec(d, layer, n_ctx_tiles),
                  _mod_spec(d, pre_layer, n_ctx_tiles)],
        out_specs=out_specs,
        out_shape=out_shape,
        compiler_params=_params(1),
        name="post_pre",
    )(x, y, norm_g, norm_g, mods, mods)
    return res if h_dtype is not None else (res[0], None)


def _rope(y, cos, sin):
    lane = lax.broadcasted_iota(I32, y.shape, 1)
    lower = (lane & (2 * ROPE_FREQS - 1)) < ROPE_FREQS
    partner = jnp.where(lower, pltpu.roll(y, HEAD_DIM - ROPE_FREQS, 1), pltpu.roll(y, ROPE_FREQS, 1))
    return y * cos + partner * sin


PLAIN = ("plain", 0, 1.0)


def _proj_kernel(x_ref, w_ref, cos_ref, sin_ref, g_ref, o_ref, wb_ref, *, tile_kinds, rows):
    n = pl.program_id(0)

    @pl.when(pl.program_id(1) == 0)
    def _():
        wb_ref[...] = w_ref[...].astype(BF16)

    def run(kinds):
        for c in range(x_ref.shape[0] // rows):
            rs = slice(c * rows, (c + 1) * rows)
            y = _dot(x_ref[rs, :], wb_ref[...])
            if all(kd == PLAIN for kd in kinds):
                o_ref[rs, :] = y.astype(o_ref.dtype)
                continue
            for hh, (kind, grow, scale) in enumerate(kinds):
                sl = slice(hh * HEAD_DIM, (hh + 1) * HEAD_DIM)
                yh = y[:, sl]
                if kind == "norm_rope":
                    yh = _rms(yh, g_ref[grow:grow + 1, :])
                if kind != "plain":
                    yh = _rope(yh, cos_ref[rs, :], sin_ref[rs, :])
                if scale != 1.0:
                    yh = yh * scale
                o_ref[rs, sl] = yh.astype(o_ref.dtype)

    t0 = 0
    while t0 < len(tile_kinds):
        t1 = t0 + 1
        while t1 < len(tile_kinds) and tile_kinds[t1] == tile_kinds[t0]:
            t1 += 1
        pl.when((n >= t0) & (n < t1))(functools.partial(run, tile_kinds[t0]))
        t0 = t1


def _proj(h, w3, layer_idx, col0, head_kinds, tn, out_dtype, cos_t, sin_t, gains):
    m, k = h.shape
    ncols = HEAD_DIM * len(head_kinds)
    assert col0 % tn == 0 and ncols % tn == 0
    n0 = col0 // tn
    hpt = tn // HEAD_DIM
    tile_kinds = tuple(tuple(head_kinds[t * hpt:(t + 1) * hpt]) for t in range(ncols // tn))
    tm, rows = next(((a, b) for a, b in PROJ_BLOCKS if m % a == 0), (TM, TM))
    return pl.pallas_call(
        functools.partial(_proj_kernel, tile_kinds=tile_kinds, rows=rows),
        grid=(ncols // tn, m // tm),
        in_specs=[pl.BlockSpec((tm, k), lambda n, i: (i, 0)),
                  pl.BlockSpec((None, k, tn), lambda n, i: (layer_idx, 0, n + n0)),
                  pl.BlockSpec((tm, HEAD_DIM), lambda n, i: (i, 0)),
                  pl.BlockSpec((tm, HEAD_DIM), lambda n, i: (i, 0)),
                  pl.BlockSpec((2, HEAD_DIM), lambda n, i: (0, 0))],
        out_specs=pl.BlockSpec((tm, tn), lambda n, i: (i, n)),
        out_shape=jax.ShapeDtypeStruct((m, ncols), out_dtype),
        scratch_shapes=[pltpu.VMEM((k, tn), BF16)],
        compiler_params=_params(2),
        name="proj",
    )(h, w3, cos_t, sin_t, gains)


WIN_QB = 2


def _win_kernel(sink_ref, q_ref, kp_ref, kc_ref, kn_ref, kx_ref, vp_ref, vc_ref, vn_ref, vx_ref, o_ref, *, group):
    kvh = pl.program_id(0)
    i = pl.program_id(1)
    nq = WIN_QB * BLK
    nk = (WIN_QB + 2) * BLK
    qi = lax.broadcasted_iota(I32, (group * nq, nk), 0) & (nq - 1)
    kj = lax.broadcasted_iota(I32, (group * nq, nk), 1)
    rel = kj - BLK - qi
    lo = jnp.where(i > 0, 0, BLK)
    hi = jnp.where(i < pl.num_programs(1) - 1, nk, nk - BLK)
    mask = (rel >= -BLK) & (rel <= BLK) & (kj >= lo) & (kj < hi)
    k_win = jnp.concatenate([kp_ref[...], kc_ref[...], kn_ref[...]], axis=0)
    v_win = jnp.concatenate([vp_ref[...], vc_ref[...], vn_ref[...]], axis=0)
    q = jnp.concatenate([q_ref[:, g * HEAD_DIM:(g + 1) * HEAD_DIM] for g in range(group)], axis=0)
    snk = jnp.concatenate([jnp.full((nq, 1), sink_ref[kvh * group + g] * LOG2E, F32) for g in range(group)], axis=0)
    s_loc = jnp.where(mask, _dot_nt(q, k_win), NEG_INF)
    s_ctx = _dot_nt(q, kx_ref[...])
    mx = jnp.maximum(jnp.maximum(jnp.max(s_loc, -1, keepdims=True), jnp.max(s_ctx, -1, keepdims=True)), snk)
    p_loc = jnp.exp2(s_loc - mx)
    p_ctx = jnp.exp2(s_ctx - mx)
    den = jnp.sum(p_loc, -1, keepdims=True) + jnp.sum(p_ctx, -1, keepdims=True) + jnp.exp2(snk - mx)
    o = (_dot(p_loc.astype(BF16), v_win) + _dot(p_ctx.astype(BF16), vx_ref[...])) / den
    for g in range(group):
        o_ref[:, g * HEAD_DIM:(g + 1) * HEAD_DIM] = o[g * nq:(g + 1) * nq, :].astype(o_ref.dtype)


def _window_attn(qkv, sink, n_ctx):
    m = qkv.shape[0]
    nb = (m - n_ctx) // BLK
    cb = n_ctx // BLK
    assert nb % WIN_QB == 0 and cb % WIN_QB == 0
    group = A_HEADS // A_KV
    kcol = A_Q // HEAD_DIM
    vcol = kcol + A_KV

    def edge_spec(col, block):
        return pl.BlockSpec((BLK, HEAD_DIM),
                            lambda h, i, s: (cb + jnp.clip(i * WIN_QB + block, 0, nb - 1), col + h))

    def own_spec(col):
        return pl.BlockSpec((WIN_QB * BLK, HEAD_DIM), lambda h, i, s: (cb // WIN_QB + i, col + h))

    def ctx_spec(col):
        return pl.BlockSpec((n_ctx, HEAD_DIM), lambda h, i, s: (0, col + h))

    grid_spec = pltpu.PrefetchScalarGridSpec(
        num_scalar_prefetch=1,
        grid=(A_KV, nb // WIN_QB),
        in_specs=[pl.BlockSpec((WIN_QB * BLK, group * HEAD_DIM), lambda h, i, s: (cb // WIN_QB + i, h)),
                  edge_spec(kcol, -1), own_spec(kcol), edge_spec(kcol, WIN_QB), ctx_spec(kcol),
                  edge_spec(vcol, -1), own_spec(vcol), edge_spec(vcol, WIN_QB), ctx_spec(vcol)],
        out_specs=pl.BlockSpec((WIN_QB * BLK, group * HEAD_DIM), lambda h, i, s: (i, h)),
    )
    return pl.pallas_call(
        functools.partial(_win_kernel, group=group),
        grid_spec=grid_spec,
        out_shape=jax.ShapeDtypeStruct((m - n_ctx, A_Q), BF16),
        compiler_params=_params(2),
        name="window_attn",
    )(sink, qkv, qkv, qkv, qkv, qkv, qkv, qkv, qkv, qkv)


def _flash_kernel(sink_ref, q_ref, k_ref, v_ref, o_ref, va_ref, m_ref, acc_ref, *, group, tq, tk, n_keys, has_sink):
    kvh = pl.program_id(0)
    rows = group * tq
    ncol = tk // HEAD_DIM

    @pl.when(pl.program_id(1) == 0)
    def _():
        va_ref[:, :HEAD_DIM] = v_ref[...]
        va_ref[:, HEAD_DIM:] = jnp.ones((n_keys, HEAD_DIM), BF16)

    q = jnp.concatenate([q_ref[:, g * HEAD_DIM:(g + 1) * HEAD_DIM] for g in range(group)], axis=0)
    m_ref[...] = jnp.full((rows, HEAD_DIM), NEG_INF, F32)
    acc_ref[...] = jnp.zeros((rows, 2 * HEAD_DIM), F32)

    def body(j, carry):
        off = pl.multiple_of(j * tk, tk)
        s = _dot_nt(q, k_ref[pl.ds(off, tk), :])
        cols = [s[:, c * HEAD_DIM:(c + 1) * HEAD_DIM] for c in range(ncol)]
        m_prev = m_ref[...]
        m_new = jnp.maximum(m_prev, jnp.max(functools.reduce(jnp.maximum, cols), -1, keepdims=True))
        alpha = jnp.exp2(m_prev - m_new)
        p = jnp.concatenate([jnp.exp2(c - m_new).astype(BF16) for c in cols], axis=1)
        acc_ref[...] = jnp.concatenate([alpha, alpha], axis=1) * acc_ref[...] + _dot(p, va_ref[pl.ds(off, tk), :])
        m_ref[...] = m_new
        return carry

    for j in range(n_keys // tk):
        body(j, 0)
    for g in range(group):
        rs = slice(g * tq, (g + 1) * tq)
        num = acc_ref[rs, :HEAD_DIM]
        den = acc_ref[rs, HEAD_DIM:]
        if has_sink:
            m_fin = m_ref[rs, :]
            snk = sink_ref[kvh * group + g] * LOG2E
            m_tot = jnp.maximum(m_fin, snk)
            scale = jnp.exp2(m_fin - m_tot)
            den = den * scale + jnp.exp2(snk - m_tot)
            num = num * scale
        o_ref[:, g * HEAD_DIM:(g + 1) * HEAD_DIM] = (num / den).astype(o_ref.dtype)


def _flash_attn(q_arr, q_col0, kv_arr, k_col0, v_col0, n_kv, group, sink, q_row0, n_q, n_keys, tq, tk):
    assert n_q % tq == 0 and q_row0 % tq == 0 and n_keys % tk == 0
    has_sink = sink is not None
    if sink is None:
        sink = jnp.zeros((n_kv * group,), F32)
    qb0 = q_row0 // tq
    qc0 = q_col0 // (group * HEAD_DIM)
    kc0 = k_col0 // HEAD_DIM
    vc0 = v_col0 // HEAD_DIM
    rows = group * tq
    grid_spec = pltpu.PrefetchScalarGridSpec(
        num_scalar_prefetch=1,
        grid=(n_kv, n_q // tq),
        in_specs=[pl.BlockSpec((tq, group * HEAD_DIM), lambda h, i, s: (qb0 + i, qc0 + h)),
                  pl.BlockSpec((n_keys, HEAD_DIM), lambda h, i, s: (0, kc0 + h)),
                  pl.BlockSpec((n_keys, HEAD_DIM), lambda h, i, s: (0, vc0 + h))],
        out_specs=pl.BlockSpec((tq, group * HEAD_DIM), lambda h, i, s: (i, h)),
        scratch_shapes=[pltpu.VMEM((n_keys, 2 * HEAD_DIM), BF16), pltpu.VMEM((rows, HEAD_DIM), F32),
                        pltpu.VMEM((rows, 2 * HEAD_DIM), F32)],
    )
    return pl.pallas_call(
        functools.partial(_flash_kernel, group=group, tq=tq, tk=tk, n_keys=n_keys, has_sink=has_sink),
        grid_spec=grid_spec,
        out_shape=jax.ShapeDtypeStruct((n_q, n_kv * group * HEAD_DIM), BF16),
        compiler_params=_params(2),
        name="flash_attn",
    )(sink, q_arr, kv_arr, kv_arr)


def _log_sigmoid(x):
    return jnp.minimum(x, 0.0) - jnp.log(1.0 + jnp.exp(-jnp.abs(x)))


def _dot_exact(a, b):
    return jnp.dot(a, b, preferred_element_type=F32, precision=lax.Precision.HIGHEST)


def _mlstm_head(q, k, v, i_col, i_row, b_col, b_row, b_tot, mask, c_ref, n_ref, m_ref):
    m_prev = m_ref[:, 0:1]
    c_prev = c_ref[...]
    n_prev = n_ref[...]
    qs = q * (B_DK ** -0.5)
    qb = qs.astype(BF16)
    dlog = jnp.where(mask, b_col - b_row + i_row, NEG_INF)
    inter = b_col + m_prev
    m_t = jnp.maximum(inter, jnp.max(dlog, -1, keepdims=True))
    dw = jnp.exp(dlog - m_t)
    iw = jnp.exp(inter - m_t)
    sc = _dot_nt(qb, k.astype(BF16)) * dw
    num = _dot(sc.astype(BF16), v.astype(BF16)) + iw * _dot(qb, c_prev.astype(BF16))
    den = jnp.sum(sc, -1, keepdims=True) + iw * jnp.sum(qs * n_prev, -1, keepdims=True)
    h = num / jnp.maximum(jnp.abs(den), jnp.exp(-m_t))
    glog_col = b_tot - b_col + i_col
    glog_row = b_tot - b_row + i_row
    m_new = jnp.maximum(b_tot + m_prev, jnp.max(glog_row, -1, keepdims=True))
    decay = jnp.exp(b_tot + m_prev - m_new)
    wk = jnp.exp(glog_col - m_new) * k
    c_ref[...] = decay * c_prev + lax.dot_general(wk.astype(BF16), v.astype(BF16), (((0,), (0,)), ((), ())),
                                                  preferred_element_type=F32)
    n_ref[...] = decay * n_prev + jnp.sum(wk, 0, keepdims=True)
    m_ref[...] = jnp.broadcast_to(m_new, m_ref.shape)
    return h


def _mlstm_kernel(brow_ref, bcol_ref,
                  qf_ref, kf_ref, vf_ref, gf_ref, gtf_ref,
                  qb_ref, kb_ref, vb_ref, gb_ref, gtb_ref,
                  of_ref, ob_ref, c_ref, n_ref, m_ref):
    L = MLSTM_CHUNK

    @pl.when(pl.program_id(0) == 0)
    def _():
        c_ref[...] = jnp.zeros(c_ref.shape, F32)
        n_ref[...] = jnp.zeros(n_ref.shape, F32)
        m_ref[...] = jnp.zeros(m_ref.shape, F32)

    r = lax.broadcasted_iota(I32, (L, L), 0)
    cidx = lax.broadcasted_iota(I32, (L, L), 1)
    lane = lax.broadcasted_iota(I32, (L, HEAD_DIM), 1)
    sub = lax.broadcasted_iota(I32, (2 * 2 * B_HEADS, L), 0)
    for d, (q_ref, k_ref, v_ref, g_ref, gt_ref, o_ref) in enumerate(
            ((qf_ref, kf_ref, vf_ref, gf_ref, gtf_ref, of_ref), (qb_ref, kb_ref, vb_ref, gb_ref, gtb_ref, ob_ref))):
        mask = (cidx <= r) if d == 0 else (cidx >= r)
        gc = g_ref[...] + brow_ref[...]
        gc = jnp.where(lane < 2 * B_HEADS, gc, _log_sigmoid(gc))
        gr = gt_ref[...] + bcol_ref[...]
        gr = jnp.where(sub < 2 * B_HEADS, gr, _log_sigmoid(gr))
        mask_t = (r <= cidx) if d == 0 else (r >= cidx)
        bc_all = _dot_exact(mask.astype(F32), gc)
        br_all = _dot_exact(gr, mask_t.astype(F32))
        edge = L - 1 if d == 0 else 0
        for hd in range(B_HEADS):
            ci = d * B_HEADS + hd
            cf = 2 * B_HEADS + ci
            b_col = bc_all[:, cf:cf + 1]
            h = _mlstm_head(q_ref[:, hd * B_DK:(hd + 1) * B_DK], k_ref[:, hd * B_DK:(hd + 1) * B_DK],
                            v_ref[:, hd * B_DV:(hd + 1) * B_DV],
                            gc[:, ci:ci + 1], gr[ci:ci + 1, :], b_col, br_all[cf:cf + 1, :],
                            b_col[edge:edge + 1, :], mask, c_ref.at[ci], n_ref.at[ci], m_ref.at[ci])
            o_ref[:, hd * B_DV:(hd + 1) * B_DV] = h


def _mlstm(qkvo, gates, gates_t, bias_row, bias_col, n_ctx):
    m = qkvo.shape[0]
    L = MLSTM_CHUNK
    nc = m // L
    ncc = n_ctx // L
    kq = B_Q // B_Q
    kv = (2 * B_Q) // B_V

    def fw(s):
        return s

    def bw(s):
        return jnp.where(s < ncc, ncc - 1 - s, nc - 1 + ncc - s)

    def specs(order):
        return [pl.BlockSpec((L, B_Q), lambda s: (order(s), 0)),
                pl.BlockSpec((L, B_Q), lambda s: (order(s), kq)),
                pl.BlockSpec((L, B_V), lambda s: (order(s), kv)),
                pl.BlockSpec((L, HEAD_DIM), lambda s: (order(s), 0)),
                pl.BlockSpec((4 * B_HEADS, L), lambda s: (0, order(s)))]

    nst = 2 * B_HEADS
    return pl.pallas_call(
        _mlstm_kernel,
        grid=(nc,),
        in_specs=[pl.BlockSpec((1, HEAD_DIM), lambda s: (0, 0)),
                  pl.BlockSpec((4 * B_HEADS, 1), lambda s: (0, 0))] + specs(fw) + specs(bw),
        out_specs=[pl.BlockSpec((L, B_V), lambda s: (fw(s), 0)),
                   pl.BlockSpec((L, B_V), lambda s: (bw(s), 0))],
        out_shape=[jax.ShapeDtypeStruct((m, B_V), F32), jax.ShapeDtypeStruct((m, B_V), F32)],
        scratch_shapes=[pltpu.VMEM((nst, B_DK, B_DV), F32), pltpu.VMEM((nst, 1, B_DK), F32),
                        pltpu.VMEM((nst, 1, HEAD_DIM), F32)],
        compiler_params=_params(1),
        name="mlstm",
    )(bias_row, bias_col, qkvo, qkvo, qkvo, gates, gates_t, qkvo, qkvo, qkvo, gates, gates_t)


def _bout_kernel(hf_ref, hb_ref, o_ref, g_ref, y_ref):
    for hd in range(B_HEADS):
        sl = slice(hd * B_DV, (hd + 1) * B_DV)
        hn = _rms(hf_ref[:, sl] + hb_ref[:, sl], g_ref[:, sl])
        y_ref[:, sl] = (jax.nn.sigmoid(o_ref[:, sl]) * hn).astype(y_ref.dtype)


def _b_out(hf, hb, qkvo, out_g):
    m = hf.shape[0]
    ocol = (2 * B_Q + B_V) // B_V
    row = pl.BlockSpec((TM, B_V), lambda i: (i, 0))
    return pl.pallas_call(
        _bout_kernel,
        grid=(m // TM,),
        in_specs=[row, row, pl.BlockSpec((TM, B_V), lambda i: (i, ocol)),
                  pl.BlockSpec((1, B_V), lambda i: (0, 0))],
        out_specs=row,
        out_shape=jax.ShapeDtypeStruct((m, B_V), BF16),
        compiler_params=_params(1),
        name="mlstm_out",
    )(hf, hb, qkvo, out_g)


def _ffn_kernel(ge_ref, gb_ref, gn_ref, x_ref, wg_ref, wu_ref, wd_ref, o_ref, wgb_ref, wub_ref, wdb_ref, *, n_sub):
    g = pl.program_id(0)
    j = pl.program_id(1)
    gn = gn_ref[g]

    @pl.when(gn > 0)
    def _():
        wgb_ref[...] = wg_ref[...].astype(BF16)
        wub_ref[...] = wu_ref[...].astype(BF16)
        wdb_ref[...] = wd_ref[...].astype(BF16)

    def rows_step(off, rows, first):
        x = x_ref[pl.ds(off, rows), :]
        a = _dot(x, wgb_ref[...])
        b = _dot(x, wub_ref[...])
        y = _dot((a * jax.nn.sigmoid(a) * b).astype(BF16), wdb_ref[...])
        if first:
            o_ref[pl.ds(off, rows), :] = y
        else:
            o_ref[pl.ds(off, rows), :] += y

    def used_tiles(first):
        def pair(rp, c):
            rows_step(pl.multiple_of(rp * (2 * TM), 2 * TM), 2 * TM, first)
            return c

        lax.fori_loop(0, lax.shift_right_logical(gn, 1), pair, 0)

        @pl.when((gn & 1) == 1)
        def _():
            rows_step(pl.multiple_of((gn - 1) * TM, TM), TM, first)

    @pl.when(j == 0)
    def _():
        used_tiles(True)

        def zero(r, c):
            off = pl.multiple_of(r * TM, TM)
            o_ref[pl.ds(off, TM), :] = jnp.zeros((TM, o_ref.shape[1]), F32)
            return c

        lax.fori_loop(gn, n_sub, zero, 0)

    @pl.when(j > 0)
    def _():
        used_tiles(False)


def _ffn(x, wg, wu, wd, e0, grp_e, grp_b, grp_n, n_sub, tf):
    p, d = x.shape
    f = wg.shape[-1]
    rg = n_sub * TM
    assert f % tf == 0 and p % rg == 0
    once = pl.Buffered(1)
    nj = f // tf

    def slice_of(j, g, gn):
        return jnp.where(gn[g] > 0, j, nj - 1)

    grid_spec = pltpu.PrefetchScalarGridSpec(
        num_scalar_prefetch=3,
        grid=(grp_e.shape[0], nj),
        in_specs=[pl.BlockSpec((rg, d), lambda g, j, ge, gb, gn: (gb[g], 0), pipeline_mode=once),
                  pl.BlockSpec((None, d, tf), lambda g, j, ge, gb, gn: (e0 + ge[g], 0, slice_of(j, g, gn))),
                  pl.BlockSpec((None, d, tf), lambda g, j, ge, gb, gn: (e0 + ge[g], 0, slice_of(j, g, gn))),
                  pl.BlockSpec((None, tf, d), lambda g, j, ge, gb, gn: (e0 + ge[g], slice_of(j, g, gn), 0))],
        out_specs=pl.BlockSpec((rg, d), lambda g, j, ge, gb, gn: (g, 0), pipeline_mode=once),
        scratch_shapes=[pltpu.VMEM((d, tf), BF16), pltpu.VMEM((d, tf), BF16), pltpu.VMEM((tf, d), BF16)],
    )
    return pl.pallas_call(
        functools.partial(_ffn_kernel, n_sub=n_sub),
        grid_spec=grid_spec,
        out_shape=jax.ShapeDtypeStruct((p, d), F32),
        compiler_params=_params(2),
        name="ffn",
    )(grp_e, grp_b, grp_n, x, wg, wu, wd)


def _dense_groups(m):
    n_sub = next(r for r in (11, 3, 2, 1) if m % (r * TM) == 0)
    n_groups = m // (n_sub * TM)
    return (jnp.zeros((n_groups,), I32), jnp.arange(n_groups, dtype=I32), jnp.full((n_groups,), n_sub, I32)), n_sub


def _router_kernel(h_ref, r_ref, idx_ref, w_ref):
    logits = _dot(h_ref[...].astype(BF16), r_ref[...].astype(BF16))
    lane = lax.broadcasted_iota(I32, logits.shape, 1).astype(F32)
    big = float(HEAD_DIM)
    lg = jnp.where(lane < N_EXPERTS, logits, NEG_INF)
    m1 = jnp.max(lg, -1, keepdims=True)
    i1 = jnp.min(jnp.where(lg == m1, lane, big), -1, keepdims=True)
    lg2 = jnp.where(lane == i1, NEG_INF, lg)
    m2 = jnp.max(lg2, -1, keepdims=True)
    i2 = jnp.min(jnp.where(lg2 == m2, lane, big), -1, keepdims=True)
    e2 = jnp.exp(m2 - m1)
    den = 1.0 + e2
    idx_ref[...] = jnp.where(lane == 0.0, i1, jnp.where(lane == 1.0, i2, 0.0)).astype(I32)
    w_ref[...] = jnp.where(lane == 0.0, 1.0 / den, jnp.where(lane == 1.0, e2 / den, 0.0))


def _router(h, router_p):
    m, d = h.shape
    row = pl.BlockSpec((TM, HEAD_DIM), lambda i: (i, 0))
    return pl.pallas_call(
        _router_kernel,
        grid=(m // TM,),
        in_specs=[pl.BlockSpec((TM, d), lambda i: (i, 0)), pl.BlockSpec((d, HEAD_DIM), lambda i: (0, 0))],
        out_specs=[row, row],
        out_shape=[jax.ShapeDtypeStruct((m, HEAD_DIM), I32), jax.ShapeDtypeStruct((m, HEAD_DIM), F32)],
        compiler_params=_params(1),
        name="router",
    )(h, router_p)


def _moe_plan(eidx, ew, n_groups, n_sub):
    m = eidx.shape[0]
    rg = n_sub * TM
    e_flat = eidx.reshape(-1)
    experts = jnp.arange(N_EXPERTS, dtype=I32)
    onehot = (e_flat[:, None] == experts[None, :]).astype(I32)
    csum = jnp.cumsum(onehot, axis=0)
    rank = jnp.sum(csum * onehot, axis=1) - 1
    counts = csum[-1]
    groups_e = (counts + rg - 1) // rg
    grp_end = jnp.cumsum(groups_e)
    grp_start = grp_end - groups_e
    pos = jnp.sum(onehot * grp_start[None, :], axis=1) * rg + rank
    token = jnp.arange(2 * m, dtype=I32) // 2
    src = jnp.zeros((n_groups * rg,), I32).at[pos].set(token)
    gids = jnp.arange(n_groups, dtype=I32)
    valid = gids < grp_end[-1]
    grp_e = jnp.sum((gids[:, None] >= grp_end[None, :]).astype(I32), axis=1)
    grp_e = jnp.where(valid, grp_e, jnp.max(jnp.where(counts > 0, experts, 0)))
    grp_b = jnp.minimum(gids, grp_end[-1] - 1).astype(I32)
    mine = (grp_e[:, None] == experts[None, :]).astype(I32)
    rows_left = jnp.sum(mine * counts[None, :], axis=1) - (gids - jnp.sum(mine * grp_start[None, :], axis=1)) * rg
    grp_n = jnp.where(valid, jnp.clip((rows_left + TM - 1) // TM, 0, n_sub), 0).astype(I32)
    tile_valid = (jnp.arange(n_groups * n_sub, dtype=I32) % n_sub < jnp.repeat(grp_n, n_sub)).astype(I32)
    pos2 = pos.reshape(m, 2).astype(I32)
    return src, (grp_e.astype(I32), grp_b, grp_n), tile_valid, pos2[:, 0], pos2[:, 1]


ROW_DMA_UNROLL = 8


def _gather_kernel(src_ref, tv_ref, h_hbm, o_ref, buf_ref, sem):
    t = pl.program_id(0)
    nt = pl.num_programs(0)

    def row_copy(row, slot, r):
        return pltpu.make_async_copy(h_hbm.at[pl.ds(row, 1), :], buf_ref.at[slot, pl.ds(r, 1), :], sem.at[slot])

    def fetch(tile):
        slot = tile % 2

        @pl.when(tv_ref[tile] == 1)
        def _():
            def issue(r, c):
                row_copy(src_ref[tile * TM + r], slot, r).start()
                return c

            lax.fori_loop(0, TM, issue, 0, unroll=ROW_DMA_UNROLL)

    @pl.when(t == 0)
    def _():
        fetch(t)

    @pl.when(t + 1 < nt)
    def _():
        fetch(t + 1)

    @pl.when(tv_ref[t] == 1)
    def _():
        slot = t % 2

        def wait(r, c):
            row_copy(0, slot, r).wait()
            return c

        lax.fori_loop(0, TM, wait, 0, unroll=ROW_DMA_UNROLL)
        o_ref[...] = buf_ref[slot].astype(o_ref.dtype)

    @pl.when(tv_ref[t] == 0)
    def _():
        o_ref[...] = jnp.zeros(o_ref.shape, o_ref.dtype)


def _gather_rows(h, src, tile_valid):
    d = h.shape[1]
    p = src.shape[0]
    grid_spec = pltpu.PrefetchScalarGridSpec(
        num_scalar_prefetch=2,
        grid=(p // TM,),
        in_specs=[pl.BlockSpec(memory_space=pl.ANY)],
        out_specs=pl.BlockSpec((TM, d), lambda t, s, tv: (t, 0)),
        scratch_shapes=[pltpu.VMEM((2, TM, d), F32), pltpu.SemaphoreType.DMA((2,))],
    )
    return pl.pallas_call(
        _gather_kernel,
        grid_spec=grid_spec,
        out_shape=jax.ShapeDtypeStruct((p, d), BF16),
        compiler_params=_params(1),
        name="moe_gather",
    )(src, tile_valid, h)


def _combine_kernel(p0_ref, p1_ref, ys_hbm, ew_ref, x_ref, gpost_ref, gpre_ref, mpost_ref, mpre_ref, xo_ref, *rest,
                    with_pre, tile0):
    if with_pre:
        h_ref, buf_ref, sem = rest
    else:
        h_ref = None
        buf_ref, sem = rest
    t = pl.program_id(0)
    nt = pl.num_programs(0)

    def row_copy(row, slot, k, r):
        return pltpu.make_async_copy(ys_hbm.at[pl.ds(row, 1), :], buf_ref.at[slot, k, pl.ds(r, 1), :], sem.at[slot])

    def fetch(tile):
        slot = tile % 2
        base = (tile + tile0) * TM

        def issue(r, c):
            row_copy(p0_ref[base + r], slot, 0, r).start()
            row_copy(p1_ref[base + r], slot, 1, r).start()
            return c

        lax.fori_loop(0, TM, issue, 0, unroll=ROW_DMA_UNROLL)

    @pl.when(t == 0)
    def _():
        fetch(t)

    @pl.when(t + 1 < nt)
    def _():
        fetch(t + 1)

    slot = t % 2

    def wait(r, c):
        row_copy(0, slot, 0, r).wait()
        row_copy(0, slot, 1, r).wait()
        return c

    lax.fori_loop(0, TM, wait, 0, unroll=ROW_DMA_UNROLL)
    f = ew_ref[:, 0:1] * buf_ref[slot, 0] + ew_ref[:, 1:2] * buf_ref[slot, 1]
    _post_pre_math(x_ref[...], f, gpost_ref, gpre_ref, mpost_ref, mpre_ref, xo_ref, h_ref, 1)


def _combine_post_pre(x, ys, pos0, pos1, ew, norm_g, mods, layer, n_ctx_tiles, h_dtype, tile0):
    m, d = x.shape
    pre_layer = min(layer + 1, norm_g.shape[0] - 1)
    with_pre = h_dtype is not None
    row_in = pl.BlockSpec((TM, d), lambda i, a, b: (i + tile0, 0))
    row_out = pl.BlockSpec((TM, d), lambda i, a, b: (i, 0))
    m_out = m - tile0 * TM

    def mod_spec(l):
        return pl.BlockSpec((None, None, 6, d), lambda i, a, b: (l, jnp.where(i + tile0 < n_ctx_tiles, 1, 0), 0, 0))

    out_shape = [jax.ShapeDtypeStruct((m_out, d), F32)]
    out_specs = [row_out]
    if with_pre:
        out_shape.append(jax.ShapeDtypeStruct((m_out, d), h_dtype))
        out_specs.append(row_out)
    grid_spec = pltpu.PrefetchScalarGridSpec(
        num_scalar_prefetch=2,
        grid=(m_out // TM,),
        in_specs=[pl.BlockSpec(memory_space=pl.ANY),
                  pl.BlockSpec((TM, HEAD_DIM), lambda i, a, b: (i + tile0, 0)), row_in,
                  pl.BlockSpec((None, 4, d), lambda i, a, b: (layer, 0, 0)),
                  pl.BlockSpec((None, 4, d), lambda i, a, b: (pre_layer, 0, 0)),
                  mod_spec(layer), mod_spec(pre_layer)],
        out_specs=out_specs,
        scratch_shapes=[pltpu.VMEM((2, 2, TM, d), F32), pltpu.SemaphoreType.DMA((2,))],
    )
    res = pl.pallas_call(
        functools.partial(_combine_kernel, with_pre=with_pre, tile0=tile0),
        grid_spec=grid_spec,
        out_shape=out_shape,
        compiler_params=_params(1),
        name="moe_combine",
    )(pos0, pos1, ys, ew, x, norm_g, norm_g, mods, mods)
    return res if with_pre else (res[0], None)


def _rope_tables(n_ctx, s):
    pos = jnp.arange(s)
    row = (pos // GRID_W).astype(F32)
    col = (pos % GRID_W).astype(F32)
    inv = ROPE_THETA ** (-jnp.arange(ROPE_FREQS, dtype=F32) / ROPE_FREQS)
    ar, ac = row[:, None] * inv, col[:, None] * inv
    cos = jnp.concatenate([jnp.cos(ar), jnp.cos(ar), jnp.cos(ac), jnp.cos(ac)], axis=-1)
    sin = jnp.concatenate([-jnp.sin(ar), jnp.sin(ar), -jnp.sin(ac), jnp.sin(ac)], axis=-1)
    cos = jnp.concatenate([jnp.ones((n_ctx, HEAD_DIM), F32), cos], axis=0)
    sin = jnp.concatenate([jnp.zeros((n_ctx, HEAD_DIM), F32), sin], axis=0)
    return cos, sin


def _layer_ab(h, j, ab_w_in, a_sink, b_ig_bias, b_fg_bias, b_norm_g, cos_t, sin_t, n_ctx):
    scale = HEAD_DIM ** -0.5 * LOG2E
    no_g = jnp.ones((2, HEAD_DIM), F32)
    a_kinds = [("rope", 0, scale)] * A_HEADS + [("rope", 0, 1.0)] * A_KV + [PLAIN] * A_KV
    a_qkv = _proj(h, ab_w_in, j, 0, a_kinds, 512, BF16, cos_t, sin_t, no_g)
    b_qkvo = _proj(h, ab_w_in, j, A_QKV, [PLAIN] * (B_QKVO // HEAD_DIM), 512, F32, cos_t, sin_t, no_g)
    ngate = 4 * B_HEADS
    w_gate = jnp.pad(ab_w_in[j, :, A_QKV + B_QKVO:], ((0, 0), (0, HEAD_DIM - ngate)))[None]
    gates = _proj(h, w_gate, 0, 0, [PLAIN], HEAD_DIM, F32, cos_t, sin_t, no_g)

    ya_l = _window_attn(a_qkv, a_sink[j], n_ctx)
    ya_c = _flash_attn(a_qkv, 0, a_qkv, A_Q, A_Q + A_KV * HEAD_DIM, A_KV, A_HEADS // A_KV, a_sink[j],
                       0, n_ctx, n_ctx, n_ctx, n_ctx)
    ya = jnp.concatenate([ya_c, ya_l], axis=0)

    bias = jnp.concatenate([b_ig_bias[j].reshape(-1), b_fg_bias[j].reshape(-1)]).astype(F32)
    bias_row = jnp.pad(bias, (0, HEAD_DIM - ngate))[None, :]
    hf, hb = _mlstm(b_qkvo, gates, jnp.transpose(gates[:, :ngate]), bias_row, bias[:, None], n_ctx)
    yb = _b_out(hf, hb, b_qkvo, b_norm_g[j][None, :])
    return jnp.concatenate([ya, yb], axis=1)


def _layer_c(h, j, c_w_qkv, c_qk_g, cos_t, sin_t, n_ctx, need_ctx):
    scale = HEAD_DIM ** -0.5 * LOG2E
    c_kinds = [("norm_rope", 0, scale)] * C_HEADS + [("norm_rope", 1, 1.0)] * C_KV + [PLAIN] * C_KV
    qkv = _proj(h, c_w_qkv, j, 0, c_kinds, 512, BF16, cos_t, sin_t, c_qk_g[j])
    m = h.shape[0]
    group = C_HEADS // C_KV
    tk = 768 if m % 768 == 0 else TM
    y_l = _flash_attn(qkv, 0, qkv, C_Q, C_Q + C_KVW, C_KV, group, None, n_ctx, m - n_ctx, m, TM, tk)
    if need_ctx:
        y_c = _flash_attn(qkv, 0, qkv, C_Q, C_Q + C_KVW, C_KV, group, None, 0, n_ctx, n_ctx, n_ctx, n_ctx)
    else:
        y_c = jnp.zeros((n_ctx, C_Q), BF16)
    return jnp.concatenate([y_c, y_l], axis=0)


def kernel(x, c, ctx, c_ctx, ada_w, ada_b, norm_g, ab_w_in, ab_w_out, a_sink, b_ig_bias, b_fg_bias, b_norm_g,
           c_w_qkv, c_w_out, c_qk_g, ffn_w_gate, ffn_w_up, ffn_w_down, moe_router, moe_w_gate, moe_w_up,
           moe_w_down):
    depth = ada_w.shape[0]
    s, d = x.shape[1], x.shape[2]
    n_ctx = ctx.shape[1]
    m = n_ctx + s
    assert x.shape[0] == 1 and n_ctx % TM == 0 and s % TM == 0
    n_ctx_tiles = n_ctx // TM
    n_tiles = m // TM
    no_g = jnp.ones((2, HEAD_DIM), F32)

    mods = _ada_mods(c, c_ctx, ada_w, ada_b)
    cos_t, sin_t = _rope_tables(n_ctx, s)
    xs = jnp.concatenate([ctx[0], x[0]], axis=0)
    h = _prenorm(xs, norm_g, mods, 0, n_ctx_tiles)
    dense_groups, dense_sub = _dense_groups(m)
    n_exp = moe_w_gate.shape[1]
    moe_g = moe_w_gate.reshape((-1,) + moe_w_gate.shape[2:])
    moe_u = moe_w_up.reshape((-1,) + moe_w_up.shape[2:])
    moe_d = moe_w_down.reshape((-1,) + moe_w_down.shape[2:])

    for layer in range(depth):
        last = layer == depth - 1
        j = layer // 2
        even = layer % 2 == 0
        if even:
            mix = _layer_ab(h, j, ab_w_in, a_sink, b_ig_bias, b_fg_bias, b_norm_g, cos_t, sin_t, n_ctx)
            w_out = ab_w_out
        else:
            mix = _layer_c(h, j, c_w_qkv, c_qk_g, cos_t, sin_t, n_ctx, not last)
            w_out = c_w_out
        y = _proj(mix, w_out, j, 0, [PLAIN] * (d // HEAD_DIM), 512 if d % 512 == 0 else d, F32, cos_t, sin_t, no_g)
        xs, h = _post_pre(xs, y, norm_g, mods, layer, 0, n_ctx_tiles, BF16 if even else F32)
        if even:
            f_dim = ffn_w_gate.shape[-1]
            f = _ffn(h, ffn_w_gate, ffn_w_up, ffn_w_down, j, *dense_groups, dense_sub, FFN_SLICE)
            xs, h = _post_pre(xs, f, norm_g, mods, layer, 1, n_ctx_tiles, None if last else BF16)
        else:
            router_p = jnp.pad(moe_router[j], ((0, 0), (0, HEAD_DIM - n_exp)))
            eidx, ew = _router(h, router_p)
            n_sub = MOE_GROUP_TILES
            n_groups = (2 * m + n_exp * (n_sub * TM - 1)) // (n_sub * TM)
            src, groups, tile_valid, pos0, pos1 = _moe_plan(eidx[:, :2], ew[:, :2], n_groups, n_sub)
            x_s = _gather_rows(h, src, tile_valid)
            y_s = _ffn(x_s, moe_g, moe_u, moe_d, j * n_exp, *groups, n_sub, FFN_SLICE)
            xs, h = _combine_post_pre(xs, y_s, pos0, pos1, ew, norm_g, mods, layer, n_ctx_tiles,
                                      None if last else BF16, n_ctx_tiles if last else 0)
    return (xs if xs.shape[0] == s else xs[n_ctx:])[None]
```

```python
import functools

import jax
import jax.numpy as jnp
import numpy as np
from jax import lax
from jax.experimental import pallas as pl
from jax.experimental.pallas import tpu as pltpu

F32 = jnp.float32
BF16 = jnp.bfloat16
I32 = jnp.int32

EPS = 1e-6
GRID_W = 64
HEAD_DIM = 128
ROPE_THETA = 10000.0
ROPE_FREQS = HEAD_DIM // 4
BLK = 128
A_HEADS = 8
A_KV = 2
B_HEADS = 4
B_DK = 128
B_DV = 256
MLSTM_CHUNK = 128
C_HEADS = 16
C_KV = 4
N_EXPERTS = 8

A_Q = A_HEADS * HEAD_DIM
A_QKV = A_Q + 2 * A_KV * HEAD_DIM
B_Q = B_HEADS * B_DK
B_V = B_HEADS * B_DV
B_QKVO = 2 * B_Q + 2 * B_V
C_Q = C_HEADS * HEAD_DIM
C_KVW = C_KV * HEAD_DIM

TM = 256
MOE_GROUP_TILES = 9
PROJ_BLOCKS = ((1408, 352), (768, 384))
FFN_SLICE = 256
VMEM_LIMIT = 56 * 1024 * 1024
NEG_INF = float("-inf")
LOG2E = 1.4426950408889634


def _row_block(m):
    return next(r * TM for r in (3, 2, 1) if m % (r * TM) == 0)


def _params(n_axes, vmem=VMEM_LIMIT):
    return pltpu.CompilerParams(dimension_semantics=("arbitrary",) * n_axes, vmem_limit_bytes=vmem)


def _rms(x, g):
    return x * lax.rsqrt(jnp.mean(x * x, axis=-1, keepdims=True) + EPS) * g


def _dot(a, b):
    return jnp.dot(a, b, preferred_element_type=F32)


def _dot_nt(a, b):
    return lax.dot_general(a, b, (((1,), (1,)), ((), ())), preferred_element_type=F32)


def _ada_kernel(s_ref, w_ref, b_ref, o_ref):
    s = s_ref[...]
    s = s * jax.nn.sigmoid(s)
    o_ref[...] = _dot(s.astype(BF16), w_ref[...].astype(BF16)) + b_ref[...]


def _ada_mods(c, c_ctx, ada_w, ada_b):
    depth, d, d6 = ada_w.shape
    tn = 1536 if d6 % 1536 == 0 else d6
    s = jnp.zeros((8, d), F32).at[0].set(c[0]).at[1].set(c_ctx)
    out = pl.pallas_call(
        _ada_kernel,
        grid=(depth, d6 // tn),
        in_specs=[pl.BlockSpec((8, d), lambda l, n: (0, 0)),
                  pl.BlockSpec((None, d, tn), lambda l, n: (l, 0, n)),
                  pl.BlockSpec((None, 1, tn), lambda l, n: (l, 0, n))],
        out_specs=pl.BlockSpec((None, 8, tn), lambda l, n: (l, 0, n)),
        out_shape=jax.ShapeDtypeStruct((depth, 8, d6), F32),
        compiler_params=_params(2),
        name="ada_mods",
    )(s, ada_w, ada_b.reshape(depth, 1, d6))
    return out[:, :2].reshape(depth, 2, 6, d)


def _mod_spec(d, layer, n_ctx_tiles):
    return pl.BlockSpec((None, None, 6, d), lambda i: (layer, jnp.where(i < n_ctx_tiles, 1, 0), 0, 0))


def _prenorm_kernel(x_ref, g_ref, mod_ref, h_ref):
    hn = _rms(x_ref[...], g_ref[0:1, :])
    h_ref[...] = (hn * (1.0 + mod_ref[1:2, :]) + mod_ref[0:1, :]).astype(h_ref.dtype)


def _prenorm(x, norm_g, mods, layer, n_ctx_tiles):
    m, d = x.shape
    return pl.pallas_call(
        _prenorm_kernel,
        grid=(m // TM,),
        in_specs=[pl.BlockSpec((TM, d), lambda i: (i, 0)),
                  pl.BlockSpec((None, 4, d), lambda i: (layer, 0, 0)),
                  _mod_spec(d, layer, n_ctx_tiles)],
        out_specs=pl.BlockSpec((TM, d), lambda i: (i, 0)),
        out_shape=jax.ShapeDtypeStruct((m, d), BF16),
        compiler_params=_params(1),
        name="prenorm",
    )(x, norm_g, mods)


def _post_pre_math(x, y, gpost_ref, gpre_ref, mpost_ref, mpre_ref, xo_ref, h_ref, sub):
    g_row = 1 + 2 * sub
    gate_row = 2 + 3 * sub
    xn = x + mpost_ref[gate_row:gate_row + 1, :] * _rms(y, gpost_ref[g_row:g_row + 1, :])
    xo_ref[...] = xn
    if h_ref is not None:
        nsub = 1 - sub
        hn = _rms(xn, gpre_ref[2 * nsub:2 * nsub + 1, :])
        h_ref[...] = (hn * (1.0 + mpre_ref[3 * nsub + 1:3 * nsub + 2, :])
                      + mpre_ref[3 * nsub:3 * nsub + 1, :]).astype(h_ref.dtype)


def _post_pre_kernel(x_ref, y_ref, gpost_ref, gpre_ref, mpost_ref, mpre_ref, xo_ref, *h_ref, sub):
    _post_pre_math(x_ref[...], y_ref[...], gpost_ref, gpre_ref, mpost_ref, mpre_ref, xo_ref,
                   h_ref[0] if h_ref else None, sub)


def _post_pre(x, y, norm_g, mods, layer, sub, n_ctx_tiles, h_dtype):
    m, d = x.shape
    pre_layer = layer if sub == 0 else min(layer + 1, norm_g.shape[0] - 1)
    row = pl.BlockSpec((TM, d), lambda i: (i, 0))
    out_shape = [jax.ShapeDtypeStruct((m, d), F32)]
    out_specs = [row]
    if h_dtype is not None:
        out_shape.append(jax.ShapeDtypeStruct((m, d), h_dtype))
        out_specs.append(row)
    res = pl.pallas_call(
        functools.partial(_post_pre_kernel, sub=sub),
        grid=(m // TM,),
        in_specs=[row, row,
                  pl.BlockSpec((None, 4, d), lambda i: (layer, 0, 0)),
                  pl.BlockSpec((None, 4, d), lambda i: (pre_layer, 0, 0)),
                  _mod_spec(d, layer, n_ctx_tiles),
                  _mod_spec(d, pre_layer, n_ctx_tiles)],
        out_specs=out_specs,
        out_shape=out_shape,
        compiler_params=_params(1),
        name="post_pre",
    )(x, y, norm_g, norm_g, mods, mods)
    return res if h_dtype is not None else (res[0], None)


def _rope(y, cos, sin):
    lane = lax.broadcasted_iota(I32, y.shape, 1)
    lower = (lane & (2 * ROPE_FREQS - 1)) < ROPE_FREQS
    partner = jnp.where(lower, pltpu.roll(y, HEAD_DIM - ROPE_FREQS, 1), pltpu.roll(y, ROPE_FREQS, 1))
    return y * cos + partner * sin


PLAIN = ("plain", 0, 1.0)


def _proj_kernel(*refs, tile_kinds, rows, k_split):
    xs_refs = refs[:len(k_split)]
    w_ref, cos_ref, sin_ref, g_ref, o_ref, wb_ref = refs[len(k_split):]
    n = pl.program_id(1)

    @pl.when(pl.program_id(0) == 0)
    def _():
        wb_ref[n] = w_ref[...].astype(BF16)

    def matmul(rs):
        k0, y = 0, None
        for x_ref, kk in zip(xs_refs, k_split):
            part = _dot(x_ref[rs, :], wb_ref[n, k0:k0 + kk, :])
            y = part if y is None else y + part
            k0 += kk
        return y

    def run(kinds):
        for c in range(o_ref.shape[0] // rows):
            rs = slice(c * rows, (c + 1) * rows)
            y = matmul(rs)
            if all(kd == PLAIN for kd in kinds):
                o_ref[rs, :] = y.astype(o_ref.dtype)
                continue
            for hh, (kind, grow, scale) in enumerate(kinds):
                sl = slice(hh * HEAD_DIM, (hh + 1) * HEAD_DIM)
                yh = y[:, sl]
                if kind == "norm_rope":
                    yh = _rms(yh, g_ref[grow:grow + 1, :])
                if kind != "plain":
                    yh = _rope(yh, cos_ref[rs, :], sin_ref[rs, :])
                if scale != 1.0:
                    yh = yh * scale
                o_ref[rs, sl] = yh.astype(o_ref.dtype)

    t0 = 0
    while t0 < len(tile_kinds):
        t1 = t0 + 1
        while t1 < len(tile_kinds) and tile_kinds[t1] == tile_kinds[t0]:
            t1 += 1
        pl.when((n >= t0) & (n < t1))(functools.partial(run, tile_kinds[t0]))
        t0 = t1


def _proj(hs, w3, layer_idx, col0, head_kinds, tn, out_dtype, cos_t, sin_t, gains):
    m = hs[0].shape[0]
    k_split = tuple(h.shape[1] for h in hs)
    k = sum(k_split)
    ncols = HEAD_DIM * len(head_kinds)
    assert col0 % tn == 0 and ncols % tn == 0 and w3.shape[1] == k
    n0 = col0 // tn
    nt = ncols // tn
    hpt = tn // HEAD_DIM
    tile_kinds = tuple(tuple(head_kinds[t * hpt:(t + 1) * hpt]) for t in range(nt))
    tm, rows = next(((a, b) for a, b in PROJ_BLOCKS if m % a == 0), (TM, TM))

    def w_tile(i, n):
        return (layer_idx, 0, n0 + jnp.where(i == 0, n, nt - 1))

    return pl.pallas_call(
        functools.partial(_proj_kernel, tile_kinds=tile_kinds, rows=rows, k_split=k_split),
        grid=(m // tm, nt),
        in_specs=[pl.BlockSpec((tm, kk), lambda i, n: (i, 0)) for kk in k_split]
                 + [pl.BlockSpec((None, k, tn), w_tile),
                    pl.BlockSpec((tm, HEAD_DIM), lambda i, n: (i, 0)),
                    pl.BlockSpec((tm, HEAD_DIM), lambda i, n: (i, 0)),
                    pl.BlockSpec((2, HEAD_DIM), lambda i, n: (0, 0))],
        out_specs=pl.BlockSpec((tm, tn), lambda i, n: (i, n)),
        out_shape=jax.ShapeDtypeStruct((m, ncols), out_dtype),
        scratch_shapes=[pltpu.VMEM((nt, k, tn), BF16)],
        compiler_params=_params(2),
        name="proj",
    )(*hs, w3, cos_t, sin_t, gains)


WIN_QB = 2


def _win_kernel(sink_ref, q_ref, kp_ref, kc_ref, kn_ref, kx_ref, vp_ref, vc_ref, vn_ref, vx_ref, o_ref, *, group,
                ctx_steps):
    kvh = pl.program_id(0)
    i = pl.program_id(1)
    nq = WIN_QB * BLK
    nk = (WIN_QB + 2) * BLK
    qi = lax.broadcasted_iota(I32, (group * nq, nk), 0) & (nq - 1)
    kj = lax.broadcasted_iota(I32, (group * nq, nk), 1)
    rel = kj - BLK - qi
    lo = jnp.where(i < ctx_steps, nk, jnp.where(i > ctx_steps, 0, BLK))
    hi = jnp.where(i < ctx_steps, 0, jnp.where(i < pl.num_programs(1) - 1, nk, nk - BLK))
    mask = (rel >= -BLK) & (rel <= BLK) & (kj >= lo) & (kj < hi)
    k_win = jnp.concatenate([kp_ref[...], kc_ref[...], kn_ref[...]], axis=0)
    v_win = jnp.concatenate([vp_ref[...], vc_ref[...], vn_ref[...]], axis=0)
    q = jnp.concatenate([q_ref[:, g * HEAD_DIM:(g + 1) * HEAD_DIM] for g in range(group)], axis=0)
    snk = jnp.concatenate([jnp.full((nq, 1), sink_ref[kvh * group + g] * LOG2E, F32) for g in range(group)], axis=0)
    s_loc = jnp.where(mask, _dot_nt(q, k_win), NEG_INF)
    s_ctx = _dot_nt(q, kx_ref[...])
    mx = jnp.maximum(jnp.maximum(jnp.max(s_loc, -1, keepdims=True), jnp.max(s_ctx, -1, keepdims=True)), snk)
    p_loc = jnp.exp2(s_loc - mx)
    p_ctx = jnp.exp2(s_ctx - mx)
    den = jnp.sum(p_loc, -1, keepdims=True) + jnp.sum(p_ctx, -1, keepdims=True) + jnp.exp2(snk - mx)
    o = (_dot(p_loc.astype(BF16), v_win) + _dot(p_ctx.astype(BF16), vx_ref[...])) / den
    for g in range(group):
        o_ref[:, g * HEAD_DIM:(g + 1) * HEAD_DIM] = o[g * nq:(g + 1) * nq, :].astype(o_ref.dtype)


def _window_attn(qkv, sink, n_ctx):
    m = qkv.shape[0]
    nb = (m - n_ctx) // BLK
    cb = n_ctx // BLK
    assert nb % WIN_QB == 0 and cb % WIN_QB == 0
    cq = cb // WIN_QB
    group = A_HEADS // A_KV
    kcol = A_Q // HEAD_DIM
    vcol = kcol + A_KV

    def edge_spec(col, block):
        return pl.BlockSpec((BLK, HEAD_DIM),
                            lambda h, i, s: (cb + jnp.clip((i - cq) * WIN_QB + block, 0, nb - 1), col + h))

    def own_spec(col):
        return pl.BlockSpec((WIN_QB * BLK, HEAD_DIM), lambda h, i, s: (i, col + h))

    def ctx_spec(col):
        return pl.BlockSpec((n_ctx, HEAD_DIM), lambda h, i, s: (0, col + h))

    grid_spec = pltpu.PrefetchScalarGridSpec(
        num_scalar_prefetch=1,
        grid=(A_KV, cq + nb // WIN_QB),
        in_specs=[pl.BlockSpec((WIN_QB * BLK, group * HEAD_DIM), lambda h, i, s: (i, h)),
                  edge_spec(kcol, -1), own_spec(kcol), edge_spec(kcol, WIN_QB), ctx_spec(kcol),
                  edge_spec(vcol, -1), own_spec(vcol), edge_spec(vcol, WIN_QB), ctx_spec(vcol)],
        out_specs=pl.BlockSpec((WIN_QB * BLK, group * HEAD_DIM), lambda h, i, s: (i, h)),
    )
    return pl.pallas_call(
        functools.partial(_win_kernel, group=group, ctx_steps=cq),
        grid_spec=grid_spec,
        out_shape=jax.ShapeDtypeStruct((m, A_Q), BF16),
        compiler_params=_params(2),
        name="window_attn",
    )(sink, qkv, qkv, qkv, qkv, qkv, qkv, qkv, qkv, qkv)


def _flash_kernel(q_ref, k_ref, v_ref, o_ref, va_ref, m_ref, acc_ref, *, group, tq, tk, n_ctx):
    rows = group * tq
    n_keys = k_ref.shape[0]

    @pl.when(pl.program_id(1) == 0)
    def _():
        va_ref[:, :HEAD_DIM] = v_ref[...]
        va_ref[:, HEAD_DIM:] = jnp.ones((n_keys, HEAD_DIM), BF16)

    def attend(limit, step):
        q = jnp.concatenate([q_ref[:, g * HEAD_DIM:(g + 1) * HEAD_DIM] for g in range(group)], axis=0)
        m_ref[...] = jnp.full((rows, HEAD_DIM), NEG_INF, F32)
        acc_ref[...] = jnp.zeros((rows, 2 * HEAD_DIM), F32)
        for j in range(limit // step):
            ks = slice(j * step, (j + 1) * step)
            s = _dot_nt(q, k_ref[ks, :])
            cols = [s[:, c * HEAD_DIM:(c + 1) * HEAD_DIM] for c in range(step // HEAD_DIM)]
            m_prev = m_ref[...]
            m_new = jnp.maximum(m_prev, jnp.max(functools.reduce(jnp.maximum, cols), -1, keepdims=True))
            alpha = jnp.exp2(m_prev - m_new)
            p = jnp.concatenate([jnp.exp2(c - m_new).astype(BF16) for c in cols], axis=1)
            acc_ref[...] = jnp.concatenate([alpha, alpha], axis=1) * acc_ref[...] + _dot(p, va_ref[ks, :])
            m_ref[...] = m_new
        for g in range(group):
            rs = slice(g * tq, (g + 1) * tq)
            o_ref[:, g * HEAD_DIM:(g + 1) * HEAD_DIM] = (acc_ref[rs, :HEAD_DIM] / acc_ref[rs, HEAD_DIM:]).astype(o_ref.dtype)

    ctx_tiles = n_ctx // tq
    pl.when(pl.program_id(1) < ctx_tiles)(functools.partial(attend, n_ctx, n_ctx))
    pl.when(pl.program_id(1) >= ctx_tiles)(functools.partial(attend, n_keys, tk))


def _flash_attn(qkv, q_col0, k_col0, v_col0, n_kv, group, n_ctx, tq, tk):
    m = qkv.shape[0]
    assert m % tq == 0 and n_ctx % tq == 0 and m % tk == 0
    qc0 = q_col0 // (group * HEAD_DIM)
    kc0 = k_col0 // HEAD_DIM
    vc0 = v_col0 // HEAD_DIM
    rows = group * tq
    return pl.pallas_call(
        functools.partial(_flash_kernel, group=group, tq=tq, tk=tk, n_ctx=n_ctx),
        grid=(n_kv, m // tq),
        in_specs=[pl.BlockSpec((tq, group * HEAD_DIM), lambda h, i: (i, qc0 + h)),
                  pl.BlockSpec((m, HEAD_DIM), lambda h, i: (0, kc0 + h)),
                  pl.BlockSpec((m, HEAD_DIM), lambda h, i: (0, vc0 + h))],
        out_specs=pl.BlockSpec((tq, group * HEAD_DIM), lambda h, i: (i, h)),
        out_shape=jax.ShapeDtypeStruct((m, n_kv * group * HEAD_DIM), BF16),
        scratch_shapes=[pltpu.VMEM((m, 2 * HEAD_DIM), BF16), pltpu.VMEM((rows, HEAD_DIM), F32),
                        pltpu.VMEM((rows, 2 * HEAD_DIM), F32)],
        compiler_params=_params(2),
        name="flash_attn",
    )(qkv, qkv, qkv)


def _log_sigmoid(x):
    return jnp.minimum(x, 0.0) - jnp.log(1.0 + jnp.exp(-jnp.abs(x)))


def _dot_exact(a, b):
    return jnp.dot(a, b, preferred_element_type=F32, precision=lax.Precision.HIGHEST)


def _mlstm_head(q, k, v, i_col, i_row, b_col, b_row, b_tot, mask, c_ref, n_ref, m_ref):
    m_prev = m_ref[:, 0:1]
    c_prev = c_ref[...]
    n_prev = n_ref[...]
    qs = q * (B_DK ** -0.5)
    qb = qs.astype(BF16)
    dlog = jnp.where(mask, b_col - b_row + i_row, NEG_INF)
    inter = b_col + m_prev
    m_t = jnp.maximum(inter, jnp.max(dlog, -1, keepdims=True))
    dw = jnp.exp(dlog - m_t)
    iw = jnp.exp(inter - m_t)
    sc = _dot_nt(qb, k.astype(BF16)) * dw
    num = _dot(sc.astype(BF16), v.astype(BF16)) + iw * _dot(qb, c_prev.astype(BF16))
    den = jnp.sum(sc, -1, keepdims=True) + iw * jnp.sum(qs * n_prev, -1, keepdims=True)
    h = num / jnp.maximum(jnp.abs(den), jnp.exp(-m_t))
    glog_col = b_tot - b_col + i_col
    glog_row = b_tot - b_row + i_row
    m_new = jnp.maximum(b_tot + m_prev, jnp.max(glog_row, -1, keepdims=True))
    decay = jnp.exp(b_tot + m_prev - m_new)
    wk = jnp.exp(glog_col - m_new) * k
    c_ref[...] = decay * c_prev + lax.dot_general(wk.astype(BF16), v.astype(BF16), (((0,), (0,)), ((), ())),
                                                  preferred_element_type=F32)
    n_ref[...] = decay * n_prev + jnp.sum(wk, 0, keepdims=True)
    m_ref[...] = jnp.broadcast_to(m_new, m_ref.shape)
    return h


def _mlstm_kernel(brow_ref, bcol_ref,
                  qf_ref, kf_ref, vf_ref, gf_ref, gtf_ref,
                  qb_ref, kb_ref, vb_ref, gb_ref, gtb_ref,
                  of_ref, ob_ref, c_ref, n_ref, m_ref):
    L = MLSTM_CHUNK

    @pl.when(pl.program_id(0) == 0)
    def _():
        c_ref[...] = jnp.zeros(c_ref.shape, F32)
        n_ref[...] = jnp.zeros(n_ref.shape, F32)
        m_ref[...] = jnp.zeros(m_ref.shape, F32)

    r = lax.broadcasted_iota(I32, (L, L), 0)
    cidx = lax.broadcasted_iota(I32, (L, L), 1)
    lane = lax.broadcasted_iota(I32, (L, HEAD_DIM), 1)
    sub = lax.broadcasted_iota(I32, (2 * 2 * B_HEADS, L), 0)
    for d, (q_ref, k_ref, v_ref, g_ref, gt_ref, o_ref) in enumerate(
            ((qf_ref, kf_ref, vf_ref, gf_ref, gtf_ref, of_ref), (qb_ref, kb_ref, vb_ref, gb_ref, gtb_ref, ob_ref))):
        mask = (cidx <= r) if d == 0 else (cidx >= r)
        gc = g_ref[...] + brow_ref[...]
        gc = jnp.where(lane < 2 * B_HEADS, gc, _log_sigmoid(gc))
        gr = gt_ref[...] + bcol_ref[...]
        gr = jnp.where(sub < 2 * B_HEADS, gr, _log_sigmoid(gr))
        mask_t = (r <= cidx) if d == 0 else (r >= cidx)
        bc_all = _dot_exact(mask.astype(F32), gc)
        br_all = _dot_exact(gr, mask_t.astype(F32))
        edge = L - 1 if d == 0 else 0
        for hd in range(B_HEADS):
            ci = d * B_HEADS + hd
            cf = 2 * B_HEADS + ci
            b_col = bc_all[:, cf:cf + 1]
            h = _mlstm_head(q_ref[:, hd * B_DK:(hd + 1) * B_DK], k_ref[:, hd * B_DK:(hd + 1) * B_DK],
                            v_ref[:, hd * B_DV:(hd + 1) * B_DV],
                            gc[:, ci:ci + 1], gr[ci:ci + 1, :], b_col, br_all[cf:cf + 1, :],
                            b_col[edge:edge + 1, :], mask, c_ref.at[ci], n_ref.at[ci], m_ref.at[ci])
            o_ref[:, hd * B_DV:(hd + 1) * B_DV] = h


def _mlstm(qkvo, gates, gates_t, bias_row, bias_col, n_ctx):
    m = qkvo.shape[0]
    L = MLSTM_CHUNK
    nc = m // L
    ncc = n_ctx // L
    kq = B_Q // B_Q
    kv = (2 * B_Q) // B_V

    def fw(s):
        return s

    def bw(s):
        return jnp.where(s < ncc, ncc - 1 - s, nc - 1 + ncc - s)

    def specs(order):
        return [pl.BlockSpec((L, B_Q), lambda s: (order(s), 0)),
                pl.BlockSpec((L, B_Q), lambda s: (order(s), kq)),
                pl.BlockSpec((L, B_V), lambda s: (order(s), kv)),
                pl.BlockSpec((L, HEAD_DIM), lambda s: (order(s), 0)),
                pl.BlockSpec((4 * B_HEADS, L), lambda s: (0, order(s)))]

    nst = 2 * B_HEADS
    return pl.pallas_call(
        _mlstm_kernel,
        grid=(nc,),
        in_specs=[pl.BlockSpec((1, HEAD_DIM), lambda s: (0, 0)),
                  pl.BlockSpec((4 * B_HEADS, 1), lambda s: (0, 0))] + specs(fw) + specs(bw),
        out_specs=[pl.BlockSpec((L, B_V), lambda s: (fw(s), 0)),
                   pl.BlockSpec((L, B_V), lambda s: (bw(s), 0))],
        out_shape=[jax.ShapeDtypeStruct((m, B_V), F32), jax.ShapeDtypeStruct((m, B_V), F32)],
        scratch_shapes=[pltpu.VMEM((nst, B_DK, B_DV), F32), pltpu.VMEM((nst, 1, B_DK), F32),
                        pltpu.VMEM((nst, 1, HEAD_DIM), F32)],
        compiler_params=_params(1),
        name="mlstm",
    )(bias_row, bias_col, qkvo, qkvo, qkvo, gates, gates_t, qkvo, qkvo, qkvo, gates, gates_t)


def _bout_kernel(hf_ref, hb_ref, o_ref, g_ref, y_ref):
    for hd in range(B_HEADS):
        sl = slice(hd * B_DV, (hd + 1) * B_DV)
        hn = _rms(hf_ref[:, sl] + hb_ref[:, sl], g_ref[:, sl])
        y_ref[:, sl] = (jax.nn.sigmoid(o_ref[:, sl]) * hn).astype(y_ref.dtype)


def _b_out(hf, hb, qkvo, out_g):
    m = hf.shape[0]
    ocol = (2 * B_Q + B_V) // B_V
    row = pl.BlockSpec((TM, B_V), lambda i: (i, 0))
    return pl.pallas_call(
        _bout_kernel,
        grid=(m // TM,),
        in_specs=[row, row, pl.BlockSpec((TM, B_V), lambda i: (i, ocol)),
                  pl.BlockSpec((1, B_V), lambda i: (0, 0))],
        out_specs=row,
        out_shape=jax.ShapeDtypeStruct((m, B_V), BF16),
        compiler_params=_params(1),
        name="mlstm_out",
    )(hf, hb, qkvo, out_g)


def _ffn_kernel(ge_ref, gb_ref, gn_ref, x_ref, wg_ref, wu_ref, wd_ref, o_ref, wgb_ref, wub_ref, wdb_ref, *, n_sub):
    g = pl.program_id(0)
    j = pl.program_id(1)
    gn = gn_ref[g]

    @pl.when(gn > 0)
    def _():
        wgb_ref[...] = wg_ref[...].astype(BF16)
        wub_ref[...] = wu_ref[...].astype(BF16)
        wdb_ref[...] = wd_ref[...].astype(BF16)

    def rows_step(off, rows, first):
        x = x_ref[pl.ds(off, rows), :]
        a = _dot(x, wgb_ref[...])
        b = _dot(x, wub_ref[...])
        y = _dot((a * jax.nn.sigmoid(a) * b).astype(BF16), wdb_ref[...])
        if first:
            o_ref[pl.ds(off, rows), :] = y
        else:
            o_ref[pl.ds(off, rows), :] += y

    def used_tiles(first):
        def quad(rq, c):
            rows_step(pl.multiple_of(rq * (4 * TM), 4 * TM), 4 * TM, first)
            return c

        lax.fori_loop(0, lax.shift_right_logical(gn, 2), quad, 0)

        @pl.when((gn & 2) == 2)
        def _():
            rows_step(pl.multiple_of((gn & ~3) * TM, 2 * TM), 2 * TM, first)

        @pl.when((gn & 1) == 1)
        def _():
            rows_step(pl.multiple_of((gn - 1) * TM, TM), TM, first)

    @pl.when(j == 0)
    def _():
        used_tiles(True)

        def zero(r, c):
            off = pl.multiple_of(r * TM, TM)
            o_ref[pl.ds(off, TM), :] = jnp.zeros((TM, o_ref.shape[1]), F32)
            return c

        lax.fori_loop(gn, n_sub, zero, 0)

    @pl.when(j > 0)
    def _():
        used_tiles(False)


def _ffn(x, wg, wu, wd, e0, grp_e, grp_b, grp_n, n_sub, tf):
    p, d = x.shape
    f = wg.shape[-1]
    rg = n_sub * TM
    assert f % tf == 0 and p % rg == 0
    once = pl.Buffered(1)
    nj = f // tf

    def slice_of(j, g, gn):
        return jnp.where(gn[g] > 0, j, nj - 1)

    grid_spec = pltpu.PrefetchScalarGridSpec(
        num_scalar_prefetch=3,
        grid=(grp_e.shape[0], nj),
        in_specs=[pl.BlockSpec((rg, d), lambda g, j, ge, gb, gn: (gb[g], 0), pipeline_mode=once),
                  pl.BlockSpec((None, d, tf), lambda g, j, ge, gb, gn: (e0 + ge[g], 0, slice_of(j, g, gn))),
                  pl.BlockSpec((None, d, tf), lambda g, j, ge, gb, gn: (e0 + ge[g], 0, slice_of(j, g, gn))),
                  pl.BlockSpec((None, tf, d), lambda g, j, ge, gb, gn: (e0 + ge[g], slice_of(j, g, gn), 0))],
        out_specs=pl.BlockSpec((rg, d), lambda g, j, ge, gb, gn: (g, 0), pipeline_mode=once),
        scratch_shapes=[pltpu.VMEM((d, tf), BF16), pltpu.VMEM((d, tf), BF16), pltpu.VMEM((tf, d), BF16)],
    )
    return pl.pallas_call(
        functools.partial(_ffn_kernel, n_sub=n_sub),
        grid_spec=grid_spec,
        out_shape=jax.ShapeDtypeStruct((p, d), F32),
        compiler_params=_params(2),
        name="ffn",
    )(grp_e, grp_b, grp_n, x, wg, wu, wd)


def _dense_groups(m):
    n_sub = next(r for r in (11, 3, 2, 1) if m % (r * TM) == 0)
    n_groups = m // (n_sub * TM)
    return (jnp.zeros((n_groups,), I32), jnp.arange(n_groups, dtype=I32), jnp.full((n_groups,), n_sub, I32)), n_sub


def _router_kernel(h_ref, r_ref, idx_ref, w_ref):
    logits = _dot(h_ref[...].astype(BF16), r_ref[...].astype(BF16))
    lane = lax.broadcasted_iota(I32, logits.shape, 1).astype(F32)
    big = float(HEAD_DIM)
    lg = jnp.where(lane < N_EXPERTS, logits, NEG_INF)
    m1 = jnp.max(lg, -1, keepdims=True)
    i1 = jnp.min(jnp.where(lg == m1, lane, big), -1, keepdims=True)
    lg2 = jnp.where(lane == i1, NEG_INF, lg)
    m2 = jnp.max(lg2, -1, keepdims=True)
    i2 = jnp.min(jnp.where(lg2 == m2, lane, big), -1, keepdims=True)
    e2 = jnp.exp(m2 - m1)
    den = 1.0 + e2
    idx_ref[...] = jnp.where(lane == 0.0, i1, jnp.where(lane == 1.0, i2, 0.0)).astype(I32)
    w_ref[...] = jnp.where(lane == 0.0, 1.0 / den, jnp.where(lane == 1.0, e2 / den, 0.0))


def _router(h, router_p):
    m, d = h.shape
    row = pl.BlockSpec((TM, HEAD_DIM), lambda i: (i, 0))
    return pl.pallas_call(
        _router_kernel,
        grid=(m // TM,),
        in_specs=[pl.BlockSpec((TM, d), lambda i: (i, 0)), pl.BlockSpec((d, HEAD_DIM), lambda i: (0, 0))],
        out_specs=[row, row],
        out_shape=[jax.ShapeDtypeStruct((m, HEAD_DIM), I32), jax.ShapeDtypeStruct((m, HEAD_DIM), F32)],
        compiler_params=_params(1),
        name="router",
    )(h, router_p)


def _moe_plan(eidx, ew, n_groups, n_sub):
    m = eidx.shape[0]
    rg = n_sub * TM
    e_flat = eidx.reshape(-1)
    experts = jnp.arange(N_EXPERTS, dtype=I32)
    onehot = (e_flat[:, None] == experts[None, :]).astype(I32)
    csum = jnp.cumsum(onehot, axis=0)
    rank = jnp.sum(csum * onehot, axis=1) - 1
    counts = csum[-1]
    groups_e = (counts + rg - 1) // rg
    grp_end = jnp.cumsum(groups_e)
    grp_start = grp_end - groups_e
    pos = jnp.sum(onehot * grp_start[None, :], axis=1) * rg + rank
    token = jnp.arange(2 * m, dtype=I32) // 2
    src = jnp.zeros((n_groups * rg,), I32).at[pos].set(token)
    gids = jnp.arange(n_groups, dtype=I32)
    valid = gids < grp_end[-1]
    grp_e = jnp.sum((gids[:, None] >= grp_end[None, :]).astype(I32), axis=1)
    grp_e = jnp.where(valid, grp_e, jnp.max(jnp.where(counts > 0, experts, 0)))
    grp_b = jnp.minimum(gids, grp_end[-1] - 1).astype(I32)
    mine = (grp_e[:, None] == experts[None, :]).astype(I32)
    rows_left = jnp.sum(mine * counts[None, :], axis=1) - (gids - jnp.sum(mine * grp_start[None, :], axis=1)) * rg
    grp_n = jnp.where(valid, jnp.clip((rows_left + TM - 1) // TM, 0, n_sub), 0).astype(I32)
    tile_valid = (jnp.arange(n_groups * n_sub, dtype=I32) % n_sub < jnp.repeat(grp_n, n_sub)).astype(I32)
    pos2 = pos.reshape(m, 2).astype(I32)
    return src, (grp_e.astype(I32), grp_b, grp_n), tile_valid, pos2[:, 0], pos2[:, 1]


ROW_DMA_UNROLL = 8


def _gather_kernel(src_ref, tv_ref, h_hbm, o_ref, buf_ref, sem):
    t = pl.program_id(0)
    nt = pl.num_programs(0)

    def row_copy(row, slot, r):
        return pltpu.make_async_copy(h_hbm.at[pl.ds(row, 1), :], buf_ref.at[slot, pl.ds(r, 1), :], sem.at[slot])

    def fetch(tile):
        slot = tile % 2

        @pl.when(tv_ref[tile] == 1)
        def _():
            def issue(r, c):
                row_copy(src_ref[tile * TM + r], slot, r).start()
                return c

            lax.fori_loop(0, TM, issue, 0, unroll=ROW_DMA_UNROLL)

    @pl.when(t == 0)
    def _():
        fetch(t)

    @pl.when(t + 1 < nt)
    def _():
        fetch(t + 1)

    @pl.when(tv_ref[t] == 1)
    def _():
        slot = t % 2

        def wait(r, c):
            row_copy(0, slot, r).wait()
            return c

        lax.fori_loop(0, TM, wait, 0, unroll=ROW_DMA_UNROLL)
        o_ref[...] = buf_ref[slot].astype(o_ref.dtype)

    @pl.when(tv_ref[t] == 0)
    def _():
        o_ref[...] = jnp.zeros(o_ref.shape, o_ref.dtype)


def _gather_rows(h, src, tile_valid):
    d = h.shape[1]
    p = src.shape[0]
    grid_spec = pltpu.PrefetchScalarGridSpec(
        num_scalar_prefetch=2,
        grid=(p // TM,),
        in_specs=[pl.BlockSpec(memory_space=pl.ANY)],
        out_specs=pl.BlockSpec((TM, d), lambda t, s, tv: (t, 0)),
        scratch_shapes=[pltpu.VMEM((2, TM, d), F32), pltpu.SemaphoreType.DMA((2,))],
    )
    return pl.pallas_call(
        _gather_kernel,
        grid_spec=grid_spec,
        out_shape=jax.ShapeDtypeStruct((p, d), BF16),
        compiler_params=_params(1),
        name="moe_gather",
    )(src, tile_valid, h)


def _combine_kernel(p0_ref, p1_ref, ys_hbm, ew_ref, x_ref, gpost_ref, gpre_ref, mpost_ref, mpre_ref, xo_ref, *rest,
                    with_pre, tile0):
    if with_pre:
        h_ref, buf_ref, sem = rest
    else:
        h_ref = None
        buf_ref, sem = rest
    t = pl.program_id(0)
    nt = pl.num_programs(0)

    def row_copy(row, slot, k, r):
        return pltpu.make_async_copy(ys_hbm.at[pl.ds(row, 1), :], buf_ref.at[slot, k, pl.ds(r, 1), :], sem.at[slot])

    def fetch(tile):
        slot = tile % 2
        base = (tile + tile0) * TM

        def issue(r, c):
            row_copy(p0_ref[base + r], slot, 0, r).start()
            row_copy(p1_ref[base + r], slot, 1, r).start()
            return c

        lax.fori_loop(0, TM, issue, 0, unroll=ROW_DMA_UNROLL)

    @pl.when(t == 0)
    def _():
        fetch(t)

    @pl.when(t + 1 < nt)
    def _():
        fetch(t + 1)

    slot = t % 2

    def wait(r, c):
        row_copy(0, slot, 0, r).wait()
        row_copy(0, slot, 1, r).wait()
        return c

    lax.fori_loop(0, TM, wait, 0, unroll=ROW_DMA_UNROLL)
    f = ew_ref[:, 0:1] * buf_ref[slot, 0] + ew_ref[:, 1:2] * buf_ref[slot, 1]
    _post_pre_math(x_ref[...], f, gpost_ref, gpre_ref, mpost_ref, mpre_ref, xo_ref, h_ref, 1)


def _combine_post_pre(x, ys, pos0, pos1, ew, norm_g, mods, layer, n_ctx_tiles, h_dtype, tile0):
    m, d = x.shape
    pre_layer = min(layer + 1, norm_g.shape[0] - 1)
    with_pre = h_dtype is not None
    row_in = pl.BlockSpec((TM, d), lambda i, a, b: (i + tile0, 0))
    row_out = pl.BlockSpec((TM, d), lambda i, a, b: (i, 0))
    m_out = m - tile0 * TM

    def mod_spec(l):
        return pl.BlockSpec((None, None, 6, d), lambda i, a, b: (l, jnp.where(i + tile0 < n_ctx_tiles, 1, 0), 0, 0))

    out_shape = [jax.ShapeDtypeStruct((m_out, d), F32)]
    out_specs = [row_out]
    if with_pre:
        out_shape.append(jax.ShapeDtypeStruct((m_out, d), h_dtype))
        out_specs.append(row_out)
    grid_spec = pltpu.PrefetchScalarGridSpec(
        num_scalar_prefetch=2,
        grid=(m_out // TM,),
        in_specs=[pl.BlockSpec(memory_space=pl.ANY),
                  pl.BlockSpec((TM, HEAD_DIM), lambda i, a, b: (i + tile0, 0)), row_in,
                  pl.BlockSpec((None, 4, d), lambda i, a, b: (layer, 0, 0)),
                  pl.BlockSpec((None, 4, d), lambda i, a, b: (pre_layer, 0, 0)),
                  mod_spec(layer), mod_spec(pre_layer)],
        out_specs=out_specs,
        scratch_shapes=[pltpu.VMEM((2, 2, TM, d), F32), pltpu.SemaphoreType.DMA((2,))],
    )
    res = pl.pallas_call(
        functools.partial(_combine_kernel, with_pre=with_pre, tile0=tile0),
        grid_spec=grid_spec,
        out_shape=out_shape,
        compiler_params=_params(1),
        name="moe_combine",
    )(pos0, pos1, ys, ew, x, norm_g, norm_g, mods, mods)
    return res if with_pre else (res[0], None)


def _rope_tables(n_ctx, s):
    pos = jnp.arange(s)
    row = (pos // GRID_W).astype(F32)
    col = (pos % GRID_W).astype(F32)
    inv = ROPE_THETA ** (-jnp.arange(ROPE_FREQS, dtype=F32) / ROPE_FREQS)
    ar, ac = row[:, None] * inv, col[:, None] * inv
    cos = jnp.concatenate([jnp.cos(ar), jnp.cos(ar), jnp.cos(ac), jnp.cos(ac)], axis=-1)
    sin = jnp.concatenate([-jnp.sin(ar), jnp.sin(ar), -jnp.sin(ac), jnp.sin(ac)], axis=-1)
    cos = jnp.concatenate([jnp.ones((n_ctx, HEAD_DIM), F32), cos], axis=0)
    sin = jnp.concatenate([jnp.zeros((n_ctx, HEAD_DIM), F32), sin], axis=0)
    return cos, sin


def _layer_ab(h, j, ab_w_in, a_sink, b_ig_bias, b_fg_bias, b_norm_g, cos_t, sin_t, n_ctx):
    scale = HEAD_DIM ** -0.5 * LOG2E
    no_g = jnp.ones((2, HEAD_DIM), F32)
    a_kinds = [("rope", 0, scale)] * A_HEADS + [("rope", 0, 1.0)] * A_KV + [PLAIN] * A_KV
    a_qkv = _proj([h], ab_w_in, j, 0, a_kinds, 512, BF16, cos_t, sin_t, no_g)
    b_qkvo = _proj([h], ab_w_in, j, A_QKV, [PLAIN] * (B_QKVO // HEAD_DIM), 512, F32, cos_t, sin_t, no_g)
    ngate = 4 * B_HEADS
    w_gate = jnp.pad(ab_w_in[j, :, A_QKV + B_QKVO:], ((0, 0), (0, HEAD_DIM - ngate)))[None]
    gates = _proj([h], w_gate, 0, 0, [PLAIN], HEAD_DIM, F32, cos_t, sin_t, no_g)

    ya = _window_attn(a_qkv, a_sink[j], n_ctx)

    bias = jnp.concatenate([b_ig_bias[j].reshape(-1), b_fg_bias[j].reshape(-1)]).astype(F32)
    bias_row = jnp.pad(bias, (0, HEAD_DIM - ngate))[None, :]
    hf, hb = _mlstm(b_qkvo, gates, jnp.transpose(gates[:, :ngate]), bias_row, bias[:, None], n_ctx)
    yb = _b_out(hf, hb, b_qkvo, b_norm_g[j][None, :])
    return [ya, yb]


def _layer_c(h, j, c_w_qkv, c_qk_g, cos_t, sin_t, n_ctx):
    scale = HEAD_DIM ** -0.5 * LOG2E
    c_kinds = [("norm_rope", 0, scale)] * C_HEADS + [("norm_rope", 1, 1.0)] * C_KV + [PLAIN] * C_KV
    qkv = _proj([h], c_w_qkv, j, 0, c_kinds, 512, BF16, cos_t, sin_t, c_qk_g[j])
    m = h.shape[0]
    tk = 768 if m % 768 == 0 else TM
    return [_flash_attn(qkv, 0, C_Q, C_Q + C_KVW, C_KV, C_HEADS // C_KV, n_ctx, TM, tk)]


def kernel(x, c, ctx, c_ctx, ada_w, ada_b, norm_g, ab_w_in, ab_w_out, a_sink, b_ig_bias, b_fg_bias, b_norm_g,
           c_w_qkv, c_w_out, c_qk_g, ffn_w_gate, ffn_w_up, ffn_w_down, moe_router, moe_w_gate, moe_w_up,
           moe_w_down):
    depth = ada_w.shape[0]
    s, d = x.shape[1], x.shape[2]
    n_ctx = ctx.shape[1]
    m = n_ctx + s
    assert x.shape[0] == 1 and n_ctx % TM == 0 and s % TM == 0
    n_ctx_tiles = n_ctx // TM
    n_tiles = m // TM
    no_g = jnp.ones((2, HEAD_DIM), F32)

    mods = _ada_mods(c, c_ctx, ada_w, ada_b)
    cos_t, sin_t = _rope_tables(n_ctx, s)
    xs = jnp.concatenate([ctx[0], x[0]], axis=0)
    h = _prenorm(xs, norm_g, mods, 0, n_ctx_tiles)
    dense_groups, dense_sub = _dense_groups(m)
    n_exp = moe_w_gate.shape[1]
    moe_g = moe_w_gate.reshape((-1,) + moe_w_gate.shape[2:])
    moe_u = moe_w_up.reshape((-1,) + moe_w_up.shape[2:])
    moe_d = moe_w_down.reshape((-1,) + moe_w_down.shape[2:])

    for layer in range(depth):
        last = layer == depth - 1
        j = layer // 2
        even = layer % 2 == 0
        if even:
            mix = _layer_ab(h, j, ab_w_in, a_sink, b_ig_bias, b_fg_bias, b_norm_g, cos_t, sin_t, n_ctx)
            w_out = ab_w_out
        else:
            mix = _layer_c(h, j, c_w_qkv, c_qk_g, cos_t, sin_t, n_ctx)
            w_out = c_w_out
        y = _proj(mix, w_out, j, 0, [PLAIN] * (d // HEAD_DIM), 512 if d % 512 == 0 else d, F32, cos_t, sin_t, no_g)
        xs, h = _post_pre(xs, y, norm_g, mods, layer, 0, n_ctx_tiles, BF16 if even else F32)
        if even:
            f_dim = ffn_w_gate.shape[-1]
            f = _ffn(h, ffn_w_gate, ffn_w_up, ffn_w_down, j, *dense_groups, dense_sub, FFN_SLICE)
            xs, h = _post_pre(xs, f, norm_g, mods, layer, 1, n_ctx_tiles, None if last else BF16)
        else:
            router_p = jnp.pad(moe_router[j], ((0, 0), (0, HEAD_DIM - n_exp)))
            eidx, ew = _router(h, router_p)
            n_sub = MOE_GROUP_TILES
            n_groups = (2 * m + n_exp * (n_sub * TM - 1)) // (n_sub * TM)
            src, groups, tile_valid, pos0, pos1 = _moe_plan(eidx[:, :2], ew[:, :2], n_groups, n_sub)
            x_s = _gather_rows(h, src, tile_valid)
            y_s = _ffn(x_s, moe_g, moe_u, moe_d, j * n_exp, *groups, n_sub, FFN_SLICE)
            xs, h = _combine_post_pre(xs, y_s, pos0, pos1, ew, norm_g, mods, layer, n_ctx_tiles,
                                      None if last else BF16, n_ctx_tiles if last else 0)
    return (xs if xs.shape[0] == s else xs[n_ctx:])[None]
```

```python
import functools

import jax
import jax.numpy as jnp
import numpy as np
from jax import lax
from jax.experimental import pallas as pl
from jax.experimental.pallas import tpu as pltpu

F32 = jnp.float32
BF16 = jnp.bfloat16
I32 = jnp.int32

EPS = 1e-6
GRID_W = 64
HEAD_DIM = 128
ROPE_THETA = 10000.0
ROPE_FREQS = HEAD_DIM // 4
BLK = 128
A_HEADS = 8
A_KV = 2
B_HEADS = 4
B_DK = 128
B_DV = 256
MLSTM_CHUNK = 128
C_HEADS = 16
C_KV = 4
N_EXPERTS = 8

A_Q = A_HEADS * HEAD_DIM
A_QKV = A_Q + 2 * A_KV * HEAD_DIM
B_Q = B_HEADS * B_DK
B_V = B_HEADS * B_DV
B_QKVO = 2 * B_Q + 2 * B_V
C_Q = C_HEADS * HEAD_DIM
C_KVW = C_KV * HEAD_DIM

TM = 256
MOE_GROUP_TILES = 9
PROJ_BLOCKS = ((1408, 352), (768, 384))
FFN_SLICE = 256
VMEM_LIMIT = 56 * 1024 * 1024
NEG_INF = float("-inf")
LOG2E = 1.4426950408889634


def _row_block(m):
    return next(r * TM for r in (3, 2, 1) if m % (r * TM) == 0)


def _params(n_axes, vmem=VMEM_LIMIT):
    return pltpu.CompilerParams(dimension_semantics=("arbitrary",) * n_axes, vmem_limit_bytes=vmem)


def _rms(x, g):
    return x * lax.rsqrt(jnp.mean(x * x, axis=-1, keepdims=True) + EPS) * g


def _dot(a, b):
    return jnp.dot(a, b, preferred_element_type=F32)


def _dot_nt(a, b):
    return lax.dot_general(a, b, (((1,), (1,)), ((), ())), preferred_element_type=F32)


def _ada_kernel(s_ref, w_ref, b_ref, o_ref):
    s = s_ref[...]
    s = s * jax.nn.sigmoid(s)
    o_ref[...] = _dot(s.astype(BF16), w_ref[...].astype(BF16)) + b_ref[...]


def _ada_mods(c, c_ctx, ada_w, ada_b):
    depth, d, d6 = ada_w.shape
    tn = 1536 if d6 % 1536 == 0 else d6
    s = jnp.zeros((8, d), F32).at[0].set(c[0]).at[1].set(c_ctx)
    out = pl.pallas_call(
        _ada_kernel,
        grid=(depth, d6 // tn),
        in_specs=[pl.BlockSpec((8, d), lambda l, n: (0, 0)),
                  pl.BlockSpec((None, d, tn), lambda l, n: (l, 0, n)),
                  pl.BlockSpec((None, 1, tn), lambda l, n: (l, 0, n))],
        out_specs=pl.BlockSpec((None, 8, tn), lambda l, n: (l, 0, n)),
        out_shape=jax.ShapeDtypeStruct((depth, 8, d6), F32),
        compiler_params=_params(2),
        name="ada_mods",
    )(s, ada_w, ada_b.reshape(depth, 1, d6))
    return out[:, :2].reshape(depth, 2, 6, d)


def _mod_spec(d, layer, n_ctx_tiles):
    return pl.BlockSpec((None, None, 6, d), lambda i: (layer, jnp.where(i < n_ctx_tiles, 1, 0), 0, 0))


def _prenorm_kernel(x_ref, g_ref, mod_ref, h_ref):
    hn = _rms(x_ref[...], g_ref[0:1, :])
    h_ref[...] = (hn * (1.0 + mod_ref[1:2, :]) + mod_ref[0:1, :]).astype(h_ref.dtype)


def _prenorm(x, norm_g, mods, layer, n_ctx_tiles):
    m, d = x.shape
    return pl.pallas_call(
        _prenorm_kernel,
        grid=(m // TM,),
        in_specs=[pl.BlockSpec((TM, d), lambda i: (i, 0)),
                  pl.BlockSpec((None, 4, d), lambda i: (layer, 0, 0)),
                  _mod_spec(d, layer, n_ctx_tiles)],
        out_specs=pl.BlockSpec((TM, d), lambda i: (i, 0)),
        out_shape=jax.ShapeDtypeStruct((m, d), BF16),
        compiler_params=_params(1),
        name="prenorm",
    )(x, norm_g, mods)


def _post_pre_math(x, y, gpost_ref, gpre_ref, mpost_ref, mpre_ref, xo_ref, h_ref, sub):
    g_row = 1 + 2 * sub
    gate_row = 2 + 3 * sub
    xn = x + mpost_ref[gate_row:gate_row + 1, :] * _rms(y, gpost_ref[g_row:g_row + 1, :])
    xo_ref[...] = xn
    if h_ref is not None:
        nsub = 1 - sub
        hn = _rms(xn, gpre_ref[2 * nsub:2 * nsub + 1, :])
        h_ref[...] = (hn * (1.0 + mpre_ref[3 * nsub + 1:3 * nsub + 2, :])
                      + mpre_ref[3 * nsub:3 * nsub + 1, :]).astype(h_ref.dtype)


def _post_pre_kernel(x_ref, y_ref, gpost_ref, gpre_ref, mpost_ref, mpre_ref, xo_ref, *h_ref, sub):
    _post_pre_math(x_ref[...], y_ref[...], gpost_ref, gpre_ref, mpost_ref, mpre_ref, xo_ref,
                   h_ref[0] if h_ref else None, sub)


def _post_pre(x, y, norm_g, mods, layer, sub, n_ctx_tiles, h_dtype):
    m, d = x.shape
    pre_layer = layer if sub == 0 else min(layer + 1, norm_g.shape[0] - 1)
    row = pl.BlockSpec((TM, d), lambda i: (i, 0))
    out_shape = [jax.ShapeDtypeStruct((m, d), F32)]
    out_specs = [row]
    if h_dtype is not None:
        out_shape.append(jax.ShapeDtypeStruct((m, d), h_dtype))
        out_specs.append(row)
    res = pl.pallas_call(
        functools.partial(_post_pre_kernel, sub=sub),
        grid=(m // TM,),
        in_specs=[row, row,
                  pl.BlockSpec((None, 4, d), lambda i: (layer, 0, 0)),
                  pl.BlockSpec((None, 4, d), lambda i: (pre_layer, 0, 0)),
                  _mod_spec(d, layer, n_ctx_tiles),
                  _mod_spec(d, pre_layer, n_ctx_tiles)],
        out_specs=out_specs,
        out_shape=out_shape,
        compiler_params=_params(1),
        name="post_pre",
    )(x, y, norm_g, norm_g, mods, mods)
    return res if h_dtype is not None else (res[0], None)


def _rope(y, cos, sin):
    lane = lax.broadcasted_iota(I32, y.shape, 1)
    lower = (lane & (2 * ROPE_FREQS - 1)) < ROPE_FREQS
    partner = jnp.where(lower, pltpu.roll(y, HEAD_DIM - ROPE_FREQS, 1), pltpu.roll(y, ROPE_FREQS, 1))
    return y * cos + partner * sin


PLAIN = ("plain", 0, 1.0)


def _proj_kernel(*refs, tile_kinds, rows, k_split):
    xs_refs = refs[:len(k_split)]
    w_ref, cos_ref, sin_ref, g_ref, o_ref, wb_ref = refs[len(k_split):]
    n = pl.program_id(1)

    @pl.when(pl.program_id(0) == 0)
    def _():
        wb_ref[n] = w_ref[...].astype(BF16)

    def matmul(rs):
        k0, y = 0, None
        for x_ref, kk in zip(xs_refs, k_split):
            part = _dot(x_ref[rs, :], wb_ref[n, k0:k0 + kk, :])
            y = part if y is None else y + part
            k0 += kk
        return y

    def run(kinds):
        for c in range(o_ref.shape[0] // rows):
            rs = slice(c * rows, (c + 1) * rows)
            y = matmul(rs)
            if all(kd == PLAIN for kd in kinds):
                o_ref[rs, :] = y.astype(o_ref.dtype)
                continue
            for hh, (kind, grow, scale) in enumerate(kinds):
                sl = slice(hh * HEAD_DIM, (hh + 1) * HEAD_DIM)
                yh = y[:, sl]
                if kind == "norm_rope":
                    yh = _rms(yh, g_ref[grow:grow + 1, :])
                if kind != "plain":
                    yh = _rope(yh, cos_ref[rs, :], sin_ref[rs, :])
                if scale != 1.0:
                    yh = yh * scale
                o_ref[rs, sl] = yh.astype(o_ref.dtype)

    t0 = 0
    while t0 < len(tile_kinds):
        t1 = t0 + 1
        while t1 < len(tile_kinds) and tile_kinds[t1] == tile_kinds[t0]:
            t1 += 1
        pl.when((n >= t0) & (n < t1))(functools.partial(run, tile_kinds[t0]))
        t0 = t1


def _proj(hs, w3, layer_idx, col0, head_kinds, tn, out_dtype, cos_t, sin_t, gains):
    m = hs[0].shape[0]
    k_split = tuple(h.shape[1] for h in hs)
    k = sum(k_split)
    ncols = HEAD_DIM * len(head_kinds)
    assert col0 % tn == 0 and ncols % tn == 0 and w3.shape[1] == k
    n0 = col0 // tn
    nt = ncols // tn
    hpt = tn // HEAD_DIM
    tile_kinds = tuple(tuple(head_kinds[t * hpt:(t + 1) * hpt]) for t in range(nt))
    tm, rows = next(((a, b) for a, b in PROJ_BLOCKS if m % a == 0), (TM, TM))

    def w_tile(i, n):
        return (layer_idx, 0, n0 + jnp.where(i == 0, n, nt - 1))

    return pl.pallas_call(
        functools.partial(_proj_kernel, tile_kinds=tile_kinds, rows=rows, k_split=k_split),
        grid=(m // tm, nt),
        in_specs=[pl.BlockSpec((tm, kk), lambda i, n: (i, 0)) for kk in k_split]
                 + [pl.BlockSpec((None, k, tn), w_tile),
                    pl.BlockSpec((tm, HEAD_DIM), lambda i, n: (i, 0)),
                    pl.BlockSpec((tm, HEAD_DIM), lambda i, n: (i, 0)),
                    pl.BlockSpec((2, HEAD_DIM), lambda i, n: (0, 0))],
        out_specs=pl.BlockSpec((tm, tn), lambda i, n: (i, n)),
        out_shape=jax.ShapeDtypeStruct((m, ncols), out_dtype),
        scratch_shapes=[pltpu.VMEM((nt, k, tn), BF16)],
        compiler_params=_params(2),
        name="proj",
    )(*hs, w3, cos_t, sin_t, gains)


WIN_QB = 2


def _win_kernel(sink_ref, q_ref, kp_ref, kc_ref, kn_ref, kx_ref, vp_ref, vc_ref, vn_ref, vx_ref, o_ref, *, group,
                ctx_steps):
    kvh = pl.program_id(0)
    i = pl.program_id(1)
    nq = WIN_QB * BLK
    nk = (WIN_QB + 2) * BLK
    qi = lax.broadcasted_iota(I32, (group * nq, nk), 0) & (nq - 1)
    kj = lax.broadcasted_iota(I32, (group * nq, nk), 1)
    rel = kj - BLK - qi
    lo = jnp.where(i < ctx_steps, nk, jnp.where(i > ctx_steps, 0, BLK))
    hi = jnp.where(i < ctx_steps, 0, jnp.where(i < pl.num_programs(1) - 1, nk, nk - BLK))
    mask = (rel >= -BLK) & (rel <= BLK) & (kj >= lo) & (kj < hi)
    k_win = jnp.concatenate([kp_ref[...], kc_ref[...], kn_ref[...]], axis=0)
    v_win = jnp.concatenate([vp_ref[...], vc_ref[...], vn_ref[...]], axis=0)
    q = jnp.concatenate([q_ref[:, g * HEAD_DIM:(g + 1) * HEAD_DIM] for g in range(group)], axis=0)
    snk = jnp.concatenate([jnp.full((nq, 1), sink_ref[kvh * group + g] * LOG2E, F32) for g in range(group)], axis=0)
    s_loc = jnp.where(mask, _dot_nt(q, k_win), NEG_INF)
    s_ctx = _dot_nt(q, kx_ref[...])
    mx = jnp.maximum(jnp.maximum(jnp.max(s_loc, -1, keepdims=True), jnp.max(s_ctx, -1, keepdims=True)), snk)
    p_loc = jnp.exp2(s_loc - mx)
    p_ctx = jnp.exp2(s_ctx - mx)
    den = jnp.sum(p_loc, -1, keepdims=True) + jnp.sum(p_ctx, -1, keepdims=True) + jnp.exp2(snk - mx)
    o = (_dot(p_loc.astype(BF16), v_win) + _dot(p_ctx.astype(BF16), vx_ref[...])) / den
    for g in range(group):
        o_ref[:, g * HEAD_DIM:(g + 1) * HEAD_DIM] = o[g * nq:(g + 1) * nq, :].astype(o_ref.dtype)


def _window_attn(qkv, sink, n_ctx):
    m = qkv.shape[0]
    nb = (m - n_ctx) // BLK
    cb = n_ctx // BLK
    assert nb % WIN_QB == 0 and cb % WIN_QB == 0
    cq = cb // WIN_QB
    group = A_HEADS // A_KV
    kcol = A_Q // HEAD_DIM
    vcol = kcol + A_KV

    def edge_spec(col, block):
        return pl.BlockSpec((BLK, HEAD_DIM),
                            lambda h, i, s: (cb + jnp.clip((i - cq) * WIN_QB + block, 0, nb - 1), col + h))

    def own_spec(col):
        return pl.BlockSpec((WIN_QB * BLK, HEAD_DIM), lambda h, i, s: (i, col + h))

    def ctx_spec(col):
        return pl.BlockSpec((n_ctx, HEAD_DIM), lambda h, i, s: (0, col + h))

    grid_spec = pltpu.PrefetchScalarGridSpec(
        num_scalar_prefetch=1,
        grid=(A_KV, cq + nb // WIN_QB),
        in_specs=[pl.BlockSpec((WIN_QB * BLK, group * HEAD_DIM), lambda h, i, s: (i, h)),
                  edge_spec(kcol, -1), own_spec(kcol), edge_spec(kcol, WIN_QB), ctx_spec(kcol),
                  edge_spec(vcol, -1), own_spec(vcol), edge_spec(vcol, WIN_QB), ctx_spec(vcol)],
        out_specs=pl.BlockSpec((WIN_QB * BLK, group * HEAD_DIM), lambda h, i, s: (i, h)),
    )
    return pl.pallas_call(
        functools.partial(_win_kernel, group=group, ctx_steps=cq),
        grid_spec=grid_spec,
        out_shape=jax.ShapeDtypeStruct((m, A_Q), BF16),
        compiler_params=_params(2),
        name="window_attn",
    )(sink, qkv, qkv, qkv, qkv, qkv, qkv, qkv, qkv, qkv)


def _flash_kernel(q_ref, k_ref, v_ref, o_ref, va_ref, m_ref, acc_ref, *, group, tq, tk, n_ctx):
    rows = group * tq
    n_keys = k_ref.shape[0]

    @pl.when(pl.program_id(1) == 0)
    def _():
        va_ref[:, :HEAD_DIM] = v_ref[...]
        va_ref[:, HEAD_DIM:] = jnp.ones((n_keys, HEAD_DIM), BF16)

    def attend(limit, step):
        q = jnp.concatenate([q_ref[:, g * HEAD_DIM:(g + 1) * HEAD_DIM] for g in range(group)], axis=0)
        m_ref[...] = jnp.full((rows, HEAD_DIM), NEG_INF, F32)
        acc_ref[...] = jnp.zeros((rows, 2 * HEAD_DIM), F32)
        for j in range(limit // step):
            ks = slice(j * step, (j + 1) * step)
            s = _dot_nt(q, k_ref[ks, :])
            cols = [s[:, c * HEAD_DIM:(c + 1) * HEAD_DIM] for c in range(step // HEAD_DIM)]
            m_prev = m_ref[...]
            m_new = jnp.maximum(m_prev, jnp.max(functools.reduce(jnp.maximum, cols), -1, keepdims=True))
            alpha = jnp.exp2(m_prev - m_new)
            p = jnp.concatenate([jnp.exp2(c - m_new).astype(BF16) for c in cols], axis=1)
            acc_ref[...] = jnp.concatenate([alpha, alpha], axis=1) * acc_ref[...] + _dot(p, va_ref[ks, :])
            m_ref[...] = m_new
        for g in range(group):
            rs = slice(g * tq, (g + 1) * tq)
            o_ref[:, g * HEAD_DIM:(g + 1) * HEAD_DIM] = (acc_ref[rs, :HEAD_DIM] / acc_ref[rs, HEAD_DIM:]).astype(o_ref.dtype)

    ctx_tiles = n_ctx // tq
    pl.when(pl.program_id(1) < ctx_tiles)(functools.partial(attend, n_ctx, n_ctx))
    pl.when(pl.program_id(1) >= ctx_tiles)(functools.partial(attend, n_keys, tk))


def _flash_attn(qkv, q_col0, k_col0, v_col0, n_kv, group, n_ctx, tq, tk):
    m = qkv.shape[0]
    assert m % tq == 0 and n_ctx % tq == 0 and m % tk == 0
    qc0 = q_col0 // (group * HEAD_DIM)
    kc0 = k_col0 // HEAD_DIM
    vc0 = v_col0 // HEAD_DIM
    rows = group * tq
    return pl.pallas_call(
        functools.partial(_flash_kernel, group=group, tq=tq, tk=tk, n_ctx=n_ctx),
        grid=(n_kv, m // tq),
        in_specs=[pl.BlockSpec((tq, group * HEAD_DIM), lambda h, i: (i, qc0 + h)),
                  pl.BlockSpec((m, HEAD_DIM), lambda h, i: (0, kc0 + h)),
                  pl.BlockSpec((m, HEAD_DIM), lambda h, i: (0, vc0 + h))],
        out_specs=pl.BlockSpec((tq, group * HEAD_DIM), lambda h, i: (i, h)),
        out_shape=jax.ShapeDtypeStruct((m, n_kv * group * HEAD_DIM), BF16),
        scratch_shapes=[pltpu.VMEM((m, 2 * HEAD_DIM), BF16), pltpu.VMEM((rows, HEAD_DIM), F32),
                        pltpu.VMEM((rows, 2 * HEAD_DIM), F32)],
        compiler_params=_params(2),
        name="flash_attn",
    )(qkv, qkv, qkv)


def _log_sigmoid(x):
    return jnp.minimum(x, 0.0) - jnp.log(1.0 + jnp.exp(-jnp.abs(x)))


def _dot_exact(a, b):
    return jnp.dot(a, b, preferred_element_type=F32, precision=lax.Precision.HIGHEST)


def _mlstm_head(q, k, v, i_col, i_row, b_col, b_row, b_tot, mask, c_ref, n_ref, m_ref):
    m_prev = m_ref[:, 0:1]
    c_prev = c_ref[...]
    n_prev = n_ref[...]
    qs = q * (B_DK ** -0.5)
    qb = qs.astype(BF16)
    dlog = jnp.where(mask, b_col - b_row + i_row, NEG_INF)
    inter = b_col + m_prev
    m_t = jnp.maximum(inter, jnp.max(dlog, -1, keepdims=True))
    dw = jnp.exp(dlog - m_t)
    iw = jnp.exp(inter - m_t)
    sc = _dot_nt(qb, k.astype(BF16)) * dw
    num = _dot(sc.astype(BF16), v.astype(BF16)) + iw * _dot(qb, c_prev.astype(BF16))
    den = jnp.sum(sc, -1, keepdims=True) + iw * jnp.sum(qs * n_prev, -1, keepdims=True)
    h = num / jnp.maximum(jnp.abs(den), jnp.exp(-m_t))
    glog_col = b_tot - b_col + i_col
    glog_row = b_tot - b_row + i_row
    m_new = jnp.maximum(b_tot + m_prev, jnp.max(glog_row, -1, keepdims=True))
    decay = jnp.exp(b_tot + m_prev - m_new)
    wk = jnp.exp(glog_col - m_new) * k
    c_ref[...] = decay * c_prev + lax.dot_general(wk.astype(BF16), v.astype(BF16), (((0,), (0,)), ((), ())),
                                                  preferred_element_type=F32)
    n_ref[...] = decay * n_prev + jnp.sum(wk, 0, keepdims=True)
    m_ref[...] = jnp.broadcast_to(m_new, m_ref.shape)
    return h


def _mlstm_kernel(brow_ref, bcol_ref,
                  qf_ref, kf_ref, vf_ref, gf_ref, gtf_ref,
                  qb_ref, kb_ref, vb_ref, gb_ref, gtb_ref,
                  of_ref, ob_ref, c_ref, n_ref, m_ref):
    L = MLSTM_CHUNK

    @pl.when(pl.program_id(0) == 0)
    def _():
        c_ref[...] = jnp.zeros(c_ref.shape, F32)
        n_ref[...] = jnp.zeros(n_ref.shape, F32)
        m_ref[...] = jnp.zeros(m_ref.shape, F32)

    r = lax.broadcasted_iota(I32, (L, L), 0)
    cidx = lax.broadcasted_iota(I32, (L, L), 1)
    lane = lax.broadcasted_iota(I32, (L, HEAD_DIM), 1)
    sub = lax.broadcasted_iota(I32, (2 * 2 * B_HEADS, L), 0)
    for d, (q_ref, k_ref, v_ref, g_ref, gt_ref, o_ref) in enumerate(
            ((qf_ref, kf_ref, vf_ref, gf_ref, gtf_ref, of_ref), (qb_ref, kb_ref, vb_ref, gb_ref, gtb_ref, ob_ref))):
        mask = (cidx <= r) if d == 0 else (cidx >= r)
        gc = g_ref[...] + brow_ref[...]
        gc = jnp.where(lane < 2 * B_HEADS, gc, _log_sigmoid(gc))
        gr = gt_ref[...] + bcol_ref[...]
        gr = jnp.where(sub < 2 * B_HEADS, gr, _log_sigmoid(gr))
        mask_t = (r <= cidx) if d == 0 else (r >= cidx)
        bc_all = _dot_exact(mask.astype(F32), gc)
        br_all = _dot_exact(gr, mask_t.astype(F32))
        edge = L - 1 if d == 0 else 0
        for hd in range(B_HEADS):
            ci = d * B_HEADS + hd
            cf = 2 * B_HEADS + ci
            b_col = bc_all[:, cf:cf + 1]
            h = _mlstm_head(q_ref[:, hd * B_DK:(hd + 1) * B_DK], k_ref[:, hd * B_DK:(hd + 1) * B_DK],
                            v_ref[:, hd * B_DV:(hd + 1) * B_DV],
                            gc[:, ci:ci + 1], gr[ci:ci + 1, :], b_col, br_all[cf:cf + 1, :],
                            b_col[edge:edge + 1, :], mask, c_ref.at[ci], n_ref.at[ci], m_ref.at[ci])
            o_ref[:, hd * B_DV:(hd + 1) * B_DV] = h


def _mlstm(qkvo, gates, gates_t, bias_row, bias_col, n_ctx):
    m = qkvo.shape[0]
    L = MLSTM_CHUNK
    nc = m // L
    ncc = n_ctx // L
    kq = B_Q // B_Q
    kv = (2 * B_Q) // B_V

    def fw(s):
        return s

    def bw(s):
        return jnp.where(s < ncc, ncc - 1 - s, nc - 1 + ncc - s)

    def specs(order):
        return [pl.BlockSpec((L, B_Q), lambda s: (order(s), 0)),
                pl.BlockSpec((L, B_Q), lambda s: (order(s), kq)),
                pl.BlockSpec((L, B_V), lambda s: (order(s), kv)),
                pl.BlockSpec((L, HEAD_DIM), lambda s: (order(s), 0)),
                pl.BlockSpec((4 * B_HEADS, L), lambda s: (0, order(s)))]

    nst = 2 * B_HEADS
    return pl.pallas_call(
        _mlstm_kernel,
        grid=(nc,),
        in_specs=[pl.BlockSpec((1, HEAD_DIM), lambda s: (0, 0)),
                  pl.BlockSpec((4 * B_HEADS, 1), lambda s: (0, 0))] + specs(fw) + specs(bw),
        out_specs=[pl.BlockSpec((L, B_V), lambda s: (fw(s), 0)),
                   pl.BlockSpec((L, B_V), lambda s: (bw(s), 0))],
        out_shape=[jax.ShapeDtypeStruct((m, B_V), F32), jax.ShapeDtypeStruct((m, B_V), F32)],
        scratch_shapes=[pltpu.VMEM((nst, B_DK, B_DV), F32), pltpu.VMEM((nst, 1, B_DK), F32),
                        pltpu.VMEM((nst, 1, HEAD_DIM), F32)],
        compiler_params=_params(1),
        name="mlstm",
    )(bias_row, bias_col, qkvo, qkvo, qkvo, gates, gates_t, qkvo, qkvo, qkvo, gates, gates_t)


def _bout_kernel(hf_ref, hb_ref, o_ref, g_ref, y_ref):
    for hd in range(B_HEADS):
        sl = slice(hd * B_DV, (hd + 1) * B_DV)
        hn = _rms(hf_ref[:, sl] + hb_ref[:, sl], g_ref[:, sl])
        y_ref[:, sl] = (jax.nn.sigmoid(o_ref[:, sl]) * hn).astype(y_ref.dtype)


def _b_out(hf, hb, qkvo, out_g):
    m = hf.shape[0]
    ocol = (2 * B_Q + B_V) // B_V
    row = pl.BlockSpec((TM, B_V), lambda i: (i, 0))
    return pl.pallas_call(
        _bout_kernel,
        grid=(m // TM,),
        in_specs=[row, row, pl.BlockSpec((TM, B_V), lambda i: (i, ocol)),
                  pl.BlockSpec((1, B_V), lambda i: (0, 0))],
        out_specs=row,
        out_shape=jax.ShapeDtypeStruct((m, B_V), BF16),
        compiler_params=_params(1),
        name="mlstm_out",
    )(hf, hb, qkvo, out_g)


def _ffn_kernel(ge_ref, gb_ref, gn_ref, x_ref, wg_ref, wu_ref, wd_ref, o_ref, wgb_ref, wub_ref, wdb_ref, *, n_sub):
    g = pl.program_id(0)
    j = pl.program_id(1)
    gn = gn_ref[g]
    n_quads = lax.shift_right_logical(gn, 2)

    def cast_weights():
        wgb_ref[...] = wg_ref[...].astype(BF16)
        wub_ref[...] = wu_ref[...].astype(BF16)
        wdb_ref[...] = wd_ref[...].astype(BF16)

    def rows_step(off, rows, first):
        x = x_ref[pl.ds(off, rows), :]
        a = _dot(x, wgb_ref[...])
        b = _dot(x, wub_ref[...])
        y = _dot((a * jax.nn.sigmoid(a) * b).astype(BF16), wdb_ref[...])
        if first:
            o_ref[pl.ds(off, rows), :] = y
        else:
            o_ref[pl.ds(off, rows), :] += y

    def used_tiles(first):
        @pl.when(n_quads > 0)
        def _():
            cast_weights()
            rows_step(0, 4 * TM, first)

        @pl.when((n_quads == 0) & (gn > 0))
        def _():
            cast_weights()

        def quad(rq, c):
            rows_step(pl.multiple_of(rq * (4 * TM), 4 * TM), 4 * TM, first)
            return c

        lax.fori_loop(1, n_quads, quad, 0)

        @pl.when((gn & 2) == 2)
        def _():
            rows_step(pl.multiple_of((gn & ~3) * TM, 2 * TM), 2 * TM, first)

        @pl.when((gn & 1) == 1)
        def _():
            rows_step(pl.multiple_of((gn - 1) * TM, TM), TM, first)

    @pl.when(j == 0)
    def _():
        used_tiles(True)

        def zero(r, c):
            off = pl.multiple_of(r * TM, TM)
            o_ref[pl.ds(off, TM), :] = jnp.zeros((TM, o_ref.shape[1]), F32)
            return c

        lax.fori_loop(gn, n_sub, zero, 0)

    @pl.when(j > 0)
    def _():
        used_tiles(False)


def _ffn(x, wg, wu, wd, e0, grp_e, grp_b, grp_n, n_sub, tf):
    p, d = x.shape
    f = wg.shape[-1]
    rg = n_sub * TM
    assert f % tf == 0 and p % rg == 0
    once = pl.Buffered(1)
    nj = f // tf

    def slice_of(j, g, gn):
        return jnp.where(gn[g] > 0, j, nj - 1)

    grid_spec = pltpu.PrefetchScalarGridSpec(
        num_scalar_prefetch=3,
        grid=(grp_e.shape[0], nj),
        in_specs=[pl.BlockSpec((rg, d), lambda g, j, ge, gb, gn: (gb[g], 0), pipeline_mode=once),
                  pl.BlockSpec((None, d, tf), lambda g, j, ge, gb, gn: (e0 + ge[g], 0, slice_of(j, g, gn))),
                  pl.BlockSpec((None, d, tf), lambda g, j, ge, gb, gn: (e0 + ge[g], 0, slice_of(j, g, gn))),
                  pl.BlockSpec((None, tf, d), lambda g, j, ge, gb, gn: (e0 + ge[g], slice_of(j, g, gn), 0))],
        out_specs=pl.BlockSpec((rg, d), lambda g, j, ge, gb, gn: (g, 0), pipeline_mode=once),
        scratch_shapes=[pltpu.VMEM((d, tf), BF16), pltpu.VMEM((d, tf), BF16), pltpu.VMEM((tf, d), BF16)],
    )
    return pl.pallas_call(
        functools.partial(_ffn_kernel, n_sub=n_sub),
        grid_spec=grid_spec,
        out_shape=jax.ShapeDtypeStruct((p, d), F32),
        compiler_params=_params(2),
        name="ffn",
    )(grp_e, grp_b, grp_n, x, wg, wu, wd)


def _dense_groups(m):
    n_sub = next(r for r in (11, 3, 2, 1) if m % (r * TM) == 0)
    n_groups = m // (n_sub * TM)
    return (jnp.zeros((n_groups,), I32), jnp.arange(n_groups, dtype=I32), jnp.full((n_groups,), n_sub, I32)), n_sub


def _router_kernel(h_ref, r_ref, idx_ref, w_ref):
    logits = _dot(h_ref[...].astype(BF16), r_ref[...].astype(BF16))
    lane = lax.broadcasted_iota(I32, logits.shape, 1).astype(F32)
    big = float(HEAD_DIM)
    lg = jnp.where(lane < N_EXPERTS, logits, NEG_INF)
    m1 = jnp.max(lg, -1, keepdims=True)
    i1 = jnp.min(jnp.where(lg == m1, lane, big), -1, keepdims=True)
    lg2 = jnp.where(lane == i1, NEG_INF, lg)
    m2 = jnp.max(lg2, -1, keepdims=True)
    i2 = jnp.min(jnp.where(lg2 == m2, lane, big), -1, keepdims=True)
    e2 = jnp.exp(m2 - m1)
    den = 1.0 + e2
    idx_ref[...] = jnp.where(lane == 0.0, i1, jnp.where(lane == 1.0, i2, 0.0)).astype(I32)
    w_ref[...] = jnp.where(lane == 0.0, 1.0 / den, jnp.where(lane == 1.0, e2 / den, 0.0))


def _router(h, router_p):
    m, d = h.shape
    row = pl.BlockSpec((TM, HEAD_DIM), lambda i: (i, 0))
    return pl.pallas_call(
        _router_kernel,
        grid=(m // TM,),
        in_specs=[pl.BlockSpec((TM, d), lambda i: (i, 0)), pl.BlockSpec((d, HEAD_DIM), lambda i: (0, 0))],
        out_specs=[row, row],
        out_shape=[jax.ShapeDtypeStruct((m, HEAD_DIM), I32), jax.ShapeDtypeStruct((m, HEAD_DIM), F32)],
        compiler_params=_params(1),
        name="router",
    )(h, router_p)


def _moe_plan(eidx, ew, n_groups, n_sub):
    m = eidx.shape[0]
    rg = n_sub * TM
    e_flat = eidx.reshape(-1)
    experts = jnp.arange(N_EXPERTS, dtype=I32)
    onehot = (e_flat[:, None] == experts[None, :]).astype(I32)
    csum = jnp.cumsum(onehot, axis=0)
    rank = jnp.sum(csum * onehot, axis=1) - 1
    counts = csum[-1]
    groups_e = (counts + rg - 1) // rg
    grp_end = jnp.cumsum(groups_e)
    grp_start = grp_end - groups_e
    pos = jnp.sum(onehot * grp_start[None, :], axis=1) * rg + rank
    token = jnp.arange(2 * m, dtype=I32) // 2
    src = jnp.zeros((n_groups * rg,), I32).at[pos].set(token)
    gids = jnp.arange(n_groups, dtype=I32)
    valid = gids < grp_end[-1]
    grp_e = jnp.sum((gids[:, None] >= grp_end[None, :]).astype(I32), axis=1)
    grp_e = jnp.where(valid, grp_e, jnp.max(jnp.where(counts > 0, experts, 0)))
    grp_b = jnp.minimum(gids, grp_end[-1] - 1).astype(I32)
    mine = (grp_e[:, None] == experts[None, :]).astype(I32)
    rows_left = jnp.sum(mine * counts[None, :], axis=1) - (gids - jnp.sum(mine * grp_start[None, :], axis=1)) * rg
    grp_n = jnp.where(valid, jnp.clip((rows_left + TM - 1) // TM, 0, n_sub), 0).astype(I32)
    tile_valid = (jnp.arange(n_groups * n_sub, dtype=I32) % n_sub < jnp.repeat(grp_n, n_sub)).astype(I32)
    pos2 = pos.reshape(m, 2).astype(I32)
    return src, (grp_e.astype(I32), grp_b, grp_n), tile_valid, pos2[:, 0], pos2[:, 1]


ROW_DMA_UNROLL = 8


def _gather_kernel(src_ref, tv_ref, h_hbm, o_ref, buf_ref, sem):
    t = pl.program_id(0)
    nt = pl.num_programs(0)

    def row_copy(row, slot, r):
        return pltpu.make_async_copy(h_hbm.at[pl.ds(row, 1), :], buf_ref.at[slot, pl.ds(r, 1), :], sem.at[slot])

    def fetch(tile):
        slot = tile % 2

        @pl.when(tv_ref[tile] == 1)
        def _():
            def issue(r, c):
                row_copy(src_ref[tile * TM + r], slot, r).start()
                return c

            lax.fori_loop(0, TM, issue, 0, unroll=ROW_DMA_UNROLL)

    @pl.when(t == 0)
    def _():
        fetch(t)

    @pl.when(t + 1 < nt)
    def _():
        fetch(t + 1)

    @pl.when(tv_ref[t] == 1)
    def _():
        slot = t % 2

        def wait(r, c):
            row_copy(0, slot, r).wait()
            return c

        lax.fori_loop(0, TM, wait, 0, unroll=ROW_DMA_UNROLL)
        o_ref[...] = buf_ref[slot].astype(o_ref.dtype)

    @pl.when(tv_ref[t] == 0)
    def _():
        o_ref[...] = jnp.zeros(o_ref.shape, o_ref.dtype)


def _gather_rows(h, src, tile_valid):
    d = h.shape[1]
    p = src.shape[0]
    grid_spec = pltpu.PrefetchScalarGridSpec(
        num_scalar_prefetch=2,
        grid=(p // TM,),
        in_specs=[pl.BlockSpec(memory_space=pl.ANY)],
        out_specs=pl.BlockSpec((TM, d), lambda t, s, tv: (t, 0)),
        scratch_shapes=[pltpu.VMEM((2, TM, d), F32), pltpu.SemaphoreType.DMA((2,))],
    )
    return pl.pallas_call(
        _gather_kernel,
        grid_spec=grid_spec,
        out_shape=jax.ShapeDtypeStruct((p, d), BF16),
        compiler_params=_params(1),
        name="moe_gather",
    )(src, tile_valid, h)


def _combine_kernel(p0_ref, p1_ref, ys_hbm, ew_ref, x_ref, gpost_ref, gpre_ref, mpost_ref, mpre_ref, xo_ref, *rest,
                    with_pre, tile0):
    if with_pre:
        h_ref, buf_ref, sem = rest
    else:
        h_ref = None
        buf_ref, sem = rest
    t = pl.program_id(0)
    nt = pl.num_programs(0)

    def row_copy(row, slot, k, r):
        return pltpu.make_async_copy(ys_hbm.at[pl.ds(row, 1), :], buf_ref.at[slot, k, pl.ds(r, 1), :], sem.at[slot])

    def fetch(tile):
        slot = tile % 2
        base = (tile + tile0) * TM

        def issue(r, c):
            row_copy(p0_ref[base + r], slot, 0, r).start()
            row_copy(p1_ref[base + r], slot, 1, r).start()
            return c

        lax.fori_loop(0, TM, issue, 0, unroll=ROW_DMA_UNROLL)

    @pl.when(t == 0)
    def _():
        fetch(t)

    @pl.when(t + 1 < nt)
    def _():
        fetch(t + 1)

    slot = t % 2

    def wait(r, c):
        row_copy(0, slot, 0, r).wait()
        row_copy(0, slot, 1, r).wait()
        return c

    lax.fori_loop(0, TM, wait, 0, unroll=ROW_DMA_UNROLL)
    f = ew_ref[:, 0:1] * buf_ref[slot, 0] + ew_ref[:, 1:2] * buf_ref[slot, 1]
    _post_pre_math(x_ref[...], f, gpost_ref, gpre_ref, mpost_ref, mpre_ref, xo_ref, h_ref, 1)


def _combine_post_pre(x, ys, pos0, pos1, ew, norm_g, mods, layer, n_ctx_tiles, h_dtype, tile0):
    m, d = x.shape
    pre_layer = min(layer + 1, norm_g.shape[0] - 1)
    with_pre = h_dtype is not None
    row_in = pl.BlockSpec((TM, d), lambda i, a, b: (i + tile0, 0))
    row_out = pl.BlockSpec((TM, d), lambda i, a, b: (i, 0))
    m_out = m - tile0 * TM

    def mod_spec(l):
        return pl.BlockSpec((None, None, 6, d), lambda i, a, b: (l, jnp.where(i + tile0 < n_ctx_tiles, 1, 0), 0, 0))

    out_shape = [jax.ShapeDtypeStruct((m_out, d), F32)]
    out_specs = [row_out]
    if with_pre:
        out_shape.append(jax.ShapeDtypeStruct((m_out, d), h_dtype))
        out_specs.append(row_out)
    grid_spec = pltpu.PrefetchScalarGridSpec(
        num_scalar_prefetch=2,
        grid=(m_out // TM,),
        in_specs=[pl.BlockSpec(memory_space=pl.ANY),
                  pl.BlockSpec((TM, HEAD_DIM), lambda i, a, b: (i + tile0, 0)), row_in,
                  pl.BlockSpec((None, 4, d), lambda i, a, b: (layer, 0, 0)),
                  pl.BlockSpec((None, 4, d), lambda i, a, b: (pre_layer, 0, 0)),
                  mod_spec(layer), mod_spec(pre_layer)],
        out_specs=out_specs,
        scratch_shapes=[pltpu.VMEM((2, 2, TM, d), F32), pltpu.SemaphoreType.DMA((2,))],
    )
    res = pl.pallas_call(
        functools.partial(_combine_kernel, with_pre=with_pre, tile0=tile0),
        grid_spec=grid_spec,
        out_shape=out_shape,
        compiler_params=_params(1),
        name="moe_combine",
    )(pos0, pos1, ys, ew, x, norm_g, norm_g, mods, mods)
    return res if with_pre else (res[0], None)


def _rope_tables(n_ctx, s):
    pos = jnp.arange(s)
    row = (pos // GRID_W).astype(F32)
    col = (pos % GRID_W).astype(F32)
    inv = ROPE_THETA ** (-jnp.arange(ROPE_FREQS, dtype=F32) / ROPE_FREQS)
    ar, ac = row[:, None] * inv, col[:, None] * inv
    cos = jnp.concatenate([jnp.cos(ar), jnp.cos(ar), jnp.cos(ac), jnp.cos(ac)], axis=-1)
    sin = jnp.concatenate([-jnp.sin(ar), jnp.sin(ar), -jnp.sin(ac), jnp.sin(ac)], axis=-1)
    cos = jnp.concatenate([jnp.ones((n_ctx, HEAD_DIM), F32), cos], axis=0)
    sin = jnp.concatenate([jnp.zeros((n_ctx, HEAD_DIM), F32), sin], axis=0)
    return cos, sin


def _layer_ab(h, j, ab_w_in, a_sink, b_ig_bias, b_fg_bias, b_norm_g, cos_t, sin_t, n_ctx):
    scale = HEAD_DIM ** -0.5 * LOG2E
    no_g = jnp.ones((2, HEAD_DIM), F32)
    a_kinds = [("rope", 0, scale)] * A_HEADS + [("rope", 0, 1.0)] * A_KV + [PLAIN] * A_KV
    a_qkv = _proj([h], ab_w_in, j, 0, a_kinds, 512, BF16, cos_t, sin_t, no_g)
    b_qkvo = _proj([h], ab_w_in, j, A_QKV, [PLAIN] * (B_QKVO // HEAD_DIM), 512, F32, cos_t, sin_t, no_g)
    ngate = 4 * B_HEADS
    w_gate = jnp.pad(ab_w_in[j, :, A_QKV + B_QKVO:], ((0, 0), (0, HEAD_DIM - ngate)))[None]
    gates = _proj([h], w_gate, 0, 0, [PLAIN], HEAD_DIM, F32, cos_t, sin_t, no_g)

    ya = _window_attn(a_qkv, a_sink[j], n_ctx)

    bias = jnp.concatenate([b_ig_bias[j].reshape(-1), b_fg_bias[j].reshape(-1)]).astype(F32)
    bias_row = jnp.pad(bias, (0, HEAD_DIM - ngate))[None, :]
    hf, hb = _mlstm(b_qkvo, gates, jnp.transpose(gates[:, :ngate]), bias_row, bias[:, None], n_ctx)
    yb = _b_out(hf, hb, b_qkvo, b_norm_g[j][None, :])
    return [ya, yb]


def _layer_c(h, j, c_w_qkv, c_qk_g, cos_t, sin_t, n_ctx):
    scale = HEAD_DIM ** -0.5 * LOG2E
    c_kinds = [("norm_rope", 0, scale)] * C_HEADS + [("norm_rope", 1, 1.0)] * C_KV + [PLAIN] * C_KV
    qkv = _proj([h], c_w_qkv, j, 0, c_kinds, 512, BF16, cos_t, sin_t, c_qk_g[j])
    m = h.shape[0]
    tk = next(t for t in (1408, 768, TM) if m % t == 0)
    return [_flash_attn(qkv, 0, C_Q, C_Q + C_KVW, C_KV, C_HEADS // C_KV, n_ctx, TM, tk)]


def kernel(x, c, ctx, c_ctx, ada_w, ada_b, norm_g, ab_w_in, ab_w_out, a_sink, b_ig_bias, b_fg_bias, b_norm_g,
           c_w_qkv, c_w_out, c_qk_g, ffn_w_gate, ffn_w_up, ffn_w_down, moe_router, moe_w_gate, moe_w_up,
           moe_w_down):
    depth = ada_w.shape[0]
    s, d = x.shape[1], x.shape[2]
    n_ctx = ctx.shape[1]
    m = n_ctx + s
    assert x.shape[0] == 1 and n_ctx % TM == 0 and s % TM == 0
    n_ctx_tiles = n_ctx // TM
    n_tiles = m // TM
    no_g = jnp.ones((2, HEAD_DIM), F32)

    mods = _ada_mods(c, c_ctx, ada_w, ada_b)
    cos_t, sin_t = _rope_tables(n_ctx, s)
    xs = jnp.concatenate([ctx[0], x[0]], axis=0)
    h = _prenorm(xs, norm_g, mods, 0, n_ctx_tiles)
    dense_groups, dense_sub = _dense_groups(m)
    n_exp = moe_w_gate.shape[1]
    moe_g = moe_w_gate.reshape((-1,) + moe_w_gate.shape[2:])
    moe_u = moe_w_up.reshape((-1,) + moe_w_up.shape[2:])
    moe_d = moe_w_down.reshape((-1,) + moe_w_down.shape[2:])

    for layer in range(depth):
        last = layer == depth - 1
        j = layer // 2
        even = layer % 2 == 0
        if even:
            mix = _layer_ab(h, j, ab_w_in, a_sink, b_ig_bias, b_fg_bias, b_norm_g, cos_t, sin_t, n_ctx)
            w_out = ab_w_out
        else:
            mix = _layer_c(h, j, c_w_qkv, c_qk_g, cos_t, sin_t, n_ctx)
            w_out = c_w_out
        y = _proj(mix, w_out, j, 0, [PLAIN] * (d // HEAD_DIM), 512 if d % 512 == 0 else d, F32, cos_t, sin_t, no_g)
        xs, h = _post_pre(xs, y, norm_g, mods, layer, 0, n_ctx_tiles, BF16 if even else F32)
        if even:
            f_dim = ffn_w_gate.shape[-1]
            f = _ffn(h, ffn_w_gate, ffn_w_up, ffn_w_down, j, *dense_groups, dense_sub, FFN_SLICE)
            xs, h = _post_pre(xs, f, norm_g, mods, layer, 1, n_ctx_tiles, None if last else BF16)
        else:
            router_p = jnp.pad(moe_router[j], ((0, 0), (0, HEAD_DIM - n_exp)))
            eidx, ew = _router(h, router_p)
            n_sub = MOE_GROUP_TILES
            n_groups = (2 * m + n_exp * (n_sub * TM - 1)) // (n_sub * TM)
            src, groups, tile_valid, pos0, pos1 = _moe_plan(eidx[:, :2], ew[:, :2], n_groups, n_sub)
            x_s = _gather_rows(h, src, tile_valid)
            y_s = _ffn(x_s, moe_g, moe_u, moe_d, j * n_exp, *groups, n_sub, FFN_SLICE)
            xs, h = _combine_post_pre(xs, y_s, pos0, pos1, ew, norm_g, mods, layer, n_ctx_tiles,
                                      None if last else BF16, n_ctx_tiles if last else 0)
    return (xs if xs.shape[0] == s else xs[n_ctx:])[None]
```

```python
import functools

import jax
import jax.numpy as jnp
import numpy as np
from jax import lax
from jax.experimental import pallas as pl
from jax.experimental.pallas import tpu as pltpu

F32 = jnp.float32
BF16 = jnp.bfloat16
I32 = jnp.int32

EPS = 1e-6
GRID_W = 64
HEAD_DIM = 128
ROPE_THETA = 10000.0
ROPE_FREQS = HEAD_DIM // 4
BLK = 128
A_HEADS = 8
A_KV = 2
B_HEADS = 4
B_DK = 128
B_DV = 256
MLSTM_CHUNK = 128
C_HEADS = 16
C_KV = 4
N_EXPERTS = 8

A_Q = A_HEADS * HEAD_DIM
A_QKV = A_Q + 2 * A_KV * HEAD_DIM
B_Q = B_HEADS * B_DK
B_V = B_HEADS * B_DV
B_QKVO = 2 * B_Q + 2 * B_V
C_Q = C_HEADS * HEAD_DIM
C_KVW = C_KV * HEAD_DIM

TM = 256
MOE_GROUP_TILES = 9
PROJ_BLOCKS = ((1408, 352), (768, 384))
FFN_UNIT = 128
FFN_SLICE = 256
VMEM_LIMIT = 56 * 1024 * 1024
NEG_INF = float("-inf")
LOG2E = 1.4426950408889634


def _row_block(m):
    return next(r * TM for r in (3, 2, 1) if m % (r * TM) == 0)


def _params(n_axes, vmem=VMEM_LIMIT):
    return pltpu.CompilerParams(dimension_semantics=("arbitrary",) * n_axes, vmem_limit_bytes=vmem)


def _rms(x, g):
    return x * lax.rsqrt(jnp.mean(x * x, axis=-1, keepdims=True) + EPS) * g


def _dot(a, b):
    return jnp.dot(a, b, preferred_element_type=F32)


def _dot_nt(a, b):
    return lax.dot_general(a, b, (((1,), (1,)), ((), ())), preferred_element_type=F32)


def _ada_kernel(s_ref, w_ref, b_ref, o_ref):
    s = s_ref[...]
    s = s * jax.nn.sigmoid(s)
    o_ref[...] = _dot(s.astype(BF16), w_ref[...].astype(BF16)) + b_ref[...]


def _ada_mods(c, c_ctx, ada_w, ada_b):
    depth, d, d6 = ada_w.shape
    tn = 1536 if d6 % 1536 == 0 else d6
    s = jnp.zeros((8, d), F32).at[0].set(c[0]).at[1].set(c_ctx)
    out = pl.pallas_call(
        _ada_kernel,
        grid=(depth, d6 // tn),
        in_specs=[pl.BlockSpec((8, d), lambda l, n: (0, 0)),
                  pl.BlockSpec((None, d, tn), lambda l, n: (l, 0, n)),
                  pl.BlockSpec((None, 1, tn), lambda l, n: (l, 0, n))],
        out_specs=pl.BlockSpec((None, 8, tn), lambda l, n: (l, 0, n)),
        out_shape=jax.ShapeDtypeStruct((depth, 8, d6), F32),
        compiler_params=_params(2),
        name="ada_mods",
    )(s, ada_w, ada_b.reshape(depth, 1, d6))
    return out[:, :2].reshape(depth, 2, 6, d)


def _mod_spec(d, layer, n_ctx_tiles):
    return pl.BlockSpec((None, None, 6, d), lambda i: (layer, jnp.where(i < n_ctx_tiles, 1, 0), 0, 0))


def _prenorm_kernel(x_ref, g_ref, mod_ref, h_ref):
    hn = _rms(x_ref[...], g_ref[0:1, :])
    h_ref[...] = (hn * (1.0 + mod_ref[1:2, :]) + mod_ref[0:1, :]).astype(h_ref.dtype)


def _prenorm(x, norm_g, mods, layer, n_ctx_tiles):
    m, d = x.shape
    return pl.pallas_call(
        _prenorm_kernel,
        grid=(m // TM,),
        in_specs=[pl.BlockSpec((TM, d), lambda i: (i, 0)),
                  pl.BlockSpec((None, 4, d), lambda i: (layer, 0, 0)),
                  _mod_spec(d, layer, n_ctx_tiles)],
        out_specs=pl.BlockSpec((TM, d), lambda i: (i, 0)),
        out_shape=jax.ShapeDtypeStruct((m, d), BF16),
        compiler_params=_params(1),
        name="prenorm",
    )(x, norm_g, mods)


def _post_pre_math(x, y, gpost_ref, gpre_ref, mpost_ref, mpre_ref, xo_ref, h_ref, sub):
    g_row = 1 + 2 * sub
    gate_row = 2 + 3 * sub
    xn = x + mpost_ref[gate_row:gate_row + 1, :] * _rms(y, gpost_ref[g_row:g_row + 1, :])
    xo_ref[...] = xn
    if h_ref is None:
        return None
    nsub = 1 - sub
    hn = _rms(xn, gpre_ref[2 * nsub:2 * nsub + 1, :])
    h = hn * (1.0 + mpre_ref[3 * nsub + 1:3 * nsub + 2, :]) + mpre_ref[3 * nsub:3 * nsub + 1, :]
    h_ref[...] = h.astype(h_ref.dtype)
    return h


def _post_pre_kernel(x_ref, y_ref, gpost_ref, gpre_ref, mpost_ref, mpre_ref, *rest, sub, with_router):
    if with_router:
        r_ref, xo_ref, h_ref, idx_ref, ew_ref = rest
    else:
        xo_ref, h_ref = rest[0], (rest[1] if len(rest) > 1 else None)
    h = _post_pre_math(x_ref[...], y_ref[...], gpost_ref, gpre_ref, mpost_ref, mpre_ref, xo_ref, h_ref, sub)
    if with_router:
        _route_top2(h, r_ref, idx_ref, ew_ref)


def _post_pre(x, y, norm_g, mods, layer, sub, n_ctx_tiles, h_dtype, router_p=None):
    m, d = x.shape
    pre_layer = layer if sub == 0 else min(layer + 1, norm_g.shape[0] - 1)
    with_router = router_p is not None
    row = pl.BlockSpec((TM, d), lambda i: (i, 0))
    lanes = pl.BlockSpec((TM, HEAD_DIM), lambda i: (i, 0))
    in_specs = [row, row,
                pl.BlockSpec((None, 4, d), lambda i: (layer, 0, 0)),
                pl.BlockSpec((None, 4, d), lambda i: (pre_layer, 0, 0)),
                _mod_spec(d, layer, n_ctx_tiles),
                _mod_spec(d, pre_layer, n_ctx_tiles)]
    args = [x, y, norm_g, norm_g, mods, mods]
    out_shape = [jax.ShapeDtypeStruct((m, d), F32)]
    out_specs = [row]
    if h_dtype is not None:
        out_shape.append(jax.ShapeDtypeStruct((m, d), h_dtype))
        out_specs.append(row)
    if with_router:
        in_specs.append(pl.BlockSpec((d, HEAD_DIM), lambda i: (0, 0)))
        args.append(router_p)
        out_shape += [jax.ShapeDtypeStruct((m, HEAD_DIM), I32), jax.ShapeDtypeStruct((m, HEAD_DIM), F32)]
        out_specs += [lanes, lanes]
    res = pl.pallas_call(
        functools.partial(_post_pre_kernel, sub=sub, with_router=with_router),
        grid=(m // TM,),
        in_specs=in_specs,
        out_specs=out_specs,
        out_shape=out_shape,
        compiler_params=_params(1),
        name="post_pre",
    )(*args)
    if with_router:
        return res[0], res[1], res[2], res[3]
    return res if h_dtype is not None else (res[0], None)


def _rope(y, cos, sin):
    lane = lax.broadcasted_iota(I32, y.shape, 1)
    lower = (lane & (2 * ROPE_FREQS - 1)) < ROPE_FREQS
    partner = jnp.where(lower, pltpu.roll(y, HEAD_DIM - ROPE_FREQS, 1), pltpu.roll(y, ROPE_FREQS, 1))
    return y * cos + partner * sin


PLAIN = ("plain", 0, 1.0)


def _proj_kernel(*refs, tile_kinds, rows, k_split):
    xs_refs = refs[:len(k_split)]
    w_ref, cos_ref, sin_ref, g_ref, o_ref, wb_ref = refs[len(k_split):]
    n = pl.program_id(1)

    @pl.when(pl.program_id(0) == 0)
    def _():
        wb_ref[n] = w_ref[...].astype(BF16)

    def matmul(rs):
        k0, y = 0, None
        for x_ref, kk in zip(xs_refs, k_split):
            part = _dot(x_ref[rs, :], wb_ref[n, k0:k0 + kk, :])
            y = part if y is None else y + part
            k0 += kk
        return y

    def run(kinds):
        for c in range(o_ref.shape[0] // rows):
            rs = slice(c * rows, (c + 1) * rows)
            y = matmul(rs)
            if all(kd == PLAIN for kd in kinds):
                o_ref[rs, :] = y.astype(o_ref.dtype)
                continue
            for hh, (kind, grow, scale) in enumerate(kinds):
                sl = slice(hh * HEAD_DIM, (hh + 1) * HEAD_DIM)
                yh = y[:, sl]
                if kind == "norm_rope":
                    yh = _rms(yh, g_ref[grow:grow + 1, :])
                if kind != "plain":
                    yh = _rope(yh, cos_ref[rs, :], sin_ref[rs, :])
                if scale != 1.0:
                    yh = yh * scale
                o_ref[rs, sl] = yh.astype(o_ref.dtype)

    t0 = 0
    while t0 < len(tile_kinds):
        t1 = t0 + 1
        while t1 < len(tile_kinds) and tile_kinds[t1] == tile_kinds[t0]:
            t1 += 1
        pl.when((n >= t0) & (n < t1))(functools.partial(run, tile_kinds[t0]))
        t0 = t1


def _proj(hs, w3, layer_idx, col0, head_kinds, tn, out_dtype, cos_t, sin_t, gains):
    m = hs[0].shape[0]
    k_split = tuple(h.shape[1] for h in hs)
    k = sum(k_split)
    ncols = HEAD_DIM * len(head_kinds)
    assert col0 % tn == 0 and ncols % tn == 0 and w3.shape[1] == k
    n0 = col0 // tn
    nt = ncols // tn
    hpt = tn // HEAD_DIM
    tile_kinds = tuple(tuple(head_kinds[t * hpt:(t + 1) * hpt]) for t in range(nt))
    tm, rows = next(((a, b) for a, b in PROJ_BLOCKS if m % a == 0), (TM, TM))

    def w_tile(i, n):
        return (layer_idx, 0, n0 + jnp.where(i == 0, n, nt - 1))

    return pl.pallas_call(
        functools.partial(_proj_kernel, tile_kinds=tile_kinds, rows=rows, k_split=k_split),
        grid=(m // tm, nt),
        in_specs=[pl.BlockSpec((tm, kk), lambda i, n: (i, 0)) for kk in k_split]
                 + [pl.BlockSpec((None, k, tn), w_tile),
                    pl.BlockSpec((tm, HEAD_DIM), lambda i, n: (i, 0)),
                    pl.BlockSpec((tm, HEAD_DIM), lambda i, n: (i, 0)),
                    pl.BlockSpec((2, HEAD_DIM), lambda i, n: (0, 0))],
        out_specs=pl.BlockSpec((tm, tn), lambda i, n: (i, n)),
        out_shape=jax.ShapeDtypeStruct((m, ncols), out_dtype),
        scratch_shapes=[pltpu.VMEM((nt, k, tn), BF16)],
        compiler_params=_params(2),
        name="proj",
    )(*hs, w3, cos_t, sin_t, gains)


WIN_QB = 2


def _win_kernel(sink_ref, q_ref, kp_ref, kc_ref, kn_ref, kx_ref, vp_ref, vc_ref, vn_ref, vx_ref, o_ref, *, group,
                ctx_steps):
    kvh = pl.program_id(0)
    i = pl.program_id(1)
    nq = WIN_QB * BLK
    nk = (WIN_QB + 2) * BLK
    qi = lax.broadcasted_iota(I32, (group * nq, nk), 0) & (nq - 1)
    kj = lax.broadcasted_iota(I32, (group * nq, nk), 1)
    rel = kj - BLK - qi
    lo = jnp.where(i < ctx_steps, nk, jnp.where(i > ctx_steps, 0, BLK))
    hi = jnp.where(i < ctx_steps, 0, jnp.where(i < pl.num_programs(1) - 1, nk, nk - BLK))
    mask = (rel >= -BLK) & (rel <= BLK) & (kj >= lo) & (kj < hi)
    k_win = jnp.concatenate([kp_ref[...], kc_ref[...], kn_ref[...]], axis=0)
    v_win = jnp.concatenate([vp_ref[...], vc_ref[...], vn_ref[...]], axis=0)
    q = jnp.concatenate([q_ref[:, g * HEAD_DIM:(g + 1) * HEAD_DIM] for g in range(group)], axis=0)
    snk = jnp.concatenate([jnp.full((nq, 1), sink_ref[kvh * group + g] * LOG2E, F32) for g in range(group)], axis=0)
    s_loc = jnp.where(mask, _dot_nt(q, k_win), NEG_INF)
    s_ctx = _dot_nt(q, kx_ref[...])
    mx = jnp.maximum(jnp.maximum(jnp.max(s_loc, -1, keepdims=True), jnp.max(s_ctx, -1, keepdims=True)), snk)
    p_loc = jnp.exp2(s_loc - mx)
    p_ctx = jnp.exp2(s_ctx - mx)
    den = jnp.sum(p_loc, -1, keepdims=True) + jnp.sum(p_ctx, -1, keepdims=True) + jnp.exp2(snk - mx)
    o = (_dot(p_loc.astype(BF16), v_win) + _dot(p_ctx.astype(BF16), vx_ref[...])) / den
    for g in range(group):
        o_ref[:, g * HEAD_DIM:(g + 1) * HEAD_DIM] = o[g * nq:(g + 1) * nq, :].astype(o_ref.dtype)


def _window_attn(qkv, sink, n_ctx):
    m = qkv.shape[0]
    nb = (m - n_ctx) // BLK
    cb = n_ctx // BLK
    assert nb % WIN_QB == 0 and cb % WIN_QB == 0
    cq = cb // WIN_QB
    group = A_HEADS // A_KV
    kcol = A_Q // HEAD_DIM
    vcol = kcol + A_KV

    def edge_spec(col, block):
        return pl.BlockSpec((BLK, HEAD_DIM),
                            lambda h, i, s: (cb + jnp.clip((i - cq) * WIN_QB + block, 0, nb - 1), col + h))

    def own_spec(col):
        return pl.BlockSpec((WIN_QB * BLK, HEAD_DIM), lambda h, i, s: (i, col + h))

    def ctx_spec(col):
        return pl.BlockSpec((n_ctx, HEAD_DIM), lambda h, i, s: (0, col + h))

    grid_spec = pltpu.PrefetchScalarGridSpec(
        num_scalar_prefetch=1,
        grid=(A_KV, cq + nb // WIN_QB),
        in_specs=[pl.BlockSpec((WIN_QB * BLK, group * HEAD_DIM), lambda h, i, s: (i, h)),
                  edge_spec(kcol, -1), own_spec(kcol), edge_spec(kcol, WIN_QB), ctx_spec(kcol),
                  edge_spec(vcol, -1), own_spec(vcol), edge_spec(vcol, WIN_QB), ctx_spec(vcol)],
        out_specs=pl.BlockSpec((WIN_QB * BLK, group * HEAD_DIM), lambda h, i, s: (i, h)),
    )
    return pl.pallas_call(
        functools.partial(_win_kernel, group=group, ctx_steps=cq),
        grid_spec=grid_spec,
        out_shape=jax.ShapeDtypeStruct((m, A_Q), BF16),
        compiler_params=_params(2),
        name="window_attn",
    )(sink, qkv, qkv, qkv, qkv, qkv, qkv, qkv, qkv, qkv)


def _flash_kernel(q_ref, k_ref, v_ref, o_ref, va_ref, m_ref, acc_ref, *, group, tq, tk, n_ctx):
    rows = group * tq
    n_keys = k_ref.shape[0]

    @pl.when(pl.program_id(1) == 0)
    def _():
        va_ref[:, :HEAD_DIM] = v_ref[...]
        va_ref[:, HEAD_DIM:] = jnp.ones((n_keys, HEAD_DIM), BF16)

    def attend(limit, step):
        q = jnp.concatenate([q_ref[:, g * HEAD_DIM:(g + 1) * HEAD_DIM] for g in range(group)], axis=0)
        m_ref[...] = jnp.full((rows, HEAD_DIM), NEG_INF, F32)
        acc_ref[...] = jnp.zeros((rows, 2 * HEAD_DIM), F32)
        for j in range(limit // step):
            ks = slice(j * step, (j + 1) * step)
            s = _dot_nt(q, k_ref[ks, :])
            cols = [s[:, c * HEAD_DIM:(c + 1) * HEAD_DIM] for c in range(step // HEAD_DIM)]
            m_prev = m_ref[...]
            m_new = jnp.maximum(m_prev, jnp.max(functools.reduce(jnp.maximum, cols), -1, keepdims=True))
            alpha = jnp.exp2(m_prev - m_new)
            p = jnp.concatenate([jnp.exp2(c - m_new).astype(BF16) for c in cols], axis=1)
            acc_ref[...] = jnp.concatenate([alpha, alpha], axis=1) * acc_ref[...] + _dot(p, va_ref[ks, :])
            m_ref[...] = m_new
        for g in range(group):
            rs = slice(g * tq, (g + 1) * tq)
            o_ref[:, g * HEAD_DIM:(g + 1) * HEAD_DIM] = (acc_ref[rs, :HEAD_DIM] / acc_ref[rs, HEAD_DIM:]).astype(o_ref.dtype)

    ctx_tiles = n_ctx // tq
    pl.when(pl.program_id(1) < ctx_tiles)(functools.partial(attend, n_ctx, n_ctx))
    pl.when(pl.program_id(1) >= ctx_tiles)(functools.partial(attend, n_keys, tk))


def _flash_attn(qkv, q_col0, k_col0, v_col0, n_kv, group, n_ctx, tq, tk):
    m = qkv.shape[0]
    assert m % tq == 0 and n_ctx % tq == 0 and m % tk == 0
    qc0 = q_col0 // (group * HEAD_DIM)
    kc0 = k_col0 // HEAD_DIM
    vc0 = v_col0 // HEAD_DIM
    rows = group * tq
    return pl.pallas_call(
        functools.partial(_flash_kernel, group=group, tq=tq, tk=tk, n_ctx=n_ctx),
        grid=(n_kv, m // tq),
        in_specs=[pl.BlockSpec((tq, group * HEAD_DIM), lambda h, i: (i, qc0 + h)),
                  pl.BlockSpec((m, HEAD_DIM), lambda h, i: (0, kc0 + h)),
                  pl.BlockSpec((m, HEAD_DIM), lambda h, i: (0, vc0 + h))],
        out_specs=pl.BlockSpec((tq, group * HEAD_DIM), lambda h, i: (i, h)),
        out_shape=jax.ShapeDtypeStruct((m, n_kv * group * HEAD_DIM), BF16),
        scratch_shapes=[pltpu.VMEM((m, 2 * HEAD_DIM), BF16), pltpu.VMEM((rows, HEAD_DIM), F32),
                        pltpu.VMEM((rows, 2 * HEAD_DIM), F32)],
        compiler_params=_params(2),
        name="flash_attn",
    )(qkv, qkv, qkv)


def _log_sigmoid(x):
    return jnp.minimum(x, 0.0) - jnp.log(1.0 + jnp.exp(-jnp.abs(x)))


def _dot_exact(a, b):
    return jnp.dot(a, b, preferred_element_type=F32, precision=lax.Precision.HIGHEST)


def _mlstm_head(q, k, v, i_col, i_row, b_col, b_row, b_tot, mask, c_ref, n_ref, m_ref):
    m_prev = m_ref[:, 0:1]
    c_prev = c_ref[...]
    n_prev = n_ref[...]
    qs = q * (B_DK ** -0.5)
    qb = qs.astype(BF16)
    dlog = jnp.where(mask, b_col - b_row + i_row, NEG_INF)
    inter = b_col + m_prev
    m_t = jnp.maximum(inter, jnp.max(dlog, -1, keepdims=True))
    dw = jnp.exp(dlog - m_t)
    iw = jnp.exp(inter - m_t)
    sc = _dot_nt(qb, k.astype(BF16)) * dw
    num = _dot(sc.astype(BF16), v.astype(BF16)) + iw * _dot(qb, c_prev.astype(BF16))
    den = jnp.sum(sc, -1, keepdims=True) + iw * jnp.sum(qs * n_prev, -1, keepdims=True)
    h = num / jnp.maximum(jnp.abs(den), jnp.exp(-m_t))
    glog_col = b_tot - b_col + i_col
    glog_row = b_tot - b_row + i_row
    m_new = jnp.maximum(b_tot + m_prev, jnp.max(glog_row, -1, keepdims=True))
    decay = jnp.exp(b_tot + m_prev - m_new)
    wk = jnp.exp(glog_col - m_new) * k
    c_ref[...] = decay * c_prev + lax.dot_general(wk.astype(BF16), v.astype(BF16), (((0,), (0,)), ((), ())),
                                                  preferred_element_type=F32)
    n_ref[...] = decay * n_prev + jnp.sum(wk, 0, keepdims=True)
    m_ref[...] = jnp.broadcast_to(m_new, m_ref.shape)
    return h


def _mlstm_kernel(brow_ref, bcol_ref,
                  qf_ref, kf_ref, vf_ref, gf_ref, gtf_ref,
                  qb_ref, kb_ref, vb_ref, gb_ref, gtb_ref,
                  of_ref, ob_ref, c_ref, n_ref, m_ref):
    L = MLSTM_CHUNK

    @pl.when(pl.program_id(0) == 0)
    def _():
        c_ref[...] = jnp.zeros(c_ref.shape, F32)
        n_ref[...] = jnp.zeros(n_ref.shape, F32)
        m_ref[...] = jnp.zeros(m_ref.shape, F32)

    r = lax.broadcasted_iota(I32, (L, L), 0)
    cidx = lax.broadcasted_iota(I32, (L, L), 1)
    lane = lax.broadcasted_iota(I32, (L, HEAD_DIM), 1)
    sub = lax.broadcasted_iota(I32, (2 * 2 * B_HEADS, L), 0)
    for d, (q_ref, k_ref, v_ref, g_ref, gt_ref, o_ref) in enumerate(
            ((qf_ref, kf_ref, vf_ref, gf_ref, gtf_ref, of_ref), (qb_ref, kb_ref, vb_ref, gb_ref, gtb_ref, ob_ref))):
        mask = (cidx <= r) if d == 0 else (cidx >= r)
        gc = g_ref[...] + brow_ref[...]
        gc = jnp.where(lane < 2 * B_HEADS, gc, _log_sigmoid(gc))
        gr = gt_ref[...] + bcol_ref[...]
        gr = jnp.where(sub < 2 * B_HEADS, gr, _log_sigmoid(gr))
        mask_t = (r <= cidx) if d == 0 else (r >= cidx)
        bc_all = _dot_exact(mask.astype(F32), gc)
        br_all = _dot_exact(gr, mask_t.astype(F32))
        edge = L - 1 if d == 0 else 0
        for hd in range(B_HEADS):
            ci = d * B_HEADS + hd
            cf = 2 * B_HEADS + ci
            b_col = bc_all[:, cf:cf + 1]
            h = _mlstm_head(q_ref[:, hd * B_DK:(hd + 1) * B_DK], k_ref[:, hd * B_DK:(hd + 1) * B_DK],
                            v_ref[:, hd * B_DV:(hd + 1) * B_DV],
                            gc[:, ci:ci + 1], gr[ci:ci + 1, :], b_col, br_all[cf:cf + 1, :],
                            b_col[edge:edge + 1, :], mask, c_ref.at[ci], n_ref.at[ci], m_ref.at[ci])
            o_ref[:, hd * B_DV:(hd + 1) * B_DV] = h


def _mlstm(qkvo, gates, gates_t, bias_row, bias_col, n_ctx):
    m = qkvo.shape[0]
    L = MLSTM_CHUNK
    nc = m // L
    ncc = n_ctx // L
    kq = B_Q // B_Q
    kv = (2 * B_Q) // B_V

    def fw(s):
        return s

    def bw(s):
        return jnp.where(s < ncc, ncc - 1 - s, nc - 1 + ncc - s)

    def specs(order):
        return [pl.BlockSpec((L, B_Q), lambda s: (order(s), 0)),
                pl.BlockSpec((L, B_Q), lambda s: (order(s), kq)),
                pl.BlockSpec((L, B_V), lambda s: (order(s), kv)),
                pl.BlockSpec((L, HEAD_DIM), lambda s: (order(s), 0)),
                pl.BlockSpec((4 * B_HEADS, L), lambda s: (0, order(s)))]

    nst = 2 * B_HEADS
    return pl.pallas_call(
        _mlstm_kernel,
        grid=(nc,),
        in_specs=[pl.BlockSpec((1, HEAD_DIM), lambda s: (0, 0)),
                  pl.BlockSpec((4 * B_HEADS, 1), lambda s: (0, 0))] + specs(fw) + specs(bw),
        out_specs=[pl.BlockSpec((L, B_V), lambda s: (fw(s), 0)),
                   pl.BlockSpec((L, B_V), lambda s: (bw(s), 0))],
        out_shape=[jax.ShapeDtypeStruct((m, B_V), F32), jax.ShapeDtypeStruct((m, B_V), F32)],
        scratch_shapes=[pltpu.VMEM((nst, B_DK, B_DV), F32), pltpu.VMEM((nst, 1, B_DK), F32),
                        pltpu.VMEM((nst, 1, HEAD_DIM), F32)],
        compiler_params=_params(1),
        name="mlstm",
    )(bias_row, bias_col, qkvo, qkvo, qkvo, gates, gates_t, qkvo, qkvo, qkvo, gates, gates_t)


def _bout_kernel(hf_ref, hb_ref, o_ref, g_ref, y_ref):
    for hd in range(B_HEADS):
        sl = slice(hd * B_DV, (hd + 1) * B_DV)
        hn = _rms(hf_ref[:, sl] + hb_ref[:, sl], g_ref[:, sl])
        y_ref[:, sl] = (jax.nn.sigmoid(o_ref[:, sl]) * hn).astype(y_ref.dtype)


def _b_out(hf, hb, qkvo, out_g):
    m = hf.shape[0]
    ocol = (2 * B_Q + B_V) // B_V
    row = pl.BlockSpec((TM, B_V), lambda i: (i, 0))
    return pl.pallas_call(
        _bout_kernel,
        grid=(m // TM,),
        in_specs=[row, row, pl.BlockSpec((TM, B_V), lambda i: (i, ocol)),
                  pl.BlockSpec((1, B_V), lambda i: (0, 0))],
        out_specs=row,
        out_shape=jax.ShapeDtypeStruct((m, B_V), BF16),
        compiler_params=_params(1),
        name="mlstm_out",
    )(hf, hb, qkvo, out_g)


def _ffn_kernel(ge_ref, gb_ref, gn_ref, x_ref, wg_ref, wu_ref, wd_ref, o_ref, wgb_ref, wub_ref, wdb_ref, *, n_sub):
    g = pl.program_id(0)
    j = pl.program_id(1)
    gn = gn_ref[g]
    big = 8 * FFN_UNIT
    n_big = lax.shift_right_logical(gn, 3)

    def cast_weights():
        wgb_ref[...] = wg_ref[...].astype(BF16)
        wub_ref[...] = wu_ref[...].astype(BF16)
        wdb_ref[...] = wd_ref[...].astype(BF16)

    def rows_step(off, rows, first):
        x = x_ref[pl.ds(off, rows), :]
        a = _dot(x, wgb_ref[...])
        b = _dot(x, wub_ref[...])
        y = _dot((a * jax.nn.sigmoid(a) * b).astype(BF16), wdb_ref[...])
        if first:
            o_ref[pl.ds(off, rows), :] = y
        else:
            o_ref[pl.ds(off, rows), :] += y

    def used_tiles(first):
        @pl.when(n_big > 0)
        def _():
            cast_weights()
            rows_step(0, big, first)

        @pl.when((n_big == 0) & (gn > 0))
        def _():
            cast_weights()

        def big_step(rq, c):
            rows_step(pl.multiple_of(rq * big, big), big, first)
            return c

        lax.fori_loop(1, n_big, big_step, 0)
        for units in (4, 2, 1):
            @pl.when((gn & units) == units)
            def _(units=units):
                rows = units * FFN_UNIT
                rows_step(pl.multiple_of((gn & ~(2 * units - 1)) * FFN_UNIT, rows), rows, first)

    @pl.when(j == 0)
    def _():
        used_tiles(True)

        def zero(r, c):
            off = pl.multiple_of(r * FFN_UNIT, FFN_UNIT)
            o_ref[pl.ds(off, FFN_UNIT), :] = jnp.zeros((FFN_UNIT, o_ref.shape[1]), F32)
            return c

        lax.fori_loop(gn, n_sub * (TM // FFN_UNIT), zero, 0)

    @pl.when(j > 0)
    def _():
        used_tiles(False)


def _ffn(x, wg, wu, wd, e0, grp_e, grp_b, grp_n, n_sub, tf):
    p, d = x.shape
    f = wg.shape[-1]
    rg = n_sub * TM
    assert f % tf == 0 and p % rg == 0
    once = pl.Buffered(1)
    nj = f // tf

    def slice_of(j, g, gn):
        return jnp.where(gn[g] > 0, j, nj - 1)

    grid_spec = pltpu.PrefetchScalarGridSpec(
        num_scalar_prefetch=3,
        grid=(grp_e.shape[0], nj),
        in_specs=[pl.BlockSpec((rg, d), lambda g, j, ge, gb, gn: (gb[g], 0), pipeline_mode=once),
                  pl.BlockSpec((None, d, tf), lambda g, j, ge, gb, gn: (e0 + ge[g], 0, slice_of(j, g, gn))),
                  pl.BlockSpec((None, d, tf), lambda g, j, ge, gb, gn: (e0 + ge[g], 0, slice_of(j, g, gn))),
                  pl.BlockSpec((None, tf, d), lambda g, j, ge, gb, gn: (e0 + ge[g], slice_of(j, g, gn), 0))],
        out_specs=pl.BlockSpec((rg, d), lambda g, j, ge, gb, gn: (g, 0), pipeline_mode=once),
        scratch_shapes=[pltpu.VMEM((d, tf), BF16), pltpu.VMEM((d, tf), BF16), pltpu.VMEM((tf, d), BF16)],
    )
    return pl.pallas_call(
        functools.partial(_ffn_kernel, n_sub=n_sub),
        grid_spec=grid_spec,
        out_shape=jax.ShapeDtypeStruct((p, d), F32),
        compiler_params=_params(2),
        name="ffn",
    )(grp_e, grp_b, grp_n, x, wg, wu, wd)


def _dense_groups(m):
    n_sub = next(r for r in (11, 3, 2, 1) if m % (r * TM) == 0)
    n_groups = m // (n_sub * TM)
    full = jnp.full((n_groups,), n_sub * (TM // FFN_UNIT), I32)
    return (jnp.zeros((n_groups,), I32), jnp.arange(n_groups, dtype=I32), full), n_sub


def _route_top2(h, r_ref, idx_ref, w_ref):
    logits = _dot(h.astype(BF16), r_ref[...].astype(BF16))
    lane = lax.broadcasted_iota(I32, logits.shape, 1).astype(F32)
    big = float(HEAD_DIM)
    lg = jnp.where(lane < N_EXPERTS, logits, NEG_INF)
    m1 = jnp.max(lg, -1, keepdims=True)
    i1 = jnp.min(jnp.where(lg == m1, lane, big), -1, keepdims=True)
    lg2 = jnp.where(lane == i1, NEG_INF, lg)
    m2 = jnp.max(lg2, -1, keepdims=True)
    i2 = jnp.min(jnp.where(lg2 == m2, lane, big), -1, keepdims=True)
    e2 = jnp.exp(m2 - m1)
    den = 1.0 + e2
    idx_ref[...] = jnp.where(lane == 0.0, i1, jnp.where(lane == 1.0, i2, 0.0)).astype(I32)
    w_ref[...] = jnp.where(lane == 0.0, 1.0 / den, jnp.where(lane == 1.0, e2 / den, 0.0))


def _moe_plan(eidx, ew, n_groups, n_sub):
    m = eidx.shape[0]
    rg = n_sub * TM
    e_flat = eidx.reshape(-1)
    experts = jnp.arange(N_EXPERTS, dtype=I32)
    onehot = (e_flat[:, None] == experts[None, :]).astype(I32)
    csum = jnp.cumsum(onehot, axis=0)
    rank = jnp.sum(csum * onehot, axis=1) - 1
    counts = csum[-1]
    groups_e = (counts + rg - 1) // rg
    grp_end = jnp.cumsum(groups_e)
    grp_start = grp_end - groups_e
    pos = jnp.sum(onehot * grp_start[None, :], axis=1) * rg + rank
    token = jnp.arange(2 * m, dtype=I32) // 2
    src = jnp.zeros((n_groups * rg,), I32).at[pos].set(token)
    gids = jnp.arange(n_groups, dtype=I32)
    valid = gids < grp_end[-1]
    grp_e = jnp.sum((gids[:, None] >= grp_end[None, :]).astype(I32), axis=1)
    grp_e = jnp.where(valid, grp_e, jnp.max(jnp.where(counts > 0, experts, 0)))
    grp_b = jnp.minimum(gids, grp_end[-1] - 1).astype(I32)
    mine = (grp_e[:, None] == experts[None, :]).astype(I32)
    rows_left = jnp.sum(mine * counts[None, :], axis=1) - (gids - jnp.sum(mine * grp_start[None, :], axis=1)) * rg
    upt = TM // FFN_UNIT
    grp_n = jnp.where(valid, jnp.clip((rows_left + FFN_UNIT - 1) // FFN_UNIT, 0, n_sub * upt), 0).astype(I32)
    tile_valid = ((jnp.arange(n_groups * n_sub, dtype=I32) % n_sub) * upt < jnp.repeat(grp_n, n_sub)).astype(I32)
    pos2 = pos.reshape(m, 2).astype(I32)
    return src, (grp_e.astype(I32), grp_b, grp_n), tile_valid, pos2[:, 0], pos2[:, 1]


SUBLANES = 8


def _gather_kernel(src_ref, tv_ref, h_hbm, o_ref, buf_ref, sem):
    t = pl.program_id(0)
    nt = pl.num_programs(0)

    def row_copy(row, slot, r8, s):
        return pltpu.make_async_copy(h_hbm.at[pl.ds(row, 1), :], buf_ref.at[slot, r8, pl.ds(s, 1), :], sem.at[slot])

    def fetch(tile):
        slot = tile % 2

        @pl.when(tv_ref[tile] == 1)
        def _():
            def issue(r8, c):
                for s in range(SUBLANES):
                    row_copy(src_ref[tile * TM + r8 * SUBLANES + s], slot, r8, s).start()
                return c

            lax.fori_loop(0, TM // SUBLANES, issue, 0)

    @pl.when(t == 0)
    def _():
        fetch(t)

    @pl.when(t + 1 < nt)
    def _():
        fetch(t + 1)

    @pl.when(tv_ref[t] == 1)
    def _():
        slot = t % 2

        def wait(r8, c):
            for s in range(SUBLANES):
                row_copy(0, slot, r8, s).wait()
            return c

        lax.fori_loop(0, TM // SUBLANES, wait, 0)
        o_ref[...] = buf_ref[slot].reshape(TM, o_ref.shape[1]).astype(o_ref.dtype)

    @pl.when(tv_ref[t] == 0)
    def _():
        o_ref[...] = jnp.zeros(o_ref.shape, o_ref.dtype)


def _gather_rows(h, src, tile_valid):
    d = h.shape[1]
    p = src.shape[0]
    grid_spec = pltpu.PrefetchScalarGridSpec(
        num_scalar_prefetch=2,
        grid=(p // TM,),
        in_specs=[pl.BlockSpec(memory_space=pl.ANY)],
        out_specs=pl.BlockSpec((TM, d), lambda t, s, tv: (t, 0)),
        scratch_shapes=[pltpu.VMEM((2, TM // SUBLANES, SUBLANES, d), F32), pltpu.SemaphoreType.DMA((2,))],
    )
    return pl.pallas_call(
        _gather_kernel,
        grid_spec=grid_spec,
        out_shape=jax.ShapeDtypeStruct((p, d), BF16),
        compiler_params=_params(1),
        name="moe_gather",
    )(src, tile_valid, h)


def _combine_kernel(p0_ref, p1_ref, ys_hbm, ew_ref, x_ref, gpost_ref, gpre_ref, mpost_ref, mpre_ref, xo_ref, *rest,
                    with_pre, tile0):
    if with_pre:
        h_ref, buf_ref, sem = rest
    else:
        h_ref = None
        buf_ref, sem = rest
    t = pl.program_id(0)
    nt = pl.num_programs(0)

    def row_copy(row, slot, k, r):
        return pltpu.make_async_copy(ys_hbm.at[pl.ds(row, 1), :], buf_ref.at[slot, k, pl.ds(r, 1), :], sem.at[slot])

    def fetch(tile):
        slot = tile % 2
        base = (tile + tile0) * TM

        def issue(r, c):
            row_copy(p0_ref[base + r], slot, 0, r).start()
            row_copy(p1_ref[base + r], slot, 1, r).start()
            return c

        lax.fori_loop(0, TM, issue, 0, unroll=SUBLANES)

    @pl.when(t == 0)
    def _():
        fetch(t)

    @pl.when(t + 1 < nt)
    def _():
        fetch(t + 1)

    slot = t % 2

    def wait(r, c):
        row_copy(0, slot, 0, r).wait()
        row_copy(0, slot, 1, r).wait()
        return c

    lax.fori_loop(0, TM, wait, 0, unroll=SUBLANES)
    f = ew_ref[:, 0:1] * buf_ref[slot, 0] + ew_ref[:, 1:2] * buf_ref[slot, 1]
    _post_pre_math(x_ref[...], f, gpost_ref, gpre_ref, mpost_ref, mpre_ref, xo_ref, h_ref, 1)


def _combine_post_pre(x, ys, pos0, pos1, ew, norm_g, mods, layer, n_ctx_tiles, h_dtype, tile0):
    m, d = x.shape
    pre_layer = min(layer + 1, norm_g.shape[0] - 1)
    with_pre = h_dtype is not None
    row_in = pl.BlockSpec((TM, d), lambda i, a, b: (i + tile0, 0))
    row_out = pl.BlockSpec((TM, d), lambda i, a, b: (i, 0))
    m_out = m - tile0 * TM

    def mod_spec(l):
        return pl.BlockSpec((None, None, 6, d), lambda i, a, b: (l, jnp.where(i + tile0 < n_ctx_tiles, 1, 0), 0, 0))

    out_shape = [jax.ShapeDtypeStruct((m_out, d), F32)]
    out_specs = [row_out]
    if with_pre:
        out_shape.append(jax.ShapeDtypeStruct((m_out, d), h_dtype))
        out_specs.append(row_out)
    grid_spec = pltpu.PrefetchScalarGridSpec(
        num_scalar_prefetch=2,
        grid=(m_out // TM,),
        in_specs=[pl.BlockSpec(memory_space=pl.ANY),
                  pl.BlockSpec((TM, HEAD_DIM), lambda i, a, b: (i + tile0, 0)), row_in,
                  pl.BlockSpec((None, 4, d), lambda i, a, b: (layer, 0, 0)),
                  pl.BlockSpec((None, 4, d), lambda i, a, b: (pre_layer, 0, 0)),
                  mod_spec(layer), mod_spec(pre_layer)],
        out_specs=out_specs,
        scratch_shapes=[pltpu.VMEM((2, 2, TM, d), F32), pltpu.SemaphoreType.DMA((2,))],
    )
    res = pl.pallas_call(
        functools.partial(_combine_kernel, with_pre=with_pre, tile0=tile0),
        grid_spec=grid_spec,
        out_shape=out_shape,
        compiler_params=_params(1),
        name="moe_combine",
    )(pos0, pos1, ys, ew, x, norm_g, norm_g, mods, mods)
    return res if with_pre else (res[0], None)


def _rope_tables(n_ctx, s):
    pos = jnp.arange(s)
    row = (pos // GRID_W).astype(F32)
    col = (pos % GRID_W).astype(F32)
    inv = ROPE_THETA ** (-jnp.arange(ROPE_FREQS, dtype=F32) / ROPE_FREQS)
    ar, ac = row[:, None] * inv, col[:, None] * inv
    cos = jnp.concatenate([jnp.cos(ar), jnp.cos(ar), jnp.cos(ac), jnp.cos(ac)], axis=-1)
    sin = jnp.concatenate([-jnp.sin(ar), jnp.sin(ar), -jnp.sin(ac), jnp.sin(ac)], axis=-1)
    cos = jnp.concatenate([jnp.ones((n_ctx, HEAD_DIM), F32), cos], axis=0)
    sin = jnp.concatenate([jnp.zeros((n_ctx, HEAD_DIM), F32), sin], axis=0)
    return cos, sin


def _layer_ab(h, j, ab_w_in, a_sink, b_ig_bias, b_fg_bias, b_norm_g, cos_t, sin_t, n_ctx):
    scale = HEAD_DIM ** -0.5 * LOG2E
    no_g = jnp.ones((2, HEAD_DIM), F32)
    a_kinds = [("rope", 0, scale)] * A_HEADS + [("rope", 0, 1.0)] * A_KV + [PLAIN] * A_KV
    a_qkv = _proj([h], ab_w_in, j, 0, a_kinds, 512, BF16, cos_t, sin_t, no_g)
    b_qkvo = _proj([h], ab_w_in, j, A_QKV, [PLAIN] * (B_QKVO // HEAD_DIM), 512, F32, cos_t, sin_t, no_g)
    ngate = 4 * B_HEADS
    w_gate = jnp.pad(ab_w_in[j, :, A_QKV + B_QKVO:], ((0, 0), (0, HEAD_DIM - ngate)))[None]
    gates = _proj([h], w_gate, 0, 0, [PLAIN], HEAD_DIM, F32, cos_t, sin_t, no_g)

    ya = _window_attn(a_qkv, a_sink[j], n_ctx)

    bias = jnp.concatenate([b_ig_bias[j].reshape(-1), b_fg_bias[j].reshape(-1)]).astype(F32)
    bias_row = jnp.pad(bias, (0, HEAD_DIM - ngate))[None, :]
    hf, hb = _mlstm(b_qkvo, gates, jnp.transpose(gates[:, :ngate]), bias_row, bias[:, None], n_ctx)
    yb = _b_out(hf, hb, b_qkvo, b_norm_g[j][None, :])
    return [ya, yb]


def _layer_c(h, j, c_w_qkv, c_qk_g, cos_t, sin_t, n_ctx):
    scale = HEAD_DIM ** -0.5 * LOG2E
    c_kinds = [("norm_rope", 0, scale)] * C_HEADS + [("norm_rope", 1, 1.0)] * C_KV + [PLAIN] * C_KV
    qkv = _proj([h], c_w_qkv, j, 0, c_kinds, 512, BF16, cos_t, sin_t, c_qk_g[j])
    m = h.shape[0]
    tk = next(t for t in (1408, 768, TM) if m % t == 0)
    return [_flash_attn(qkv, 0, C_Q, C_Q + C_KVW, C_KV, C_HEADS // C_KV, n_ctx, TM, tk)]


def kernel(x, c, ctx, c_ctx, ada_w, ada_b, norm_g, ab_w_in, ab_w_out, a_sink, b_ig_bias, b_fg_bias, b_norm_g,
           c_w_qkv, c_w_out, c_qk_g, ffn_w_gate, ffn_w_up, ffn_w_down, moe_router, moe_w_gate, moe_w_up,
           moe_w_down):
    depth = ada_w.shape[0]
    s, d = x.shape[1], x.shape[2]
    n_ctx = ctx.shape[1]
    m = n_ctx + s
    assert x.shape[0] == 1 and n_ctx % TM == 0 and s % TM == 0
    n_ctx_tiles = n_ctx // TM
    n_tiles = m // TM
    no_g = jnp.ones((2, HEAD_DIM), F32)

    mods = _ada_mods(c, c_ctx, ada_w, ada_b)
    cos_t, sin_t = _rope_tables(n_ctx, s)
    xs = jnp.concatenate([ctx[0], x[0]], axis=0)
    h = _prenorm(xs, norm_g, mods, 0, n_ctx_tiles)
    dense_groups, dense_sub = _dense_groups(m)
    n_exp = moe_w_gate.shape[1]
    moe_g = moe_w_gate.reshape((-1,) + moe_w_gate.shape[2:])
    moe_u = moe_w_up.reshape((-1,) + moe_w_up.shape[2:])
    moe_d = moe_w_down.reshape((-1,) + moe_w_down.shape[2:])

    for layer in range(depth):
        last = layer == depth - 1
        j = layer // 2
        even = layer % 2 == 0
        if even:
            mix = _layer_ab(h, j, ab_w_in, a_sink, b_ig_bias, b_fg_bias, b_norm_g, cos_t, sin_t, n_ctx)
            w_out = ab_w_out
        else:
            mix = _layer_c(h, j, c_w_qkv, c_qk_g, cos_t, sin_t, n_ctx)
            w_out = c_w_out
        y = _proj(mix, w_out, j, 0, [PLAIN] * (d // HEAD_DIM), 512 if d % 512 == 0 else d, F32, cos_t, sin_t, no_g)
        if even:
            xs, h = _post_pre(xs, y, norm_g, mods, layer, 0, n_ctx_tiles, BF16)
        else:
            router_p = jnp.pad(moe_router[j], ((0, 0), (0, HEAD_DIM - n_exp)))
            xs, h, eidx, ew = _post_pre(xs, y, norm_g, mods, layer, 0, n_ctx_tiles, F32, router_p)
        if even:
            f_dim = ffn_w_gate.shape[-1]
            f = _ffn(h, ffn_w_gate, ffn_w_up, ffn_w_down, j, *dense_groups, dense_sub, FFN_SLICE)
            xs, h = _post_pre(xs, f, norm_g, mods, layer, 1, n_ctx_tiles, None if last else BF16)
        else:
            n_sub = MOE_GROUP_TILES
            n_groups = (2 * m + n_exp * (n_sub * TM - 1)) // (n_sub * TM)
            src, groups, tile_valid, pos0, pos1 = _moe_plan(eidx[:, :2], ew[:, :2], n_groups, n_sub)
            x_s = _gather_rows(h, src, tile_valid)
            y_s = _ffn(x_s, moe_g, moe_u, moe_d, j * n_exp, *groups, n_sub, FFN_SLICE)
            xs, h = _combine_post_pre(xs, y_s, pos0, pos1, ew, norm_g, mods, layer, n_ctx_tiles,
                                      None if last else BF16, n_ctx_tiles if last else 0)
    return (xs if xs.shape[0] == s else xs[n_ctx:])[None]
```

```python
import functools

import jax
import jax.numpy as jnp
import numpy as np
from jax import lax
from jax.experimental import pallas as pl
from jax.experimental.pallas import tpu as pltpu

F32 = jnp.float32
BF16 = jnp.bfloat16
I32 = jnp.int32

EPS = 1e-6
GRID_W = 64
HEAD_DIM = 128
ROPE_THETA = 10000.0
ROPE_FREQS = HEAD_DIM // 4
BLK = 128
A_HEADS = 8
A_KV = 2
B_HEADS = 4
B_DK = 128
B_DV = 256
MLSTM_CHUNK = 128
MLSTM_STEP_CHUNKS = 2
C_HEADS = 16
C_KV = 4
N_EXPERTS = 8

A_Q = A_HEADS * HEAD_DIM
A_QKV = A_Q + 2 * A_KV * HEAD_DIM
B_Q = B_HEADS * B_DK
B_V = B_HEADS * B_DV
B_QKVO = 2 * B_Q + 2 * B_V
C_Q = C_HEADS * HEAD_DIM
C_KVW = C_KV * HEAD_DIM

TM = 256
MOE_GROUP_TILES = 9
PROJ_BLOCKS = ((1408, 352), (768, 384))
FFN_UNIT = 128
FFN_SLICE = 256
VMEM_LIMIT = 56 * 1024 * 1024
NEG_INF = float("-inf")
LOG2E = 1.4426950408889634


def _row_block(m):
    return next(r * TM for r in (3, 2, 1) if m % (r * TM) == 0)


def _params(n_axes, vmem=VMEM_LIMIT):
    return pltpu.CompilerParams(dimension_semantics=("arbitrary",) * n_axes, vmem_limit_bytes=vmem)


def _rms(x, g):
    return x * lax.rsqrt(jnp.mean(x * x, axis=-1, keepdims=True) + EPS) * g


def _dot(a, b):
    return jnp.dot(a, b, preferred_element_type=F32)


def _dot_nt(a, b):
    return lax.dot_general(a, b, (((1,), (1,)), ((), ())), preferred_element_type=F32)


def _ada_kernel(s_ref, w_ref, b_ref, o_ref):
    s = s_ref[...]
    s = s * jax.nn.sigmoid(s)
    o_ref[...] = _dot(s.astype(BF16), w_ref[...].astype(BF16)) + b_ref[...]


def _ada_mods(c, c_ctx, ada_w, ada_b):
    depth, d, d6 = ada_w.shape
    tn = 1536 if d6 % 1536 == 0 else d6
    s = jnp.zeros((8, d), F32).at[0].set(c[0]).at[1].set(c_ctx)
    out = pl.pallas_call(
        _ada_kernel,
        grid=(depth, d6 // tn),
        in_specs=[pl.BlockSpec((8, d), lambda l, n: (0, 0)),
                  pl.BlockSpec((None, d, tn), lambda l, n: (l, 0, n)),
                  pl.BlockSpec((None, 1, tn), lambda l, n: (l, 0, n))],
        out_specs=pl.BlockSpec((None, 8, tn), lambda l, n: (l, 0, n)),
        out_shape=jax.ShapeDtypeStruct((depth, 8, d6), F32),
        compiler_params=_params(2),
        name="ada_mods",
    )(s, ada_w, ada_b.reshape(depth, 1, d6))
    return out[:, :2].reshape(depth, 2, 6, d)


def _mod_spec(d, layer, n_ctx_tiles):
    return pl.BlockSpec((None, None, 6, d), lambda i: (layer, jnp.where(i < n_ctx_tiles, 1, 0), 0, 0))


def _prenorm_kernel(c_ref, x_ref, g_ref, mod_ref, xs_ref, h_ref, *, n_ctx_tiles):
    def emit(x):
        xs_ref[...] = x
        hn = _rms(x, g_ref[0:1, :])
        h_ref[...] = (hn * (1.0 + mod_ref[1:2, :]) + mod_ref[0:1, :]).astype(h_ref.dtype)

    pl.when(pl.program_id(0) < n_ctx_tiles)(lambda: emit(c_ref[...]))
    pl.when(pl.program_id(0) >= n_ctx_tiles)(lambda: emit(x_ref[...]))


def _prenorm(ctx2, x2, norm_g, mods, layer):
    (n_ctx, d), s = ctx2.shape, x2.shape[0]
    nct = n_ctx // TM
    row = pl.BlockSpec((TM, d), lambda i: (i, 0))
    return pl.pallas_call(
        functools.partial(_prenorm_kernel, n_ctx_tiles=nct),
        grid=((n_ctx + s) // TM,),
        in_specs=[pl.BlockSpec((TM, d), lambda i: (jnp.minimum(i, nct - 1), 0)),
                  pl.BlockSpec((TM, d), lambda i: (jnp.maximum(i - nct, 0), 0)),
                  pl.BlockSpec((None, 4, d), lambda i: (layer, 0, 0)),
                  _mod_spec(d, layer, nct)],
        out_specs=[row, row],
        out_shape=[jax.ShapeDtypeStruct((n_ctx + s, d), F32), jax.ShapeDtypeStruct((n_ctx + s, d), BF16)],
        compiler_params=_params(1),
        name="prenorm",
    )(ctx2, x2, norm_g, mods)


def _post_pre_math(x, y, gpost_ref, gpre_ref, mpost_ref, mpre_ref, xo_ref, h_ref, sub):
    g_row = 1 + 2 * sub
    gate_row = 2 + 3 * sub
    xn = x + mpost_ref[gate_row:gate_row + 1, :] * _rms(y, gpost_ref[g_row:g_row + 1, :])
    xo_ref[...] = xn
    if h_ref is None:
        return None
    nsub = 1 - sub
    hn = _rms(xn, gpre_ref[2 * nsub:2 * nsub + 1, :])
    h = hn * (1.0 + mpre_ref[3 * nsub + 1:3 * nsub + 2, :]) + mpre_ref[3 * nsub:3 * nsub + 1, :]
    h_ref[...] = h.astype(h_ref.dtype)
    return h


def _post_pre_kernel(x_ref, y_ref, gpost_ref, gpre_ref, mpost_ref, mpre_ref, *rest, sub, with_router):
    if with_router:
        r_ref, xo_ref, h_ref, idx_ref, ew_ref = rest
    else:
        xo_ref, h_ref = rest[0], (rest[1] if len(rest) > 1 else None)
    h = _post_pre_math(x_ref[...], y_ref[...], gpost_ref, gpre_ref, mpost_ref, mpre_ref, xo_ref, h_ref, sub)
    if with_router:
        _route_top2(h, r_ref, idx_ref, ew_ref)


def _post_pre(x, y, norm_g, mods, layer, sub, n_ctx_tiles, h_dtype, router_p=None):
    m, d = x.shape
    pre_layer = layer if sub == 0 else min(layer + 1, norm_g.shape[0] - 1)
    with_router = router_p is not None
    row = pl.BlockSpec((TM, d), lambda i: (i, 0))
    lanes = pl.BlockSpec((TM, HEAD_DIM), lambda i: (i, 0))
    in_specs = [row, row,
                pl.BlockSpec((None, 4, d), lambda i: (layer, 0, 0)),
                pl.BlockSpec((None, 4, d), lambda i: (pre_layer, 0, 0)),
                _mod_spec(d, layer, n_ctx_tiles),
                _mod_spec(d, pre_layer, n_ctx_tiles)]
    args = [x, y, norm_g, norm_g, mods, mods]
    out_shape = [jax.ShapeDtypeStruct((m, d), F32)]
    out_specs = [row]
    if h_dtype is not None:
        out_shape.append(jax.ShapeDtypeStruct((m, d), h_dtype))
        out_specs.append(row)
    if with_router:
        in_specs.append(pl.BlockSpec((d, HEAD_DIM), lambda i: (0, 0)))
        args.append(router_p)
        out_shape += [jax.ShapeDtypeStruct((m, HEAD_DIM), I32), jax.ShapeDtypeStruct((m, HEAD_DIM), F32)]
        out_specs += [lanes, lanes]
    res = pl.pallas_call(
        functools.partial(_post_pre_kernel, sub=sub, with_router=with_router),
        grid=(m // TM,),
        in_specs=in_specs,
        out_specs=out_specs,
        out_shape=out_shape,
        compiler_params=_params(1),
        name="post_pre",
    )(*args)
    if with_router:
        return res[0], res[1], res[2], res[3]
    return res if h_dtype is not None else (res[0], None)


def _rope(y, cos, sin):
    lane = lax.broadcasted_iota(I32, y.shape, 1)
    lower = (lane & (2 * ROPE_FREQS - 1)) < ROPE_FREQS
    partner = jnp.where(lower, pltpu.roll(y, HEAD_DIM - ROPE_FREQS, 1), pltpu.roll(y, ROPE_FREQS, 1))
    return y * cos + partner * sin


PLAIN = ("plain", 0, 1.0)


def _proj_kernel(*refs, tile_kinds, rows, k_split):
    xs_refs = refs[:len(k_split)]
    w_ref, cos_ref, sin_ref, g_ref, o_ref, wb_ref = refs[len(k_split):]
    n = pl.program_id(1)

    @pl.when(pl.program_id(0) == 0)
    def _():
        wb_ref[n] = w_ref[...].astype(BF16)

    def matmul(rs):
        k0, y = 0, None
        for x_ref, kk in zip(xs_refs, k_split):
            part = _dot(x_ref[rs, :], wb_ref[n, k0:k0 + kk, :])
            y = part if y is None else y + part
            k0 += kk
        return y

    def run(kinds):
        for c in range(o_ref.shape[0] // rows):
            rs = slice(c * rows, (c + 1) * rows)
            y = matmul(rs)
            if all(kd == PLAIN for kd in kinds):
                o_ref[rs, :] = y.astype(o_ref.dtype)
                continue
            for hh, (kind, grow, scale) in enumerate(kinds):
                sl = slice(hh * HEAD_DIM, (hh + 1) * HEAD_DIM)
                yh = y[:, sl]
                if kind == "norm_rope":
                    yh = _rms(yh, g_ref[grow:grow + 1, :])
                if kind != "plain":
                    yh = _rope(yh, cos_ref[rs, :], sin_ref[rs, :])
                if scale != 1.0:
                    yh = yh * scale
                o_ref[rs, sl] = yh.astype(o_ref.dtype)

    t0 = 0
    while t0 < len(tile_kinds):
        t1 = t0 + 1
        while t1 < len(tile_kinds) and tile_kinds[t1] == tile_kinds[t0]:
            t1 += 1
        pl.when((n >= t0) & (n < t1))(functools.partial(run, tile_kinds[t0]))
        t0 = t1


def _proj(hs, w3, layer_idx, col0, head_kinds, tn, out_dtype, cos_t, sin_t, gains):
    m = hs[0].shape[0]
    k_split = tuple(h.shape[1] for h in hs)
    k = sum(k_split)
    ncols = HEAD_DIM * len(head_kinds)
    assert col0 % tn == 0 and ncols % tn == 0 and w3.shape[1] == k
    n0 = col0 // tn
    nt = ncols // tn
    hpt = tn // HEAD_DIM
    tile_kinds = tuple(tuple(head_kinds[t * hpt:(t + 1) * hpt]) for t in range(nt))
    tm, rows = next(((a, b) for a, b in PROJ_BLOCKS if m % a == 0), (TM, TM))

    def w_tile(i, n):
        return (layer_idx, 0, n0 + jnp.where(i == 0, n, nt - 1))

    return pl.pallas_call(
        functools.partial(_proj_kernel, tile_kinds=tile_kinds, rows=rows, k_split=k_split),
        grid=(m // tm, nt),
        in_specs=[pl.BlockSpec((tm, kk), lambda i, n: (i, 0)) for kk in k_split]
                 + [pl.BlockSpec((None, k, tn), w_tile),
                    pl.BlockSpec((tm, HEAD_DIM), lambda i, n: (i, 0)),
                    pl.BlockSpec((tm, HEAD_DIM), lambda i, n: (i, 0)),
                    pl.BlockSpec((2, HEAD_DIM), lambda i, n: (0, 0))],
        out_specs=pl.BlockSpec((tm, tn), lambda i, n: (i, n)),
        out_shape=jax.ShapeDtypeStruct((m, ncols), out_dtype),
        scratch_shapes=[pltpu.VMEM((nt, k, tn), BF16)],
        compiler_params=_params(2),
        name="proj",
    )(*hs, w3, cos_t, sin_t, gains)


WIN_QB = 2


def _win_kernel(sink_ref, q_ref, kp_ref, kc_ref, kn_ref, kx_ref, vp_ref, vc_ref, vn_ref, vx_ref, o_ref, *, group,
                ctx_steps):
    kvh = pl.program_id(0)
    i = pl.program_id(1)
    nq = WIN_QB * BLK
    nk = (WIN_QB + 2) * BLK
    qi = lax.broadcasted_iota(I32, (group * nq, nk), 0) & (nq - 1)
    kj = lax.broadcasted_iota(I32, (group * nq, nk), 1)
    rel = kj - BLK - qi
    lo = jnp.where(i < ctx_steps, nk, jnp.where(i > ctx_steps, 0, BLK))
    hi = jnp.where(i < ctx_steps, 0, jnp.where(i < pl.num_programs(1) - 1, nk, nk - BLK))
    mask = (rel >= -BLK) & (rel <= BLK) & (kj >= lo) & (kj < hi)
    k_win = jnp.concatenate([kp_ref[...], kc_ref[...], kn_ref[...]], axis=0)
    v_win = jnp.concatenate([vp_ref[...], vc_ref[...], vn_ref[...]], axis=0)
    q = jnp.concatenate([q_ref[:, g * HEAD_DIM:(g + 1) * HEAD_DIM] for g in range(group)], axis=0)
    snk = jnp.concatenate([jnp.full((nq, 1), sink_ref[kvh * group + g] * LOG2E, F32) for g in range(group)], axis=0)
    s_loc = jnp.where(mask, _dot_nt(q, k_win), NEG_INF)
    s_ctx = _dot_nt(q, kx_ref[...])
    mx = jnp.maximum(jnp.maximum(jnp.max(s_loc, -1, keepdims=True), jnp.max(s_ctx, -1, keepdims=True)), snk)
    p_loc = jnp.exp2(s_loc - mx)
    p_ctx = jnp.exp2(s_ctx - mx)
    den = jnp.sum(p_loc, -1, keepdims=True) + jnp.sum(p_ctx, -1, keepdims=True) + jnp.exp2(snk - mx)
    o = (_dot(p_loc.astype(BF16), v_win) + _dot(p_ctx.astype(BF16), vx_ref[...])) / den
    for g in range(group):
        o_ref[:, g * HEAD_DIM:(g + 1) * HEAD_DIM] = o[g * nq:(g + 1) * nq, :].astype(o_ref.dtype)


def _window_attn(qkv, sink, n_ctx):
    m = qkv.shape[0]
    nb = (m - n_ctx) // BLK
    cb = n_ctx // BLK
    assert nb % WIN_QB == 0 and cb % WIN_QB == 0
    cq = cb // WIN_QB
    group = A_HEADS // A_KV
    kcol = A_Q // HEAD_DIM
    vcol = kcol + A_KV

    def edge_spec(col, block):
        return pl.BlockSpec((BLK, HEAD_DIM),
                            lambda h, i, s: (cb + jnp.clip((i - cq) * WIN_QB + block, 0, nb - 1), col + h))

    def own_spec(col):
        return pl.BlockSpec((WIN_QB * BLK, HEAD_DIM), lambda h, i, s: (i, col + h))

    def ctx_spec(col):
        return pl.BlockSpec((n_ctx, HEAD_DIM), lambda h, i, s: (0, col + h))

    grid_spec = pltpu.PrefetchScalarGridSpec(
        num_scalar_prefetch=1,
        grid=(A_KV, cq + nb // WIN_QB),
        in_specs=[pl.BlockSpec((WIN_QB * BLK, group * HEAD_DIM), lambda h, i, s: (i, h)),
                  edge_spec(kcol, -1), own_spec(kcol), edge_spec(kcol, WIN_QB), ctx_spec(kcol),
                  edge_spec(vcol, -1), own_spec(vcol), edge_spec(vcol, WIN_QB), ctx_spec(vcol)],
        out_specs=pl.BlockSpec((WIN_QB * BLK, group * HEAD_DIM), lambda h, i, s: (i, h)),
    )
    return pl.pallas_call(
        functools.partial(_win_kernel, group=group, ctx_steps=cq),
        grid_spec=grid_spec,
        out_shape=jax.ShapeDtypeStruct((m, A_Q), BF16),
        compiler_params=_params(2),
        name="window_attn",
    )(sink, qkv, qkv, qkv, qkv, qkv, qkv, qkv, qkv, qkv)


def _flash_kernel(q_ref, k_ref, v_ref, o_ref, va_ref, m_ref, acc_ref, *, group, tq, tk, n_ctx):
    rows = group * tq
    n_keys = k_ref.shape[0]

    @pl.when(pl.program_id(1) == 0)
    def _():
        va_ref[:, :HEAD_DIM] = v_ref[...]
        va_ref[:, HEAD_DIM:] = jnp.ones((n_keys, HEAD_DIM), BF16)

    def attend(limit, step):
        q = jnp.concatenate([q_ref[:, g * HEAD_DIM:(g + 1) * HEAD_DIM] for g in range(group)], axis=0)
        m_ref[...] = jnp.full((rows, HEAD_DIM), NEG_INF, F32)
        acc_ref[...] = jnp.zeros((rows, 2 * HEAD_DIM), F32)
        for j in range(limit // step):
            ks = slice(j * step, (j + 1) * step)
            s = _dot_nt(q, k_ref[ks, :])
            cols = [s[:, c * HEAD_DIM:(c + 1) * HEAD_DIM] for c in range(step // HEAD_DIM)]
            m_prev = m_ref[...]
            m_new = jnp.maximum(m_prev, jnp.max(functools.reduce(jnp.maximum, cols), -1, keepdims=True))
            alpha = jnp.exp2(m_prev - m_new)
            p = jnp.concatenate([jnp.exp2(c - m_new).astype(BF16) for c in cols], axis=1)
            acc_ref[...] = jnp.concatenate([alpha, alpha], axis=1) * acc_ref[...] + _dot(p, va_ref[ks, :])
            m_ref[...] = m_new
        for g in range(group):
            rs = slice(g * tq, (g + 1) * tq)
            o_ref[:, g * HEAD_DIM:(g + 1) * HEAD_DIM] = (acc_ref[rs, :HEAD_DIM] / acc_ref[rs, HEAD_DIM:]).astype(o_ref.dtype)

    ctx_tiles = n_ctx // tq
    pl.when(pl.program_id(1) < ctx_tiles)(functools.partial(attend, n_ctx, n_ctx))
    pl.when(pl.program_id(1) >= ctx_tiles)(functools.partial(attend, n_keys, tk))


def _flash_attn(qkv, q_col0, k_col0, v_col0, n_kv, group, n_ctx, tq, tk):
    m = qkv.shape[0]
    assert m % tq == 0 and n_ctx % tq == 0 and m % tk == 0
    qc0 = q_col0 // (group * HEAD_DIM)
    kc0 = k_col0 // HEAD_DIM
    vc0 = v_col0 // HEAD_DIM
    rows = group * tq
    return pl.pallas_call(
        functools.partial(_flash_kernel, group=group, tq=tq, tk=tk, n_ctx=n_ctx),
        grid=(n_kv, m // tq),
        in_specs=[pl.BlockSpec((tq, group * HEAD_DIM), lambda h, i: (i, qc0 + h)),
                  pl.BlockSpec((m, HEAD_DIM), lambda h, i: (0, kc0 + h)),
                  pl.BlockSpec((m, HEAD_DIM), lambda h, i: (0, vc0 + h))],
        out_specs=pl.BlockSpec((tq, group * HEAD_DIM), lambda h, i: (i, h)),
        out_shape=jax.ShapeDtypeStruct((m, n_kv * group * HEAD_DIM), BF16),
        scratch_shapes=[pltpu.VMEM((m, 2 * HEAD_DIM), BF16), pltpu.VMEM((rows, HEAD_DIM), F32),
                        pltpu.VMEM((rows, 2 * HEAD_DIM), F32)],
        compiler_params=_params(2),
        name="flash_attn",
    )(qkv, qkv, qkv)


def _log_sigmoid(x):
    return jnp.minimum(x, 0.0) - jnp.log(1.0 + jnp.exp(-jnp.abs(x)))


def _dot_exact(a, b):
    return jnp.dot(a, b, preferred_element_type=F32, precision=lax.Precision.HIGHEST)


def _mlstm_head(q, k, v, i_col, i_row, b_col, b_row, b_tot, mask, c_ref, n_ref, m_ref):
    m_prev = m_ref[:, 0:1]
    c_prev = c_ref[...]
    n_prev = n_ref[...]
    qs = q * (B_DK ** -0.5)
    qb = qs.astype(BF16)
    dlog = jnp.where(mask, b_col - b_row + i_row, NEG_INF)
    inter = b_col + m_prev
    m_t = jnp.maximum(inter, jnp.max(dlog, -1, keepdims=True))
    dw = jnp.exp(dlog - m_t)
    iw = jnp.exp(inter - m_t)
    sc = _dot_nt(qb, k.astype(BF16)) * dw
    num = _dot(sc.astype(BF16), v.astype(BF16)) + iw * _dot(qb, c_prev.astype(BF16))
    den = jnp.sum(sc, -1, keepdims=True) + iw * jnp.sum(qs * n_prev, -1, keepdims=True)
    h = num / jnp.maximum(jnp.abs(den), jnp.exp(-m_t))
    glog_col = b_tot - b_col + i_col
    glog_row = b_tot - b_row + i_row
    m_new = jnp.maximum(b_tot + m_prev, jnp.max(glog_row, -1, keepdims=True))
    decay = jnp.exp(b_tot + m_prev - m_new)
    wk = jnp.exp(glog_col - m_new) * k
    c_ref[...] = decay * c_prev + lax.dot_general(wk.astype(BF16), v.astype(BF16), (((0,), (0,)), ((), ())),
                                                  preferred_element_type=F32)
    n_ref[...] = decay * n_prev + jnp.sum(wk, 0, keepdims=True)
    m_ref[...] = jnp.broadcast_to(m_new, m_ref.shape)
    return h


def _mlstm_kernel(brow_ref, bcol_ref,
                  qf_ref, kf_ref, vf_ref, gf_ref, gtf_ref,
                  qb_ref, kb_ref, vb_ref, gb_ref, gtb_ref,
                  of_ref, ob_ref, c_ref, n_ref, m_ref):
    L = MLSTM_CHUNK

    @pl.when(pl.program_id(0) == 0)
    def _():
        c_ref[...] = jnp.zeros(c_ref.shape, F32)
        n_ref[...] = jnp.zeros(n_ref.shape, F32)
        m_ref[...] = jnp.zeros(m_ref.shape, F32)

    r = lax.broadcasted_iota(I32, (L, L), 0)
    cidx = lax.broadcasted_iota(I32, (L, L), 1)
    lane = lax.broadcasted_iota(I32, (L, HEAD_DIM), 1)
    sub = lax.broadcasted_iota(I32, (2 * 2 * B_HEADS, L), 0)
    for d, (q_ref, k_ref, v_ref, g_ref, gt_ref, o_ref) in enumerate(
            ((qf_ref, kf_ref, vf_ref, gf_ref, gtf_ref, of_ref), (qb_ref, kb_ref, vb_ref, gb_ref, gtb_ref, ob_ref))):
        mask = (cidx <= r) if d == 0 else (cidx >= r)
        mask_t = (r <= cidx) if d == 0 else (r >= cidx)
        edge = L - 1 if d == 0 else 0
        chunks = range(MLSTM_STEP_CHUNKS) if d == 0 else reversed(range(MLSTM_STEP_CHUNKS))
        for cc in chunks:
            rs = slice(cc * L, (cc + 1) * L)
            gc = g_ref[rs, :] + brow_ref[...]
            gc = jnp.where(lane < 2 * B_HEADS, gc, _log_sigmoid(gc))
            gr = gt_ref[:, rs] + bcol_ref[...]
            gr = jnp.where(sub < 2 * B_HEADS, gr, _log_sigmoid(gr))
            bc_all = _dot_exact(mask.astype(F32), gc)
            br_all = _dot_exact(gr, mask_t.astype(F32))
            for hd in range(B_HEADS):
                ci = d * B_HEADS + hd
                cf = 2 * B_HEADS + ci
                b_col = bc_all[:, cf:cf + 1]
                h = _mlstm_head(q_ref[rs, hd * B_DK:(hd + 1) * B_DK], k_ref[rs, hd * B_DK:(hd + 1) * B_DK],
                                v_ref[rs, hd * B_DV:(hd + 1) * B_DV],
                                gc[:, ci:ci + 1], gr[ci:ci + 1, :], b_col, br_all[cf:cf + 1, :],
                                b_col[edge:edge + 1, :], mask, c_ref.at[ci], n_ref.at[ci], m_ref.at[ci])
                o_ref[rs, hd * B_DV:(hd + 1) * B_DV] = h


def _mlstm(qkvo, gates, gates_t, bias_row, bias_col, n_ctx):
    m = qkvo.shape[0]
    L = MLSTM_STEP_CHUNKS * MLSTM_CHUNK
    assert m % L == 0 and n_ctx % L == 0
    nc = m // L
    ncc = n_ctx // L
    kq = B_Q // B_Q
    kv = (2 * B_Q) // B_V

    def fw(s):
        return s

    def bw(s):
        return jnp.where(s < ncc, ncc - 1 - s, nc - 1 + ncc - s)

    def specs(order):
        return [pl.BlockSpec((L, B_Q), lambda s: (order(s), 0)),
                pl.BlockSpec((L, B_Q), lambda s: (order(s), kq)),
                pl.BlockSpec((L, B_V), lambda s: (order(s), kv)),
                pl.BlockSpec((L, HEAD_DIM), lambda s: (order(s), 0)),
                pl.BlockSpec((4 * B_HEADS, L), lambda s: (0, order(s)))]

    nst = 2 * B_HEADS
    return pl.pallas_call(
        _mlstm_kernel,
        grid=(nc,),
        in_specs=[pl.BlockSpec((1, HEAD_DIM), lambda s: (0, 0)),
                  pl.BlockSpec((4 * B_HEADS, 1), lambda s: (0, 0))] + specs(fw) + specs(bw),
        out_specs=[pl.BlockSpec((L, B_V), lambda s: (fw(s), 0)),
                   pl.BlockSpec((L, B_V), lambda s: (bw(s), 0))],
        out_shape=[jax.ShapeDtypeStruct((m, B_V), F32), jax.ShapeDtypeStruct((m, B_V), F32)],
        scratch_shapes=[pltpu.VMEM((nst, B_DK, B_DV), F32), pltpu.VMEM((nst, 1, B_DK), F32),
                        pltpu.VMEM((nst, 1, HEAD_DIM), F32)],
        compiler_params=_params(1),
        name="mlstm",
    )(bias_row, bias_col, qkvo, qkvo, qkvo, gates, gates_t, qkvo, qkvo, qkvo, gates, gates_t)


def _bout_kernel(hf_ref, hb_ref, o_ref, g_ref, y_ref):
    for hd in range(B_HEADS):
        sl = slice(hd * B_DV, (hd + 1) * B_DV)
        hn = _rms(hf_ref[:, sl] + hb_ref[:, sl], g_ref[:, sl])
        y_ref[:, sl] = (jax.nn.sigmoid(o_ref[:, sl]) * hn).astype(y_ref.dtype)


def _b_out(hf, hb, qkvo, out_g):
    m = hf.shape[0]
    ocol = (2 * B_Q + B_V) // B_V
    row = pl.BlockSpec((TM, B_V), lambda i: (i, 0))
    return pl.pallas_call(
        _bout_kernel,
        grid=(m // TM,),
        in_specs=[row, row, pl.BlockSpec((TM, B_V), lambda i: (i, ocol)),
                  pl.BlockSpec((1, B_V), lambda i: (0, 0))],
        out_specs=row,
        out_shape=jax.ShapeDtypeStruct((m, B_V), BF16),
        compiler_params=_params(1),
        name="mlstm_out",
    )(hf, hb, qkvo, out_g)


def _ffn_kernel(ge_ref, gb_ref, gn_ref, x_ref, wg_ref, wu_ref, wd_ref, o_ref, wgb_ref, wub_ref, wdb_ref, *, n_sub):
    g = pl.program_id(0)
    j = pl.program_id(1)
    gn = gn_ref[g]
    big = 8 * FFN_UNIT
    n_big = lax.shift_right_logical(gn, 3)

    def cast_weights():
        wgb_ref[...] = wg_ref[...].astype(BF16)
        wub_ref[...] = wu_ref[...].astype(BF16)
        wdb_ref[...] = wd_ref[...].astype(BF16)

    def rows_step(off, rows, first):
        x = x_ref[pl.ds(off, rows), :]
        a = _dot(x, wgb_ref[...])
        b = _dot(x, wub_ref[...])
        y = _dot((a * jax.nn.sigmoid(a) * b).astype(BF16), wdb_ref[...])
        if first:
            o_ref[pl.ds(off, rows), :] = y
        else:
            o_ref[pl.ds(off, rows), :] += y

    def used_tiles(first):
        @pl.when(n_big > 0)
        def _():
            cast_weights()
            rows_step(0, big, first)

        @pl.when((n_big == 0) & (gn > 0))
        def _():
            cast_weights()

        def big_step(rq, c):
            rows_step(pl.multiple_of(rq * big, big), big, first)
            return c

        lax.fori_loop(1, n_big, big_step, 0)
        for units in (4, 2, 1):
            @pl.when((gn & units) == units)
            def _(units=units):
                rows = units * FFN_UNIT
                rows_step(pl.multiple_of((gn & ~(2 * units - 1)) * FFN_UNIT, rows), rows, first)

    @pl.when(j == 0)
    def _():
        used_tiles(True)

        def zero(r, c):
            off = pl.multiple_of(r * FFN_UNIT, FFN_UNIT)
            o_ref[pl.ds(off, FFN_UNIT), :] = jnp.zeros((FFN_UNIT, o_ref.shape[1]), F32)
            return c

        lax.fori_loop(gn, n_sub * (TM // FFN_UNIT), zero, 0)

    @pl.when(j > 0)
    def _():
        used_tiles(False)


def _ffn(x, wg, wu, wd, e0, grp_e, grp_b, grp_n, n_sub, tf):
    p, d = x.shape
    f = wg.shape[-1]
    rg = n_sub * TM
    assert f % tf == 0 and p % rg == 0
    once = pl.Buffered(1)
    nj = f // tf

    def slice_of(j, g, gn):
        return jnp.where(gn[g] > 0, j, nj - 1)

    grid_spec = pltpu.PrefetchScalarGridSpec(
        num_scalar_prefetch=3,
        grid=(grp_e.shape[0], nj),
        in_specs=[pl.BlockSpec((rg, d), lambda g, j, ge, gb, gn: (gb[g], 0), pipeline_mode=once),
                  pl.BlockSpec((None, d, tf), lambda g, j, ge, gb, gn: (e0 + ge[g], 0, slice_of(j, g, gn))),
                  pl.BlockSpec((None, d, tf), lambda g, j, ge, gb, gn: (e0 + ge[g], 0, slice_of(j, g, gn))),
                  pl.BlockSpec((None, tf, d), lambda g, j, ge, gb, gn: (e0 + ge[g], slice_of(j, g, gn), 0))],
        out_specs=pl.BlockSpec((rg, d), lambda g, j, ge, gb, gn: (g, 0), pipeline_mode=once),
        scratch_shapes=[pltpu.VMEM((d, tf), BF16), pltpu.VMEM((d, tf), BF16), pltpu.VMEM((tf, d), BF16)],
    )
    return pl.pallas_call(
        functools.partial(_ffn_kernel, n_sub=n_sub),
        grid_spec=grid_spec,
        out_shape=jax.ShapeDtypeStruct((p, d), F32),
        compiler_params=_params(2),
        name="ffn",
    )(grp_e, grp_b, grp_n, x, wg, wu, wd)


def _dense_groups(m):
    n_sub = next(r for r in (11, 3, 2, 1) if m % (r * TM) == 0)
    n_groups = m // (n_sub * TM)
    full = jnp.full((n_groups,), n_sub * (TM // FFN_UNIT), I32)
    return (jnp.zeros((n_groups,), I32), jnp.arange(n_groups, dtype=I32), full), n_sub


def _route_top2(h, r_ref, idx_ref, w_ref):
    logits = _dot(h.astype(BF16), r_ref[...].astype(BF16))
    lane = lax.broadcasted_iota(I32, logits.shape, 1).astype(F32)
    big = float(HEAD_DIM)
    lg = jnp.where(lane < N_EXPERTS, logits, NEG_INF)
    m1 = jnp.max(lg, -1, keepdims=True)
    i1 = jnp.min(jnp.where(lg == m1, lane, big), -1, keepdims=True)
    lg2 = jnp.where(lane == i1, NEG_INF, lg)
    m2 = jnp.max(lg2, -1, keepdims=True)
    i2 = jnp.min(jnp.where(lg2 == m2, lane, big), -1, keepdims=True)
    e2 = jnp.exp(m2 - m1)
    den = 1.0 + e2
    idx_ref[...] = jnp.where(lane == 0.0, i1, jnp.where(lane == 1.0, i2, 0.0)).astype(I32)
    w_ref[...] = jnp.where(lane == 0.0, 1.0 / den, jnp.where(lane == 1.0, e2 / den, 0.0))


def _moe_plan(eidx, ew, n_groups, n_sub):
    m = eidx.shape[0]
    rg = n_sub * TM
    e_flat = eidx.reshape(-1)
    experts = jnp.arange(N_EXPERTS, dtype=I32)
    onehot = (e_flat[:, None] == experts[None, :]).astype(I32)
    csum = jnp.cumsum(onehot, axis=0)
    rank = jnp.sum(csum * onehot, axis=1) - 1
    counts = csum[-1]
    groups_e = (counts + rg - 1) // rg
    grp_end = jnp.cumsum(groups_e)
    grp_start = grp_end - groups_e
    pos = jnp.sum(onehot * grp_start[None, :], axis=1) * rg + rank
    token = jnp.arange(2 * m, dtype=I32) // 2
    src = jnp.zeros((n_groups * rg,), I32).at[pos].set(token)
    gids = jnp.arange(n_groups, dtype=I32)
    valid = gids < grp_end[-1]
    grp_e = jnp.sum((gids[:, None] >= grp_end[None, :]).astype(I32), axis=1)
    grp_e = jnp.where(valid, grp_e, jnp.max(jnp.where(counts > 0, experts, 0)))
    grp_b = jnp.minimum(gids, grp_end[-1] - 1).astype(I32)
    mine = (grp_e[:, None] == experts[None, :]).astype(I32)
    rows_left = jnp.sum(mine * counts[None, :], axis=1) - (gids - jnp.sum(mine * grp_start[None, :], axis=1)) * rg
    upt = TM // FFN_UNIT
    grp_n = jnp.where(valid, jnp.clip((rows_left + FFN_UNIT - 1) // FFN_UNIT, 0, n_sub * upt), 0).astype(I32)
    tile_valid = ((jnp.arange(n_groups * n_sub, dtype=I32) % n_sub) * upt < jnp.repeat(grp_n, n_sub)).astype(I32)
    pos2 = pos.reshape(m, 2).astype(I32)
    return src, (grp_e.astype(I32), grp_b, grp_n), tile_valid, pos2[:, 0], pos2[:, 1]


SUBLANES = 8


def _gather_kernel(src_ref, tv_ref, h_hbm, o_ref, buf_ref, sem):
    t = pl.program_id(0)
    nt = pl.num_programs(0)

    def row_copy(row, slot, r8, s):
        return pltpu.make_async_copy(h_hbm.at[pl.ds(row, 1), :], buf_ref.at[slot, r8, pl.ds(s, 1), :], sem.at[slot])

    def fetch(tile):
        slot = tile % 2

        @pl.when(tv_ref[tile] == 1)
        def _():
            def issue(r8, c):
                for s in range(SUBLANES):
                    row_copy(src_ref[tile * TM + r8 * SUBLANES + s], slot, r8, s).start()
                return c

            lax.fori_loop(0, TM // SUBLANES, issue, 0)

    @pl.when(t == 0)
    def _():
        fetch(t)

    @pl.when(t + 1 < nt)
    def _():
        fetch(t + 1)

    @pl.when(tv_ref[t] == 1)
    def _():
        slot = t % 2

        def wait(r8, c):
            for s in range(SUBLANES):
                row_copy(0, slot, r8, s).wait()
            return c

        lax.fori_loop(0, TM // SUBLANES, wait, 0)
        o_ref[...] = buf_ref[slot].reshape(TM, o_ref.shape[1]).astype(o_ref.dtype)

    @pl.when(tv_ref[t] == 0)
    def _():
        o_ref[...] = jnp.zeros(o_ref.shape, o_ref.dtype)


def _gather_rows(h, src, tile_valid):
    d = h.shape[1]
    p = src.shape[0]
    grid_spec = pltpu.PrefetchScalarGridSpec(
        num_scalar_prefetch=2,
        grid=(p // TM,),
        in_specs=[pl.BlockSpec(memory_space=pl.ANY)],
        out_specs=pl.BlockSpec((TM, d), lambda t, s, tv: (t, 0)),
        scratch_shapes=[pltpu.VMEM((2, TM // SUBLANES, SUBLANES, d), F32), pltpu.SemaphoreType.DMA((2,))],
    )
    return pl.pallas_call(
        _gather_kernel,
        grid_spec=grid_spec,
        out_shape=jax.ShapeDtypeStruct((p, d), BF16),
        compiler_params=_params(1),
        name="moe_gather",
    )(src, tile_valid, h)


def _combine_kernel(p0_ref, p1_ref, ys_hbm, ew_ref, x_ref, gpost_ref, gpre_ref, mpost_ref, mpre_ref, xo_ref, *rest,
                    with_pre, tile0):
    if with_pre:
        h_ref, buf_ref, sem = rest
    else:
        h_ref = None
        buf_ref, sem = rest
    t = pl.program_id(0)
    nt = pl.num_programs(0)

    def row_copy(row, slot, k, r):
        return pltpu.make_async_copy(ys_hbm.at[pl.ds(row, 1), :], buf_ref.at[slot, k, pl.ds(r, 1), :], sem.at[slot])

    def fetch(tile):
        slot = tile % 2
        base = (tile + tile0) * TM

        def issue(r, c):
            row_copy(p0_ref[base + r], slot, 0, r).start()
            row_copy(p1_ref[base + r], slot, 1, r).start()
            return c

        lax.fori_loop(0, TM, issue, 0, unroll=SUBLANES)

    @pl.when(t == 0)
    def _():
        fetch(t)

    @pl.when(t + 1 < nt)
    def _():
        fetch(t + 1)

    slot = t % 2

    def wait(r, c):
        row_copy(0, slot, 0, r).wait()
        row_copy(0, slot, 1, r).wait()
        return c

    lax.fori_loop(0, TM, wait, 0, unroll=SUBLANES)
    f = ew_ref[:, 0:1] * buf_ref[slot, 0] + ew_ref[:, 1:2] * buf_ref[slot, 1]
    _post_pre_math(x_ref[...], f, gpost_ref, gpre_ref, mpost_ref, mpre_ref, xo_ref, h_ref, 1)


def _combine_post_pre(x, ys, pos0, pos1, ew, norm_g, mods, layer, n_ctx_tiles, h_dtype, tile0):
    m, d = x.shape
    pre_layer = min(layer + 1, norm_g.shape[0] - 1)
    with_pre = h_dtype is not None
    row_in = pl.BlockSpec((TM, d), lambda i, a, b: (i + tile0, 0))
    row_out = pl.BlockSpec((TM, d), lambda i, a, b: (i, 0))
    m_out = m - tile0 * TM

    def mod_spec(l):
        return pl.BlockSpec((None, None, 6, d), lambda i, a, b: (l, jnp.where(i + tile0 < n_ctx_tiles, 1, 0), 0, 0))

    out_shape = [jax.ShapeDtypeStruct((m_out, d), F32)]
    out_specs = [row_out]
    if with_pre:
        out_shape.append(jax.ShapeDtypeStruct((m_out, d), h_dtype))
        out_specs.append(row_out)
    grid_spec = pltpu.PrefetchScalarGridSpec(
        num_scalar_prefetch=2,
        grid=(m_out // TM,),
        in_specs=[pl.BlockSpec(memory_space=pl.ANY),
                  pl.BlockSpec((TM, HEAD_DIM), lambda i, a, b: (i + tile0, 0)), row_in,
                  pl.BlockSpec((None, 4, d), lambda i, a, b: (layer, 0, 0)),
                  pl.BlockSpec((None, 4, d), lambda i, a, b: (pre_layer, 0, 0)),
                  mod_spec(layer), mod_spec(pre_layer)],
        out_specs=out_specs,
        scratch_shapes=[pltpu.VMEM((2, 2, TM, d), F32), pltpu.SemaphoreType.DMA((2,))],
    )
    res = pl.pallas_call(
        functools.partial(_combine_kernel, with_pre=with_pre, tile0=tile0),
        grid_spec=grid_spec,
        out_shape=out_shape,
        compiler_params=_params(1),
        name="moe_combine",
    )(pos0, pos1, ys, ew, x, norm_g, norm_g, mods, mods)
    return res if with_pre else (res[0], None)


def _rope_tables(n_ctx, s):
    pos = jnp.arange(s)
    row = (pos // GRID_W).astype(F32)
    col = (pos % GRID_W).astype(F32)
    inv = ROPE_THETA ** (-jnp.arange(ROPE_FREQS, dtype=F32) / ROPE_FREQS)
    ar, ac = row[:, None] * inv, col[:, None] * inv
    cos = jnp.concatenate([jnp.cos(ar), jnp.cos(ar), jnp.cos(ac), jnp.cos(ac)], axis=-1)
    sin = jnp.concatenate([-jnp.sin(ar), jnp.sin(ar), -jnp.sin(ac), jnp.sin(ac)], axis=-1)
    cos = jnp.concatenate([jnp.ones((n_ctx, HEAD_DIM), F32), cos], axis=0)
    sin = jnp.concatenate([jnp.zeros((n_ctx, HEAD_DIM), F32), sin], axis=0)
    return cos, sin


def _layer_ab(h, j, ab_w_in, a_sink, b_ig_bias, b_fg_bias, b_norm_g, cos_t, sin_t, n_ctx):
    scale = HEAD_DIM ** -0.5 * LOG2E
    no_g = jnp.ones((2, HEAD_DIM), F32)
    a_kinds = [("rope", 0, scale)] * A_HEADS + [("rope", 0, 1.0)] * A_KV + [PLAIN] * A_KV
    a_qkv = _proj([h], ab_w_in, j, 0, a_kinds, 512, BF16, cos_t, sin_t, no_g)
    b_qkvo = _proj([h], ab_w_in, j, A_QKV, [PLAIN] * (B_QKVO // HEAD_DIM), 512, F32, cos_t, sin_t, no_g)
    ngate = 4 * B_HEADS
    w_gate = jnp.pad(ab_w_in[j, :, A_QKV + B_QKVO:], ((0, 0), (0, HEAD_DIM - ngate)))[None]
    gates = _proj([h], w_gate, 0, 0, [PLAIN], HEAD_DIM, F32, cos_t, sin_t, no_g)

    ya = _window_attn(a_qkv, a_sink[j], n_ctx)

    bias = jnp.concatenate([b_ig_bias[j].reshape(-1), b_fg_bias[j].reshape(-1)]).astype(F32)
    bias_row = jnp.pad(bias, (0, HEAD_DIM - ngate))[None, :]
    hf, hb = _mlstm(b_qkvo, gates, jnp.transpose(gates[:, :ngate]), bias_row, bias[:, None], n_ctx)
    yb = _b_out(hf, hb, b_qkvo, b_norm_g[j][None, :])
    return [ya, yb]


def _layer_c(h, j, c_w_qkv, c_qk_g, cos_t, sin_t, n_ctx):
    scale = HEAD_DIM ** -0.5 * LOG2E
    c_kinds = [("norm_rope", 0, scale)] * C_HEADS + [("norm_rope", 1, 1.0)] * C_KV + [PLAIN] * C_KV
    qkv = _proj([h], c_w_qkv, j, 0, c_kinds, 512, BF16, cos_t, sin_t, c_qk_g[j])
    m = h.shape[0]
    tk = next(t for t in (1408, 768, TM) if m % t == 0)
    return [_flash_attn(qkv, 0, C_Q, C_Q + C_KVW, C_KV, C_HEADS // C_KV, n_ctx, TM, tk)]


def kernel(x, c, ctx, c_ctx, ada_w, ada_b, norm_g, ab_w_in, ab_w_out, a_sink, b_ig_bias, b_fg_bias, b_norm_g,
           c_w_qkv, c_w_out, c_qk_g, ffn_w_gate, ffn_w_up, ffn_w_down, moe_router, moe_w_gate, moe_w_up,
           moe_w_down):
    depth = ada_w.shape[0]
    s, d = x.shape[1], x.shape[2]
    n_ctx = ctx.shape[1]
    m = n_ctx + s
    assert x.shape[0] == 1 and n_ctx % TM == 0 and s % TM == 0
    n_ctx_tiles = n_ctx // TM
    n_tiles = m // TM
    no_g = jnp.ones((2, HEAD_DIM), F32)

    mods = _ada_mods(c, c_ctx, ada_w, ada_b)
    cos_t, sin_t = _rope_tables(n_ctx, s)
    xs, h = _prenorm(ctx[0], x[0], norm_g, mods, 0)
    dense_groups, dense_sub = _dense_groups(m)
    n_exp = moe_w_gate.shape[1]
    moe_g = moe_w_gate.reshape((-1,) + moe_w_gate.shape[2:])
    moe_u = moe_w_up.reshape((-1,) + moe_w_up.shape[2:])
    moe_d = moe_w_down.reshape((-1,) + moe_w_down.shape[2:])

    for layer in range(depth):
        last = layer == depth - 1
        j = layer // 2
        even = layer % 2 == 0
        if even:
            mix = _layer_ab(h, j, ab_w_in, a_sink, b_ig_bias, b_fg_bias, b_norm_g, cos_t, sin_t, n_ctx)
            w_out = ab_w_out
        else:
            mix = _layer_c(h, j, c_w_qkv, c_qk_g, cos_t, sin_t, n_ctx)
            w_out = c_w_out
        y = _proj(mix, w_out, j, 0, [PLAIN] * (d // HEAD_DIM), 512 if d % 512 == 0 else d, F32, cos_t, sin_t, no_g)
        if even:
            xs, h = _post_pre(xs, y, norm_g, mods, layer, 0, n_ctx_tiles, BF16)
        else:
            router_p = jnp.pad(moe_router[j], ((0, 0), (0, HEAD_DIM - n_exp)))
            xs, h, eidx, ew = _post_pre(xs, y, norm_g, mods, layer, 0, n_ctx_tiles, F32, router_p)
        if even:
            f_dim = ffn_w_gate.shape[-1]
            f = _ffn(h, ffn_w_gate, ffn_w_up, ffn_w_down, j, *dense_groups, dense_sub, FFN_SLICE)
            xs, h = _post_pre(xs, f, norm_g, mods, layer, 1, n_ctx_tiles, None if last else BF16)
        else:
            n_sub = MOE_GROUP_TILES
            n_groups = (2 * m + n_exp * (n_sub * TM - 1)) // (n_sub * TM)
            src, groups, tile_valid, pos0, pos1 = _moe_plan(eidx[:, :2], ew[:, :2], n_groups, n_sub)
            x_s = _gather_rows(h, src, tile_valid)
            y_s = _ffn(x_s, moe_g, moe_u, moe_d, j * n_exp, *groups, n_sub, FFN_SLICE)
            xs, h = _combine_post_pre(xs, y_s, pos0, pos1, ew, norm_g, mods, layer, n_ctx_tiles,
                                      None if last else BF16, n_ctx_tiles if last else 0)
    return (xs if xs.shape[0] == s else xs[n_ctx:])[None]
```

```python
import functools

import jax
import jax.numpy as jnp
import numpy as np
from jax import lax
from jax.experimental import pallas as pl
from jax.experimental.pallas import tpu as pltpu

F32 = jnp.float32
BF16 = jnp.bfloat16
I32 = jnp.int32

EPS = 1e-6
GRID_W = 64
HEAD_DIM = 128
ROPE_THETA = 10000.0
ROPE_FREQS = HEAD_DIM // 4
BLK = 128
A_HEADS = 8
A_KV = 2
B_HEADS = 4
B_DK = 128
B_DV = 256
MLSTM_CHUNK = 128
MLSTM_STEP_CHUNKS = 2
C_HEADS = 16
C_KV = 4
N_EXPERTS = 8

A_Q = A_HEADS * HEAD_DIM
A_QKV = A_Q + 2 * A_KV * HEAD_DIM
B_Q = B_HEADS * B_DK
B_V = B_HEADS * B_DV
B_QKVO = 2 * B_Q + 2 * B_V
C_Q = C_HEADS * HEAD_DIM
C_KVW = C_KV * HEAD_DIM

TM = 256
MOE_GROUP_TILES = 9
PROJ_BLOCKS = ((1408, 352), (768, 384))
FFN_UNIT = 128
FFN_SLICE = 256
VMEM_LIMIT = 56 * 1024 * 1024
NEG_INF = float("-inf")
LOG2E = 1.4426950408889634


def _row_block(m):
    return next(r * TM for r in (3, 2, 1) if m % (r * TM) == 0)


def _params(n_axes, vmem=VMEM_LIMIT):
    return pltpu.CompilerParams(dimension_semantics=("arbitrary",) * n_axes, vmem_limit_bytes=vmem)


def _rms(x, g):
    return x * lax.rsqrt(jnp.mean(x * x, axis=-1, keepdims=True) + EPS) * g


def _dot(a, b):
    return jnp.dot(a, b, preferred_element_type=F32)


def _dot_nt(a, b):
    return lax.dot_general(a, b, (((1,), (1,)), ((), ())), preferred_element_type=F32)


def _ada_kernel(s_ref, w_ref, b_ref, o_ref):
    s = s_ref[...]
    s = s * jax.nn.sigmoid(s)
    o_ref[...] = _dot(s.astype(BF16), w_ref[...].astype(BF16)) + b_ref[...]


def _ada_mods(c, c_ctx, ada_w, ada_b):
    depth, d, d6 = ada_w.shape
    tn = 1536 if d6 % 1536 == 0 else d6
    s = jnp.zeros((8, d), F32).at[0].set(c[0]).at[1].set(c_ctx)
    out = pl.pallas_call(
        _ada_kernel,
        grid=(depth, d6 // tn),
        in_specs=[pl.BlockSpec((8, d), lambda l, n: (0, 0)),
                  pl.BlockSpec((None, d, tn), lambda l, n: (l, 0, n)),
                  pl.BlockSpec((None, 1, tn), lambda l, n: (l, 0, n))],
        out_specs=pl.BlockSpec((None, 8, tn), lambda l, n: (l, 0, n)),
        out_shape=jax.ShapeDtypeStruct((depth, 8, d6), F32),
        compiler_params=_params(2),
        name="ada_mods",
    )(s, ada_w, ada_b.reshape(depth, 1, d6))
    return out[:, :2].reshape(depth, 2, 6, d)


def _mod_spec(d, layer, n_ctx_tiles):
    return pl.BlockSpec((None, None, 6, d), lambda i: (layer, jnp.where(i < n_ctx_tiles, 1, 0), 0, 0))


def _prenorm_kernel(c_ref, x_ref, g_ref, mod_ref, xs_ref, h_ref, *, n_ctx_tiles):
    def emit(x):
        xs_ref[...] = x
        hn = _rms(x, g_ref[0:1, :])
        h_ref[...] = (hn * (1.0 + mod_ref[1:2, :]) + mod_ref[0:1, :]).astype(h_ref.dtype)

    pl.when(pl.program_id(0) < n_ctx_tiles)(lambda: emit(c_ref[...]))
    pl.when(pl.program_id(0) >= n_ctx_tiles)(lambda: emit(x_ref[...]))


def _prenorm(ctx2, x2, norm_g, mods, layer):
    (n_ctx, d), s = ctx2.shape, x2.shape[0]
    nct = n_ctx // TM
    row = pl.BlockSpec((TM, d), lambda i: (i, 0))
    return pl.pallas_call(
        functools.partial(_prenorm_kernel, n_ctx_tiles=nct),
        grid=((n_ctx + s) // TM,),
        in_specs=[pl.BlockSpec((TM, d), lambda i: (jnp.minimum(i, nct - 1), 0)),
                  pl.BlockSpec((TM, d), lambda i: (jnp.maximum(i - nct, 0), 0)),
                  pl.BlockSpec((None, 4, d), lambda i: (layer, 0, 0)),
                  _mod_spec(d, layer, nct)],
        out_specs=[row, row],
        out_shape=[jax.ShapeDtypeStruct((n_ctx + s, d), F32), jax.ShapeDtypeStruct((n_ctx + s, d), BF16)],
        compiler_params=_params(1),
        name="prenorm",
    )(ctx2, x2, norm_g, mods)


def _post_pre_math(x, y, gpost_ref, gpre_ref, mpost_ref, mpre_ref, xo_ref, h_ref, sub):
    g_row = 1 + 2 * sub
    gate_row = 2 + 3 * sub
    xn = x + mpost_ref[gate_row:gate_row + 1, :] * _rms(y, gpost_ref[g_row:g_row + 1, :])
    xo_ref[...] = xn
    if h_ref is None:
        return None
    nsub = 1 - sub
    hn = _rms(xn, gpre_ref[2 * nsub:2 * nsub + 1, :])
    h = hn * (1.0 + mpre_ref[3 * nsub + 1:3 * nsub + 2, :]) + mpre_ref[3 * nsub:3 * nsub + 1, :]
    h_ref[...] = h.astype(h_ref.dtype)
    return h


def _post_pre_kernel(x_ref, y_ref, gpost_ref, gpre_ref, mpost_ref, mpre_ref, *rest, sub, with_router):
    if with_router:
        r_ref, xo_ref, h_ref, idx_ref, ew_ref = rest
    else:
        xo_ref, h_ref = rest[0], (rest[1] if len(rest) > 1 else None)
    h = _post_pre_math(x_ref[...], y_ref[...], gpost_ref, gpre_ref, mpost_ref, mpre_ref, xo_ref, h_ref, sub)
    if with_router:
        _route_top2(h, r_ref, idx_ref, ew_ref)


def _post_pre(x, y, norm_g, mods, layer, sub, n_ctx_tiles, h_dtype, router_p=None):
    m, d = x.shape
    pre_layer = layer if sub == 0 else min(layer + 1, norm_g.shape[0] - 1)
    with_router = router_p is not None
    row = pl.BlockSpec((TM, d), lambda i: (i, 0))
    lanes = pl.BlockSpec((TM, HEAD_DIM), lambda i: (i, 0))
    in_specs = [row, row,
                pl.BlockSpec((None, 4, d), lambda i: (layer, 0, 0)),
                pl.BlockSpec((None, 4, d), lambda i: (pre_layer, 0, 0)),
                _mod_spec(d, layer, n_ctx_tiles),
                _mod_spec(d, pre_layer, n_ctx_tiles)]
    args = [x, y, norm_g, norm_g, mods, mods]
    out_shape = [jax.ShapeDtypeStruct((m, d), F32)]
    out_specs = [row]
    if h_dtype is not None:
        out_shape.append(jax.ShapeDtypeStruct((m, d), h_dtype))
        out_specs.append(row)
    if with_router:
        in_specs.append(pl.BlockSpec((d, HEAD_DIM), lambda i: (0, 0)))
        args.append(router_p)
        out_shape += [jax.ShapeDtypeStruct((m, HEAD_DIM), I32), jax.ShapeDtypeStruct((m, HEAD_DIM), F32)]
        out_specs += [lanes, lanes]
    res = pl.pallas_call(
        functools.partial(_post_pre_kernel, sub=sub, with_router=with_router),
        grid=(m // TM,),
        in_specs=in_specs,
        out_specs=out_specs,
        out_shape=out_shape,
        compiler_params=_params(1),
        name="post_pre",
    )(*args)
    if with_router:
        return res[0], res[1], res[2], res[3]
    return res if h_dtype is not None else (res[0], None)


def _rope(y, cos, sin):
    lane = lax.broadcasted_iota(I32, y.shape, 1)
    lower = (lane & (2 * ROPE_FREQS - 1)) < ROPE_FREQS
    partner = jnp.where(lower, pltpu.roll(y, HEAD_DIM - ROPE_FREQS, 1), pltpu.roll(y, ROPE_FREQS, 1))
    return y * cos + partner * sin


PLAIN = ("plain", 0, 1.0)


def _proj_kernel(*refs, tile_kinds, rows, k_split):
    xs_refs = refs[:len(k_split)]
    w_ref, cos_ref, sin_ref, g_ref, o_ref, wb_ref = refs[len(k_split):]
    n = pl.program_id(1)

    @pl.when(pl.program_id(0) == 0)
    def _():
        wb_ref[n] = w_ref[...].astype(BF16)

    def matmul(rs):
        k0, y = 0, None
        for x_ref, kk in zip(xs_refs, k_split):
            part = _dot(x_ref[rs, :], wb_ref[n, k0:k0 + kk, :])
            y = part if y is None else y + part
            k0 += kk
        return y

    def run(kinds):
        for c in range(o_ref.shape[0] // rows):
            rs = slice(c * rows, (c + 1) * rows)
            y = matmul(rs)
            if all(kd == PLAIN for kd in kinds):
                o_ref[rs, :] = y.astype(o_ref.dtype)
                continue
            for hh, (kind, grow, scale) in enumerate(kinds):
                sl = slice(hh * HEAD_DIM, (hh + 1) * HEAD_DIM)
                yh = y[:, sl]
                if kind == "norm_rope":
                    yh = _rms(yh, g_ref[grow:grow + 1, :])
                if kind != "plain":
                    yh = _rope(yh, cos_ref[rs, :], sin_ref[rs, :])
                if scale != 1.0:
                    yh = yh * scale
                o_ref[rs, sl] = yh.astype(o_ref.dtype)

    t0 = 0
    while t0 < len(tile_kinds):
        t1 = t0 + 1
        while t1 < len(tile_kinds) and tile_kinds[t1] == tile_kinds[t0]:
            t1 += 1
        pl.when((n >= t0) & (n < t1))(functools.partial(run, tile_kinds[t0]))
        t0 = t1


def _proj(hs, w3, layer_idx, col0, head_kinds, tn, out_dtype, cos_t, sin_t, gains):
    m = hs[0].shape[0]
    k_split = tuple(h.shape[1] for h in hs)
    k = sum(k_split)
    ncols = HEAD_DIM * len(head_kinds)
    assert col0 % tn == 0 and ncols % tn == 0 and w3.shape[1] == k
    n0 = col0 // tn
    nt = ncols // tn
    hpt = tn // HEAD_DIM
    tile_kinds = tuple(tuple(head_kinds[t * hpt:(t + 1) * hpt]) for t in range(nt))
    tm, rows = next(((a, b) for a, b in PROJ_BLOCKS if m % a == 0), (TM, TM))

    def w_tile(i, n):
        return (layer_idx, 0, n0 + jnp.where(i == 0, n, nt - 1))

    return pl.pallas_call(
        functools.partial(_proj_kernel, tile_kinds=tile_kinds, rows=rows, k_split=k_split),
        grid=(m // tm, nt),
        in_specs=[pl.BlockSpec((tm, kk), lambda i, n: (i, 0)) for kk in k_split]
                 + [pl.BlockSpec((None, k, tn), w_tile),
                    pl.BlockSpec((tm, HEAD_DIM), lambda i, n: (i, 0)),
                    pl.BlockSpec((tm, HEAD_DIM), lambda i, n: (i, 0)),
                    pl.BlockSpec((2, HEAD_DIM), lambda i, n: (0, 0))],
        out_specs=pl.BlockSpec((tm, tn), lambda i, n: (i, n)),
        out_shape=jax.ShapeDtypeStruct((m, ncols), out_dtype),
        scratch_shapes=[pltpu.VMEM((nt, k, tn), BF16)],
        compiler_params=_params(2),
        name="proj",
    )(*hs, w3, cos_t, sin_t, gains)


WIN_QB = 2


def _win_kernel(sink_ref, q_ref, kp_ref, kc_ref, kn_ref, kx_ref, vp_ref, vc_ref, vn_ref, vx_ref, o_ref, *, group,
                ctx_steps):
    kvh = pl.program_id(0)
    i = pl.program_id(1)
    nq = WIN_QB * BLK
    nk = (WIN_QB + 2) * BLK
    qi = lax.broadcasted_iota(I32, (group * nq, nk), 0) & (nq - 1)
    kj = lax.broadcasted_iota(I32, (group * nq, nk), 1)
    rel = kj - BLK - qi
    lo = jnp.where(i < ctx_steps, nk, jnp.where(i > ctx_steps, 0, BLK))
    hi = jnp.where(i < ctx_steps, 0, jnp.where(i < pl.num_programs(1) - 1, nk, nk - BLK))
    mask = (rel >= -BLK) & (rel <= BLK) & (kj >= lo) & (kj < hi)
    k_win = jnp.concatenate([kp_ref[...], kc_ref[...], kn_ref[...]], axis=0)
    v_win = jnp.concatenate([vp_ref[...], vc_ref[...], vn_ref[...]], axis=0)
    v_win = jnp.concatenate([v_win, jnp.ones(v_win.shape, BF16)], axis=1)
    v_ctx = jnp.concatenate([vx_ref[...], jnp.ones(vx_ref.shape, BF16)], axis=1)
    q = jnp.concatenate([q_ref[:, g * HEAD_DIM:(g + 1) * HEAD_DIM] for g in range(group)], axis=0)
    snk = jnp.concatenate([jnp.full((nq, HEAD_DIM), sink_ref[kvh * group + g] * LOG2E, F32) for g in range(group)],
                          axis=0)
    s_loc = jnp.where(mask, _dot_nt(q, k_win), NEG_INF)
    s_ctx = _dot_nt(q, kx_ref[...])
    cols_loc = [s_loc[:, c * HEAD_DIM:(c + 1) * HEAD_DIM] for c in range(nk // HEAD_DIM)]
    cols_ctx = [s_ctx[:, c * HEAD_DIM:(c + 1) * HEAD_DIM] for c in range(kx_ref.shape[0] // HEAD_DIM)]
    mx = jnp.maximum(snk, jnp.max(functools.reduce(jnp.maximum, cols_loc + cols_ctx), -1, keepdims=True))
    p_loc = jnp.concatenate([jnp.exp2(c - mx).astype(BF16) for c in cols_loc], axis=1)
    p_ctx = jnp.concatenate([jnp.exp2(c - mx).astype(BF16) for c in cols_ctx], axis=1)
    pv = _dot(p_loc, v_win) + _dot(p_ctx, v_ctx)
    o = pv[:, :HEAD_DIM] / (pv[:, HEAD_DIM:] + jnp.exp2(snk - mx))
    for g in range(group):
        o_ref[:, g * HEAD_DIM:(g + 1) * HEAD_DIM] = o[g * nq:(g + 1) * nq, :].astype(o_ref.dtype)


def _window_attn(qkv, sink, n_ctx):
    m = qkv.shape[0]
    nb = (m - n_ctx) // BLK
    cb = n_ctx // BLK
    assert nb % WIN_QB == 0 and cb % WIN_QB == 0
    cq = cb // WIN_QB
    group = A_HEADS // A_KV
    kcol = A_Q // HEAD_DIM
    vcol = kcol + A_KV

    def edge_spec(col, block):
        return pl.BlockSpec((BLK, HEAD_DIM),
                            lambda h, i, s: (cb + jnp.clip((i - cq) * WIN_QB + block, 0, nb - 1), col + h))

    def own_spec(col):
        return pl.BlockSpec((WIN_QB * BLK, HEAD_DIM), lambda h, i, s: (i, col + h))

    def ctx_spec(col):
        return pl.BlockSpec((n_ctx, HEAD_DIM), lambda h, i, s: (0, col + h))

    grid_spec = pltpu.PrefetchScalarGridSpec(
        num_scalar_prefetch=1,
        grid=(A_KV, cq + nb // WIN_QB),
        in_specs=[pl.BlockSpec((WIN_QB * BLK, group * HEAD_DIM), lambda h, i, s: (i, h)),
                  edge_spec(kcol, -1), own_spec(kcol), edge_spec(kcol, WIN_QB), ctx_spec(kcol),
                  edge_spec(vcol, -1), own_spec(vcol), edge_spec(vcol, WIN_QB), ctx_spec(vcol)],
        out_specs=pl.BlockSpec((WIN_QB * BLK, group * HEAD_DIM), lambda h, i, s: (i, h)),
    )
    return pl.pallas_call(
        functools.partial(_win_kernel, group=group, ctx_steps=cq),
        grid_spec=grid_spec,
        out_shape=jax.ShapeDtypeStruct((m, A_Q), BF16),
        compiler_params=_params(2),
        name="window_attn",
    )(sink, qkv, qkv, qkv, qkv, qkv, qkv, qkv, qkv, qkv)


def _flash_kernel(q_ref, k_ref, v_ref, o_ref, va_ref, m_ref, acc_ref, *, group, tq, tk, n_ctx):
    rows = group * tq
    n_keys = k_ref.shape[0]

    @pl.when(pl.program_id(1) == 0)
    def _():
        va_ref[:, :HEAD_DIM] = v_ref[...]
        va_ref[:, HEAD_DIM:] = jnp.ones((n_keys, HEAD_DIM), BF16)

    def attend(limit, step):
        q = jnp.concatenate([q_ref[:, g * HEAD_DIM:(g + 1) * HEAD_DIM] for g in range(group)], axis=0)
        m_ref[...] = jnp.full((rows, HEAD_DIM), NEG_INF, F32)
        acc_ref[...] = jnp.zeros((rows, 2 * HEAD_DIM), F32)
        for j in range(limit // step):
            ks = slice(j * step, (j + 1) * step)
            s = _dot_nt(q, k_ref[ks, :])
            cols = [s[:, c * HEAD_DIM:(c + 1) * HEAD_DIM] for c in range(step // HEAD_DIM)]
            m_prev = m_ref[...]
            m_new = jnp.maximum(m_prev, jnp.max(functools.reduce(jnp.maximum, cols), -1, keepdims=True))
            alpha = jnp.exp2(m_prev - m_new)
            p = jnp.concatenate([jnp.exp2(c - m_new).astype(BF16) for c in cols], axis=1)
            acc_ref[...] = jnp.concatenate([alpha, alpha], axis=1) * acc_ref[...] + _dot(p, va_ref[ks, :])
            m_ref[...] = m_new
        for g in range(group):
            rs = slice(g * tq, (g + 1) * tq)
            o_ref[:, g * HEAD_DIM:(g + 1) * HEAD_DIM] = (acc_ref[rs, :HEAD_DIM] / acc_ref[rs, HEAD_DIM:]).astype(o_ref.dtype)

    ctx_tiles = n_ctx // tq
    pl.when(pl.program_id(1) < ctx_tiles)(functools.partial(attend, n_ctx, n_ctx))
    pl.when(pl.program_id(1) >= ctx_tiles)(functools.partial(attend, n_keys, tk))


def _flash_attn(qkv, q_col0, k_col0, v_col0, n_kv, group, n_ctx, tq, tk):
    m = qkv.shape[0]
    assert m % tq == 0 and n_ctx % tq == 0 and m % tk == 0
    qc0 = q_col0 // (group * HEAD_DIM)
    kc0 = k_col0 // HEAD_DIM
    vc0 = v_col0 // HEAD_DIM
    rows = group * tq
    return pl.pallas_call(
        functools.partial(_flash_kernel, group=group, tq=tq, tk=tk, n_ctx=n_ctx),
        grid=(n_kv, m // tq),
        in_specs=[pl.BlockSpec((tq, group * HEAD_DIM), lambda h, i: (i, qc0 + h)),
                  pl.BlockSpec((m, HEAD_DIM), lambda h, i: (0, kc0 + h)),
                  pl.BlockSpec((m, HEAD_DIM), lambda h, i: (0, vc0 + h))],
        out_specs=pl.BlockSpec((tq, group * HEAD_DIM), lambda h, i: (i, h)),
        out_shape=jax.ShapeDtypeStruct((m, n_kv * group * HEAD_DIM), BF16),
        scratch_shapes=[pltpu.VMEM((m, 2 * HEAD_DIM), BF16), pltpu.VMEM((rows, HEAD_DIM), F32),
                        pltpu.VMEM((rows, 2 * HEAD_DIM), F32)],
        compiler_params=_params(2),
        name="flash_attn",
    )(qkv, qkv, qkv)


def _log_sigmoid(x):
    return jnp.minimum(x, 0.0) - jnp.log(1.0 + jnp.exp(-jnp.abs(x)))


def _dot_exact(a, b):
    return jnp.dot(a, b, preferred_element_type=F32, precision=lax.Precision.HIGHEST)


def _mlstm_head(q, k, v, i_col, i_row, b_col, b_row, b_tot, mask, c_ref, n_ref, m_ref):
    m_prev = m_ref[:, 0:1]
    c_prev = c_ref[...]
    n_prev = n_ref[...]
    qs = q * (B_DK ** -0.5)
    qb = qs.astype(BF16)
    dlog = jnp.where(mask, b_col - b_row + i_row, NEG_INF)
    inter = b_col + m_prev
    m_t = jnp.maximum(inter, jnp.max(dlog, -1, keepdims=True))
    dw = jnp.exp(dlog - m_t)
    iw = jnp.exp(inter - m_t)
    sc = _dot_nt(qb, k.astype(BF16)) * dw
    num = _dot(sc.astype(BF16), v.astype(BF16)) + iw * _dot(qb, c_prev.astype(BF16))
    den = jnp.sum(sc, -1, keepdims=True) + iw * jnp.sum(qs * n_prev, -1, keepdims=True)
    h = num / jnp.maximum(jnp.abs(den), jnp.exp(-m_t))
    glog_col = b_tot - b_col + i_col
    glog_row = b_tot - b_row + i_row
    m_new = jnp.maximum(b_tot + m_prev, jnp.max(glog_row, -1, keepdims=True))
    decay = jnp.exp(b_tot + m_prev - m_new)
    wk = jnp.exp(glog_col - m_new) * k
    c_ref[...] = decay * c_prev + lax.dot_general(wk.astype(BF16), v.astype(BF16), (((0,), (0,)), ((), ())),
                                                  preferred_element_type=F32)
    n_ref[...] = decay * n_prev + jnp.sum(wk, 0, keepdims=True)
    m_ref[...] = jnp.broadcast_to(m_new, m_ref.shape)
    return h


def _mlstm_kernel(brow_ref, bcol_ref,
                  qf_ref, kf_ref, vf_ref, gf_ref, gtf_ref,
                  qb_ref, kb_ref, vb_ref, gb_ref, gtb_ref,
                  of_ref, ob_ref, c_ref, n_ref, m_ref):
    L = MLSTM_CHUNK

    @pl.when(pl.program_id(0) == 0)
    def _():
        c_ref[...] = jnp.zeros(c_ref.shape, F32)
        n_ref[...] = jnp.zeros(n_ref.shape, F32)
        m_ref[...] = jnp.zeros(m_ref.shape, F32)

    r = lax.broadcasted_iota(I32, (L, L), 0)
    cidx = lax.broadcasted_iota(I32, (L, L), 1)
    lane = lax.broadcasted_iota(I32, (L, HEAD_DIM), 1)
    sub = lax.broadcasted_iota(I32, (2 * 2 * B_HEADS, L), 0)
    for d, (q_ref, k_ref, v_ref, g_ref, gt_ref, o_ref) in enumerate(
            ((qf_ref, kf_ref, vf_ref, gf_ref, gtf_ref, of_ref), (qb_ref, kb_ref, vb_ref, gb_ref, gtb_ref, ob_ref))):
        mask = (cidx <= r) if d == 0 else (cidx >= r)
        mask_t = (r <= cidx) if d == 0 else (r >= cidx)
        edge = L - 1 if d == 0 else 0
        chunks = range(MLSTM_STEP_CHUNKS) if d == 0 else reversed(range(MLSTM_STEP_CHUNKS))
        for cc in chunks:
            rs = slice(cc * L, (cc + 1) * L)
            gc = g_ref[rs, :] + brow_ref[...]
            gc = jnp.where(lane < 2 * B_HEADS, gc, _log_sigmoid(gc))
            gr = gt_ref[:, rs] + bcol_ref[...]
            gr = jnp.where(sub < 2 * B_HEADS, gr, _log_sigmoid(gr))
            bc_all = _dot_exact(mask.astype(F32), gc)
            br_all = _dot_exact(gr, mask_t.astype(F32))
            for hd in range(B_HEADS):
                ci = d * B_HEADS + hd
                cf = 2 * B_HEADS + ci
                b_col = bc_all[:, cf:cf + 1]
                h = _mlstm_head(q_ref[rs, hd * B_DK:(hd + 1) * B_DK], k_ref[rs, hd * B_DK:(hd + 1) * B_DK],
                                v_ref[rs, hd * B_DV:(hd + 1) * B_DV],
                                gc[:, ci:ci + 1], gr[ci:ci + 1, :], b_col, br_all[cf:cf + 1, :],
                                b_col[edge:edge + 1, :], mask, c_ref.at[ci], n_ref.at[ci], m_ref.at[ci])
                o_ref[rs, hd * B_DV:(hd + 1) * B_DV] = h


def _mlstm(qkvo, gates, gates_t, bias_row, bias_col, n_ctx):
    m = qkvo.shape[0]
    L = MLSTM_STEP_CHUNKS * MLSTM_CHUNK
    assert m % L == 0 and n_ctx % L == 0
    nc = m // L
    ncc = n_ctx // L
    kq = B_Q // B_Q
    kv = (2 * B_Q) // B_V

    def fw(s):
        return s

    def bw(s):
        return jnp.where(s < ncc, ncc - 1 - s, nc - 1 + ncc - s)

    def specs(order):
        return [pl.BlockSpec((L, B_Q), lambda s: (order(s), 0)),
                pl.BlockSpec((L, B_Q), lambda s: (order(s), kq)),
                pl.BlockSpec((L, B_V), lambda s: (order(s), kv)),
                pl.BlockSpec((L, HEAD_DIM), lambda s: (order(s), 0)),
                pl.BlockSpec((4 * B_HEADS, L), lambda s: (0, order(s)))]

    nst = 2 * B_HEADS
    return pl.pallas_call(
        _mlstm_kernel,
        grid=(nc,),
        in_specs=[pl.BlockSpec((1, HEAD_DIM), lambda s: (0, 0)),
                  pl.BlockSpec((4 * B_HEADS, 1), lambda s: (0, 0))] + specs(fw) + specs(bw),
        out_specs=[pl.BlockSpec((L, B_V), lambda s: (fw(s), 0)),
                   pl.BlockSpec((L, B_V), lambda s: (bw(s), 0))],
        out_shape=[jax.ShapeDtypeStruct((m, B_V), F32), jax.ShapeDtypeStruct((m, B_V), F32)],
        scratch_shapes=[pltpu.VMEM((nst, B_DK, B_DV), F32), pltpu.VMEM((nst, 1, B_DK), F32),
                        pltpu.VMEM((nst, 1, HEAD_DIM), F32)],
        compiler_params=_params(1),
        name="mlstm",
    )(bias_row, bias_col, qkvo, qkvo, qkvo, gates, gates_t, qkvo, qkvo, qkvo, gates, gates_t)


def _bout_kernel(hf_ref, hb_ref, o_ref, g_ref, y_ref):
    for hd in range(B_HEADS):
        sl = slice(hd * B_DV, (hd + 1) * B_DV)
        hn = _rms(hf_ref[:, sl] + hb_ref[:, sl], g_ref[:, sl])
        y_ref[:, sl] = (jax.nn.sigmoid(o_ref[:, sl]) * hn).astype(y_ref.dtype)


def _b_out(hf, hb, qkvo, out_g):
    m = hf.shape[0]
    ocol = (2 * B_Q + B_V) // B_V
    row = pl.BlockSpec((TM, B_V), lambda i: (i, 0))
    return pl.pallas_call(
        _bout_kernel,
        grid=(m // TM,),
        in_specs=[row, row, pl.BlockSpec((TM, B_V), lambda i: (i, ocol)),
                  pl.BlockSpec((1, B_V), lambda i: (0, 0))],
        out_specs=row,
        out_shape=jax.ShapeDtypeStruct((m, B_V), BF16),
        compiler_params=_params(1),
        name="mlstm_out",
    )(hf, hb, qkvo, out_g)


def _ffn_kernel(ge_ref, gb_ref, gn_ref, x_ref, wg_ref, wu_ref, wd_ref, o_ref, wgb_ref, wub_ref, wdb_ref, *, n_sub):
    g = pl.program_id(0)
    j = pl.program_id(1)
    gn = gn_ref[g]
    big = 8 * FFN_UNIT
    n_big = lax.shift_right_logical(gn, 3)

    def cast_weights():
        wgb_ref[...] = wg_ref[...].astype(BF16)
        wub_ref[...] = wu_ref[...].astype(BF16)
        wdb_ref[...] = wd_ref[...].astype(BF16)

    def rows_step(off, rows, first):
        x = x_ref[pl.ds(off, rows), :]
        a = _dot(x, wgb_ref[...])
        b = _dot(x, wub_ref[...])
        y = _dot((a * jax.nn.sigmoid(a) * b).astype(BF16), wdb_ref[...])
        if first:
            o_ref[pl.ds(off, rows), :] = y
        else:
            o_ref[pl.ds(off, rows), :] += y

    def used_tiles(first):
        @pl.when(n_big > 0)
        def _():
            cast_weights()
            rows_step(0, big, first)

        @pl.when((n_big == 0) & (gn > 0))
        def _():
            cast_weights()

        def big_step(rq, c):
            rows_step(pl.multiple_of(rq * big, big), big, first)
            return c

        lax.fori_loop(1, n_big, big_step, 0)
        for units in (4, 2, 1):
            @pl.when((gn & units) == units)
            def _(units=units):
                rows = units * FFN_UNIT
                rows_step(pl.multiple_of((gn & ~(2 * units - 1)) * FFN_UNIT, rows), rows, first)

    @pl.when(j == 0)
    def _():
        used_tiles(True)

        def zero(r, c):
            off = pl.multiple_of(r * FFN_UNIT, FFN_UNIT)
            o_ref[pl.ds(off, FFN_UNIT), :] = jnp.zeros((FFN_UNIT, o_ref.shape[1]), F32)
            return c

        lax.fori_loop(gn, n_sub * (TM // FFN_UNIT), zero, 0)

    @pl.when(j > 0)
    def _():
        used_tiles(False)


def _ffn(x, wg, wu, wd, e0, grp_e, grp_b, grp_n, n_sub, tf):
    p, d = x.shape
    f = wg.shape[-1]
    rg = n_sub * TM
    assert f % tf == 0 and p % rg == 0
    once = pl.Buffered(1)
    nj = f // tf

    def slice_of(j, g, gn):
        return jnp.where(gn[g] > 0, j, nj - 1)

    grid_spec = pltpu.PrefetchScalarGridSpec(
        num_scalar_prefetch=3,
        grid=(grp_e.shape[0], nj),
        in_specs=[pl.BlockSpec((rg, d), lambda g, j, ge, gb, gn: (gb[g], 0), pipeline_mode=once),
                  pl.BlockSpec((None, d, tf), lambda g, j, ge, gb, gn: (e0 + ge[g], 0, slice_of(j, g, gn))),
                  pl.BlockSpec((None, d, tf), lambda g, j, ge, gb, gn: (e0 + ge[g], 0, slice_of(j, g, gn))),
                  pl.BlockSpec((None, tf, d), lambda g, j, ge, gb, gn: (e0 + ge[g], slice_of(j, g, gn), 0))],
        out_specs=pl.BlockSpec((rg, d), lambda g, j, ge, gb, gn: (g, 0), pipeline_mode=once),
        scratch_shapes=[pltpu.VMEM((d, tf), BF16), pltpu.VMEM((d, tf), BF16), pltpu.VMEM((tf, d), BF16)],
    )
    return pl.pallas_call(
        functools.partial(_ffn_kernel, n_sub=n_sub),
        grid_spec=grid_spec,
        out_shape=jax.ShapeDtypeStruct((p, d), F32),
        compiler_params=_params(2),
        name="ffn",
    )(grp_e, grp_b, grp_n, x, wg, wu, wd)


def _dense_groups(m):
    n_sub = next(r for r in (11, 3, 2, 1) if m % (r * TM) == 0)
    n_groups = m // (n_sub * TM)
    full = jnp.full((n_groups,), n_sub * (TM // FFN_UNIT), I32)
    return (jnp.zeros((n_groups,), I32), jnp.arange(n_groups, dtype=I32), full), n_sub


def _route_top2(h, r_ref, idx_ref, w_ref):
    logits = _dot(h.astype(BF16), r_ref[...].astype(BF16))
    lane = lax.broadcasted_iota(I32, logits.shape, 1).astype(F32)
    big = float(HEAD_DIM)
    lg = jnp.where(lane < N_EXPERTS, logits, NEG_INF)
    m1 = jnp.max(lg, -1, keepdims=True)
    i1 = jnp.min(jnp.where(lg == m1, lane, big), -1, keepdims=True)
    lg2 = jnp.where(lane == i1, NEG_INF, lg)
    m2 = jnp.max(lg2, -1, keepdims=True)
    i2 = jnp.min(jnp.where(lg2 == m2, lane, big), -1, keepdims=True)
    e2 = jnp.exp(m2 - m1)
    den = 1.0 + e2
    idx_ref[...] = jnp.where(lane == 0.0, i1, jnp.where(lane == 1.0, i2, 0.0)).astype(I32)
    w_ref[...] = jnp.where(lane == 0.0, 1.0 / den, jnp.where(lane == 1.0, e2 / den, 0.0))


def _moe_plan(eidx, ew, n_groups, n_sub):
    m = eidx.shape[0]
    rg = n_sub * TM
    e_flat = eidx.reshape(-1)
    experts = jnp.arange(N_EXPERTS, dtype=I32)
    onehot = (e_flat[:, None] == experts[None, :]).astype(I32)
    csum = jnp.cumsum(onehot, axis=0)
    rank = jnp.sum(csum * onehot, axis=1) - 1
    counts = csum[-1]
    groups_e = (counts + rg - 1) // rg
    grp_end = jnp.cumsum(groups_e)
    grp_start = grp_end - groups_e
    pos = jnp.sum(onehot * grp_start[None, :], axis=1) * rg + rank
    token = jnp.arange(2 * m, dtype=I32) // 2
    src = jnp.zeros((n_groups * rg,), I32).at[pos].set(token)
    gids = jnp.arange(n_groups, dtype=I32)
    valid = gids < grp_end[-1]
    grp_e = jnp.sum((gids[:, None] >= grp_end[None, :]).astype(I32), axis=1)
    grp_e = jnp.where(valid, grp_e, jnp.max(jnp.where(counts > 0, experts, 0)))
    grp_b = jnp.minimum(gids, grp_end[-1] - 1).astype(I32)
    mine = (grp_e[:, None] == experts[None, :]).astype(I32)
    rows_left = jnp.sum(mine * counts[None, :], axis=1) - (gids - jnp.sum(mine * grp_start[None, :], axis=1)) * rg
    upt = TM // FFN_UNIT
    grp_n = jnp.where(valid, jnp.clip((rows_left + FFN_UNIT - 1) // FFN_UNIT, 0, n_sub * upt), 0).astype(I32)
    tile_valid = ((jnp.arange(n_groups * n_sub, dtype=I32) % n_sub) * upt < jnp.repeat(grp_n, n_sub)).astype(I32)
    pos2 = pos.reshape(m, 2).astype(I32)
    return src, (grp_e.astype(I32), grp_b, grp_n), tile_valid, pos2[:, 0], pos2[:, 1]


SUBLANES = 8


def _gather_kernel(src_ref, tv_ref, h_hbm, o_ref, buf_ref, sem):
    t = pl.program_id(0)
    nt = pl.num_programs(0)

    def row_copy(row, slot, r8, s):
        return pltpu.make_async_copy(h_hbm.at[pl.ds(row, 1), :], buf_ref.at[slot, r8, pl.ds(s, 1), :], sem.at[slot])

    def fetch(tile):
        slot = tile % 2

        @pl.when(tv_ref[tile] == 1)
        def _():
            def issue(r8, c):
                for s in range(SUBLANES):
                    row_copy(src_ref[tile * TM + r8 * SUBLANES + s], slot, r8, s).start()
                return c

            lax.fori_loop(0, TM // SUBLANES, issue, 0)

    @pl.when(t == 0)
    def _():
        fetch(t)

    @pl.when(t + 1 < nt)
    def _():
        fetch(t + 1)

    @pl.when(tv_ref[t] == 1)
    def _():
        slot = t % 2

        def wait(r8, c):
            for s in range(SUBLANES):
                row_copy(0, slot, r8, s).wait()
            return c

        lax.fori_loop(0, TM // SUBLANES, wait, 0)
        o_ref[...] = buf_ref[slot].reshape(TM, o_ref.shape[1]).astype(o_ref.dtype)

    @pl.when(tv_ref[t] == 0)
    def _():
        o_ref[...] = jnp.zeros(o_ref.shape, o_ref.dtype)


def _gather_rows(h, src, tile_valid):
    d = h.shape[1]
    p = src.shape[0]
    grid_spec = pltpu.PrefetchScalarGridSpec(
        num_scalar_prefetch=2,
        grid=(p // TM,),
        in_specs=[pl.BlockSpec(memory_space=pl.ANY)],
        out_specs=pl.BlockSpec((TM, d), lambda t, s, tv: (t, 0)),
        scratch_shapes=[pltpu.VMEM((2, TM // SUBLANES, SUBLANES, d), F32), pltpu.SemaphoreType.DMA((2,))],
    )
    return pl.pallas_call(
        _gather_kernel,
        grid_spec=grid_spec,
        out_shape=jax.ShapeDtypeStruct((p, d), BF16),
        compiler_params=_params(1),
        name="moe_gather",
    )(src, tile_valid, h)


def _combine_kernel(p0_ref, p1_ref, ys_hbm, ew_ref, x_ref, gpost_ref, gpre_ref, mpost_ref, mpre_ref, xo_ref, *rest,
                    with_pre, tile0):
    if with_pre:
        h_ref, buf_ref, sem = rest
    else:
        h_ref = None
        buf_ref, sem = rest
    t = pl.program_id(0)
    nt = pl.num_programs(0)

    def row_copy(row, slot, k, r):
        return pltpu.make_async_copy(ys_hbm.at[pl.ds(row, 1), :], buf_ref.at[slot, k, pl.ds(r, 1), :], sem.at[slot])

    def fetch(tile):
        slot = tile % 2
        base = (tile + tile0) * TM

        def issue(r, c):
            row_copy(p0_ref[base + r], slot, 0, r).start()
            row_copy(p1_ref[base + r], slot, 1, r).start()
            return c

        lax.fori_loop(0, TM, issue, 0, unroll=SUBLANES)

    @pl.when(t == 0)
    def _():
        fetch(t)

    @pl.when(t + 1 < nt)
    def _():
        fetch(t + 1)

    slot = t % 2

    def wait(r, c):
        row_copy(0, slot, 0, r).wait()
        row_copy(0, slot, 1, r).wait()
        return c

    lax.fori_loop(0, TM, wait, 0, unroll=SUBLANES)
    f = ew_ref[:, 0:1] * buf_ref[slot, 0] + ew_ref[:, 1:2] * buf_ref[slot, 1]
    _post_pre_math(x_ref[...], f, gpost_ref, gpre_ref, mpost_ref, mpre_ref, xo_ref, h_ref, 1)


def _combine_post_pre(x, ys, pos0, pos1, ew, norm_g, mods, layer, n_ctx_tiles, h_dtype, tile0):
    m, d = x.shape
    pre_layer = min(layer + 1, norm_g.shape[0] - 1)
    with_pre = h_dtype is not None
    row_in = pl.BlockSpec((TM, d), lambda i, a, b: (i + tile0, 0))
    row_out = pl.BlockSpec((TM, d), lambda i, a, b: (i, 0))
    m_out = m - tile0 * TM

    def mod_spec(l):
        return pl.BlockSpec((None, None, 6, d), lambda i, a, b: (l, jnp.where(i + tile0 < n_ctx_tiles, 1, 0), 0, 0))

    out_shape = [jax.ShapeDtypeStruct((m_out, d), F32)]
    out_specs = [row_out]
    if with_pre:
        out_shape.append(jax.ShapeDtypeStruct((m_out, d), h_dtype))
        out_specs.append(row_out)
    grid_spec = pltpu.PrefetchScalarGridSpec(
        num_scalar_prefetch=2,
        grid=(m_out // TM,),
        in_specs=[pl.BlockSpec(memory_space=pl.ANY),
                  pl.BlockSpec((TM, HEAD_DIM), lambda i, a, b: (i + tile0, 0)), row_in,
                  pl.BlockSpec((None, 4, d), lambda i, a, b: (layer, 0, 0)),
                  pl.BlockSpec((None, 4, d), lambda i, a, b: (pre_layer, 0, 0)),
                  mod_spec(layer), mod_spec(pre_layer)],
        out_specs=out_specs,
        scratch_shapes=[pltpu.VMEM((2, 2, TM, d), F32), pltpu.SemaphoreType.DMA((2,))],
    )
    res = pl.pallas_call(
        functools.partial(_combine_kernel, with_pre=with_pre, tile0=tile0),
        grid_spec=grid_spec,
        out_shape=out_shape,
        compiler_params=_params(1),
        name="moe_combine",
    )(pos0, pos1, ys, ew, x, norm_g, norm_g, mods, mods)
    return res if with_pre else (res[0], None)


def _rope_tables(n_ctx, s):
    pos = jnp.arange(s)
    row = (pos // GRID_W).astype(F32)
    col = (pos % GRID_W).astype(F32)
    inv = ROPE_THETA ** (-jnp.arange(ROPE_FREQS, dtype=F32) / ROPE_FREQS)
    ar, ac = row[:, None] * inv, col[:, None] * inv
    cos = jnp.concatenate([jnp.cos(ar), jnp.cos(ar), jnp.cos(ac), jnp.cos(ac)], axis=-1)
    sin = jnp.concatenate([-jnp.sin(ar), jnp.sin(ar), -jnp.sin(ac), jnp.sin(ac)], axis=-1)
    cos = jnp.concatenate([jnp.ones((n_ctx, HEAD_DIM), F32), cos], axis=0)
    sin = jnp.concatenate([jnp.zeros((n_ctx, HEAD_DIM), F32), sin], axis=0)
    return cos, sin


def _layer_ab(h, j, ab_w_in, a_sink, b_ig_bias, b_fg_bias, b_norm_g, cos_t, sin_t, n_ctx):
    scale = HEAD_DIM ** -0.5 * LOG2E
    no_g = jnp.ones((2, HEAD_DIM), F32)
    a_kinds = [("rope", 0, scale)] * A_HEADS + [("rope", 0, 1.0)] * A_KV + [PLAIN] * A_KV
    a_qkv = _proj([h], ab_w_in, j, 0, a_kinds, 512, BF16, cos_t, sin_t, no_g)
    b_qkvo = _proj([h], ab_w_in, j, A_QKV, [PLAIN] * (B_QKVO // HEAD_DIM), 512, F32, cos_t, sin_t, no_g)
    ngate = 4 * B_HEADS
    w_gate = jnp.pad(ab_w_in[j, :, A_QKV + B_QKVO:], ((0, 0), (0, HEAD_DIM - ngate)))[None]
    gates = _proj([h], w_gate, 0, 0, [PLAIN], HEAD_DIM, F32, cos_t, sin_t, no_g)

    ya = _window_attn(a_qkv, a_sink[j], n_ctx)

    bias = jnp.concatenate([b_ig_bias[j].reshape(-1), b_fg_bias[j].reshape(-1)]).astype(F32)
    bias_row = jnp.pad(bias, (0, HEAD_DIM - ngate))[None, :]
    hf, hb = _mlstm(b_qkvo, gates, jnp.transpose(gates[:, :ngate]), bias_row, bias[:, None], n_ctx)
    yb = _b_out(hf, hb, b_qkvo, b_norm_g[j][None, :])
    return [ya, yb]


def _layer_c(h, j, c_w_qkv, c_qk_g, cos_t, sin_t, n_ctx):
    scale = HEAD_DIM ** -0.5 * LOG2E
    c_kinds = [("norm_rope", 0, scale)] * C_HEADS + [("norm_rope", 1, 1.0)] * C_KV + [PLAIN] * C_KV
    qkv = _proj([h], c_w_qkv, j, 0, c_kinds, 512, BF16, cos_t, sin_t, c_qk_g[j])
    m = h.shape[0]
    tk = next(t for t in (1408, 768, TM) if m % t == 0)
    return [_flash_attn(qkv, 0, C_Q, C_Q + C_KVW, C_KV, C_HEADS // C_KV, n_ctx, TM, tk)]


def kernel(x, c, ctx, c_ctx, ada_w, ada_b, norm_g, ab_w_in, ab_w_out, a_sink, b_ig_bias, b_fg_bias, b_norm_g,
           c_w_qkv, c_w_out, c_qk_g, ffn_w_gate, ffn_w_up, ffn_w_down, moe_router, moe_w_gate, moe_w_up,
           moe_w_down):
    depth = ada_w.shape[0]
    s, d = x.shape[1], x.shape[2]
    n_ctx = ctx.shape[1]
    m = n_ctx + s
    assert x.shape[0] == 1 and n_ctx % TM == 0 and s % TM == 0
    n_ctx_tiles = n_ctx // TM
    n_tiles = m // TM
    no_g = jnp.ones((2, HEAD_DIM), F32)

    mods = _ada_mods(c, c_ctx, ada_w, ada_b)
    cos_t, sin_t = _rope_tables(n_ctx, s)
    xs, h = _prenorm(ctx[0], x[0], norm_g, mods, 0)
    dense_groups, dense_sub = _dense_groups(m)
    n_exp = moe_w_gate.shape[1]
    moe_g = moe_w_gate.reshape((-1,) + moe_w_gate.shape[2:])
    moe_u = moe_w_up.reshape((-1,) + moe_w_up.shape[2:])
    moe_d = moe_w_down.reshape((-1,) + moe_w_down.shape[2:])

    for layer in range(depth):
        last = layer == depth - 1
        j = layer // 2
        even = layer % 2 == 0
        if even:
            mix = _layer_ab(h, j, ab_w_in, a_sink, b_ig_bias, b_fg_bias, b_norm_g, cos_t, sin_t, n_ctx)
            w_out = ab_w_out
        else:
            mix = _layer_c(h, j, c_w_qkv, c_qk_g, cos_t, sin_t, n_ctx)
            w_out = c_w_out
        y = _proj(mix, w_out, j, 0, [PLAIN] * (d // HEAD_DIM), 512 if d % 512 == 0 else d, F32, cos_t, sin_t, no_g)
        if even:
            xs, h = _post_pre(xs, y, norm_g, mods, layer, 0, n_ctx_tiles, BF16)
        else:
            router_p = jnp.pad(moe_router[j], ((0, 0), (0, HEAD_DIM - n_exp)))
            xs, h, eidx, ew = _post_pre(xs, y, norm_g, mods, layer, 0, n_ctx_tiles, F32, router_p)
        if even:
            f_dim = ffn_w_gate.shape[-1]
            f = _ffn(h, ffn_w_gate, ffn_w_up, ffn_w_down, j, *dense_groups, dense_sub, FFN_SLICE)
            xs, h = _post_pre(xs, f, norm_g, mods, layer, 1, n_ctx_tiles, None if last else BF16)
        else:
            n_sub = MOE_GROUP_TILES
            n_groups = (2 * m + n_exp * (n_sub * TM - 1)) // (n_sub * TM)
            src, groups, tile_valid, pos0, pos1 = _moe_plan(eidx[:, :2], ew[:, :2], n_groups, n_sub)
            x_s = _gather_rows(h, src, tile_valid)
            y_s = _ffn(x_s, moe_g, moe_u, moe_d, j * n_exp, *groups, n_sub, FFN_SLICE)
            xs, h = _combine_post_pre(xs, y_s, pos0, pos1, ew, norm_g, mods, layer, n_ctx_tiles,
                                      None if last else BF16, n_ctx_tiles if last else 0)
    return (xs if xs.shape[0] == s else xs[n_ctx:])[None]
```

```python
import functools

import jax
import jax.numpy as jnp
import numpy as np
from jax import lax
from jax.experimental import pallas as pl
from jax.experimental.pallas import tpu as pltpu

F32 = jnp.float32
BF16 = jnp.bfloat16
I32 = jnp.int32

EPS = 1e-6
GRID_W = 64
HEAD_DIM = 128
ROPE_THETA = 10000.0
ROPE_FREQS = HEAD_DIM // 4
BLK = 128
A_HEADS = 8
A_KV = 2
B_HEADS = 4
B_DK = 128
B_DV = 256
MLSTM_CHUNK = 128
MLSTM_STEP_CHUNKS = 2
C_HEADS = 16
C_KV = 4
N_EXPERTS = 8

A_Q = A_HEADS * HEAD_DIM
A_QKV = A_Q + 2 * A_KV * HEAD_DIM
B_Q = B_HEADS * B_DK
B_V = B_HEADS * B_DV
B_QKVO = 2 * B_Q + 2 * B_V
C_Q = C_HEADS * HEAD_DIM
C_KVW = C_KV * HEAD_DIM

TM = 256
MOE_GROUP_TILES = 9
PROJ_BLOCKS = ((1408, 352), (768, 384))
FFN_UNIT = 128
FFN_SLICE = 256
VMEM_LIMIT = 56 * 1024 * 1024
NEG_INF = float("-inf")
LOG2E = 1.4426950408889634


def _row_block(m):
    return next(r * TM for r in (3, 2, 1) if m % (r * TM) == 0)


def _params(n_axes, vmem=VMEM_LIMIT):
    return pltpu.CompilerParams(dimension_semantics=("arbitrary",) * n_axes, vmem_limit_bytes=vmem)


def _rms(x, g):
    return x * lax.rsqrt(jnp.mean(x * x, axis=-1, keepdims=True) + EPS) * g


def _dot(a, b):
    return jnp.dot(a, b, preferred_element_type=F32)


def _dot_nt(a, b):
    return lax.dot_general(a, b, (((1,), (1,)), ((), ())), preferred_element_type=F32)


def _ada_kernel(s_ref, w_ref, b_ref, o_ref):
    s = s_ref[...]
    s = s * jax.nn.sigmoid(s)
    o_ref[...] = _dot(s.astype(BF16), w_ref[...].astype(BF16)) + b_ref[...]


def _ada_mods(c, c_ctx, ada_w, ada_b):
    depth, d, d6 = ada_w.shape
    tn = 1536 if d6 % 1536 == 0 else d6
    s = jnp.zeros((8, d), F32).at[0].set(c[0]).at[1].set(c_ctx)
    out = pl.pallas_call(
        _ada_kernel,
        grid=(depth, d6 // tn),
        in_specs=[pl.BlockSpec((8, d), lambda l, n: (0, 0)),
                  pl.BlockSpec((None, d, tn), lambda l, n: (l, 0, n)),
                  pl.BlockSpec((None, 1, tn), lambda l, n: (l, 0, n))],
        out_specs=pl.BlockSpec((None, 8, tn), lambda l, n: (l, 0, n)),
        out_shape=jax.ShapeDtypeStruct((depth, 8, d6), F32),
        compiler_params=_params(2),
        name="ada_mods",
    )(s, ada_w, ada_b.reshape(depth, 1, d6))
    return out[:, :2].reshape(depth, 2, 6, d)


def _mod_spec(d, layer, n_ctx_tiles):
    return pl.BlockSpec((None, None, 6, d), lambda i: (layer, jnp.where(i < n_ctx_tiles, 1, 0), 0, 0))


def _prenorm_kernel(c_ref, x_ref, g_ref, mod_ref, xs_ref, h_ref, *, n_ctx_tiles):
    def emit(x):
        xs_ref[...] = x
        hn = _rms(x, g_ref[0:1, :])
        h_ref[...] = (hn * (1.0 + mod_ref[1:2, :]) + mod_ref[0:1, :]).astype(h_ref.dtype)

    pl.when(pl.program_id(0) < n_ctx_tiles)(lambda: emit(c_ref[...]))
    pl.when(pl.program_id(0) >= n_ctx_tiles)(lambda: emit(x_ref[...]))


def _prenorm(ctx2, x2, norm_g, mods, layer):
    (n_ctx, d), s = ctx2.shape, x2.shape[0]
    nct = n_ctx // TM
    row = pl.BlockSpec((TM, d), lambda i: (i, 0))
    return pl.pallas_call(
        functools.partial(_prenorm_kernel, n_ctx_tiles=nct),
        grid=((n_ctx + s) // TM,),
        in_specs=[pl.BlockSpec((TM, d), lambda i: (jnp.minimum(i, nct - 1), 0)),
                  pl.BlockSpec((TM, d), lambda i: (jnp.maximum(i - nct, 0), 0)),
                  pl.BlockSpec((None, 4, d), lambda i: (layer, 0, 0)),
                  _mod_spec(d, layer, nct)],
        out_specs=[row, row],
        out_shape=[jax.ShapeDtypeStruct((n_ctx + s, d), F32), jax.ShapeDtypeStruct((n_ctx + s, d), BF16)],
        compiler_params=_params(1),
        name="prenorm",
    )(ctx2, x2, norm_g, mods)


def _post_pre_math(x, y, gpost_ref, gpre_ref, mpost_ref, mpre_ref, xo_ref, h_ref, sub):
    g_row = 1 + 2 * sub
    gate_row = 2 + 3 * sub
    xn = x + mpost_ref[gate_row:gate_row + 1, :] * _rms(y, gpost_ref[g_row:g_row + 1, :])
    xo_ref[...] = xn
    if h_ref is None:
        return None
    nsub = 1 - sub
    hn = _rms(xn, gpre_ref[2 * nsub:2 * nsub + 1, :])
    h = hn * (1.0 + mpre_ref[3 * nsub + 1:3 * nsub + 2, :]) + mpre_ref[3 * nsub:3 * nsub + 1, :]
    h_ref[...] = h.astype(h_ref.dtype)
    return h


def _post_pre_kernel(x_ref, y_ref, gpost_ref, gpre_ref, mpost_ref, mpre_ref, *rest, sub, with_router):
    if with_router:
        r_ref, xo_ref, h_ref, idx_ref, ew_ref = rest
    else:
        xo_ref, h_ref = rest[0], (rest[1] if len(rest) > 1 else None)
    h = _post_pre_math(x_ref[...], y_ref[...], gpost_ref, gpre_ref, mpost_ref, mpre_ref, xo_ref, h_ref, sub)
    if with_router:
        _route_top2(h, r_ref, idx_ref, ew_ref)


def _post_pre(x, y, norm_g, mods, layer, sub, n_ctx_tiles, h_dtype, router_p=None):
    m, d = x.shape
    pre_layer = layer if sub == 0 else min(layer + 1, norm_g.shape[0] - 1)
    with_router = router_p is not None
    row = pl.BlockSpec((TM, d), lambda i: (i, 0))
    lanes = pl.BlockSpec((TM, HEAD_DIM), lambda i: (i, 0))
    in_specs = [row, row,
                pl.BlockSpec((None, 4, d), lambda i: (layer, 0, 0)),
                pl.BlockSpec((None, 4, d), lambda i: (pre_layer, 0, 0)),
                _mod_spec(d, layer, n_ctx_tiles),
                _mod_spec(d, pre_layer, n_ctx_tiles)]
    args = [x, y, norm_g, norm_g, mods, mods]
    out_shape = [jax.ShapeDtypeStruct((m, d), F32)]
    out_specs = [row]
    if h_dtype is not None:
        out_shape.append(jax.ShapeDtypeStruct((m, d), h_dtype))
        out_specs.append(row)
    if with_router:
        in_specs.append(pl.BlockSpec((d, HEAD_DIM), lambda i: (0, 0)))
        args.append(router_p)
        out_shape += [jax.ShapeDtypeStruct((m, HEAD_DIM), I32), jax.ShapeDtypeStruct((m, HEAD_DIM), F32)]
        out_specs += [lanes, lanes]
    res = pl.pallas_call(
        functools.partial(_post_pre_kernel, sub=sub, with_router=with_router),
        grid=(m // TM,),
        in_specs=in_specs,
        out_specs=out_specs,
        out_shape=out_shape,
        compiler_params=_params(1),
        name="post_pre",
    )(*args)
    if with_router:
        return res[0], res[1], res[2], res[3]
    return res if h_dtype is not None else (res[0], None)


def _rope(y, cos, sin):
    lane = lax.broadcasted_iota(I32, y.shape, 1)
    lower = (lane & (2 * ROPE_FREQS - 1)) < ROPE_FREQS
    partner = jnp.where(lower, pltpu.roll(y, HEAD_DIM - ROPE_FREQS, 1), pltpu.roll(y, ROPE_FREQS, 1))
    return y * cos + partner * sin


PLAIN = ("plain", 0, 1.0)


def _proj_kernel(*refs, tile_kinds, rows, k_split):
    xs_refs = refs[:len(k_split)]
    w_ref, cos_ref, sin_ref, g_ref, o_ref, wb_ref = refs[len(k_split):]
    n = pl.program_id(1)

    @pl.when(pl.program_id(0) == 0)
    def _():
        wb_ref[n] = w_ref[...].astype(BF16)

    def matmul(rs):
        k0, y = 0, None
        for x_ref, kk in zip(xs_refs, k_split):
            part = _dot(x_ref[rs, :], wb_ref[n, k0:k0 + kk, :])
            y = part if y is None else y + part
            k0 += kk
        return y

    def run(kinds):
        for c in range(o_ref.shape[0] // rows):
            rs = slice(c * rows, (c + 1) * rows)
            y = matmul(rs)
            if all(kd == PLAIN for kd in kinds):
                o_ref[rs, :] = y.astype(o_ref.dtype)
                continue
            for hh, (kind, grow, scale) in enumerate(kinds):
                sl = slice(hh * HEAD_DIM, (hh + 1) * HEAD_DIM)
                yh = y[:, sl]
                if kind == "norm_rope":
                    yh = _rms(yh, g_ref[grow:grow + 1, :])
                if kind != "plain":
                    yh = _rope(yh, cos_ref[rs, :], sin_ref[rs, :])
                if scale != 1.0:
                    yh = yh * scale
                o_ref[rs, sl] = yh.astype(o_ref.dtype)

    t0 = 0
    while t0 < len(tile_kinds):
        t1 = t0 + 1
        while t1 < len(tile_kinds) and tile_kinds[t1] == tile_kinds[t0]:
            t1 += 1
        pl.when((n >= t0) & (n < t1))(functools.partial(run, tile_kinds[t0]))
        t0 = t1


def _proj(hs, w3, layer_idx, col0, head_kinds, tn, out_dtype, cos_t, sin_t, gains):
    m = hs[0].shape[0]
    k_split = tuple(h.shape[1] for h in hs)
    k = sum(k_split)
    ncols = HEAD_DIM * len(head_kinds)
    assert col0 % tn == 0 and ncols % tn == 0 and w3.shape[1] == k
    n0 = col0 // tn
    nt = ncols // tn
    hpt = tn // HEAD_DIM
    tile_kinds = tuple(tuple(head_kinds[t * hpt:(t + 1) * hpt]) for t in range(nt))
    tm, rows = next(((a, b) for a, b in PROJ_BLOCKS if m % a == 0), (TM, TM))

    def w_tile(i, n):
        return (layer_idx, 0, n0 + jnp.where(i == 0, n, nt - 1))

    return pl.pallas_call(
        functools.partial(_proj_kernel, tile_kinds=tile_kinds, rows=rows, k_split=k_split),
        grid=(m // tm, nt),
        in_specs=[pl.BlockSpec((tm, kk), lambda i, n: (i, 0)) for kk in k_split]
                 + [pl.BlockSpec((None, k, tn), w_tile),
                    pl.BlockSpec((tm, HEAD_DIM), lambda i, n: (i, 0)),
                    pl.BlockSpec((tm, HEAD_DIM), lambda i, n: (i, 0)),
                    pl.BlockSpec((2, HEAD_DIM), lambda i, n: (0, 0))],
        out_specs=pl.BlockSpec((tm, tn), lambda i, n: (i, n)),
        out_shape=jax.ShapeDtypeStruct((m, ncols), out_dtype),
        scratch_shapes=[pltpu.VMEM((nt, k, tn), BF16)],
        compiler_params=_params(2),
        name="proj",
    )(*hs, w3, cos_t, sin_t, gains)


WIN_QB = 2


def _win_kernel(sink_ref, q_ref, kp_ref, kc_ref, kn_ref, kx_ref, vp_ref, vc_ref, vn_ref, vx_ref, o_ref, *, group,
                ctx_steps):
    kvh = pl.program_id(0)
    i = pl.program_id(1)
    nq = WIN_QB * BLK
    nk = (WIN_QB + 2) * BLK
    qi = lax.broadcasted_iota(I32, (group * nq, nk), 0) & (nq - 1)
    kj = lax.broadcasted_iota(I32, (group * nq, nk), 1)
    rel = kj - BLK - qi
    lo = jnp.where(i < ctx_steps, nk, jnp.where(i > ctx_steps, 0, BLK))
    hi = jnp.where(i < ctx_steps, 0, jnp.where(i < pl.num_programs(1) - 1, nk, nk - BLK))
    mask = (rel >= -BLK) & (rel <= BLK) & (kj >= lo) & (kj < hi)
    k_win = jnp.concatenate([kp_ref[...], kc_ref[...], kn_ref[...]], axis=0)
    v_win = jnp.concatenate([vp_ref[...], vc_ref[...], vn_ref[...]], axis=0)
    v_win = jnp.concatenate([v_win, jnp.ones(v_win.shape, BF16)], axis=1)
    v_ctx = jnp.concatenate([vx_ref[...], jnp.ones(vx_ref.shape, BF16)], axis=1)
    q = jnp.concatenate([q_ref[:, g * HEAD_DIM:(g + 1) * HEAD_DIM] for g in range(group)], axis=0)
    snk = jnp.concatenate([jnp.full((nq, HEAD_DIM), sink_ref[kvh * group + g] * LOG2E, F32) for g in range(group)],
                          axis=0)
    s_loc = jnp.where(mask, _dot_nt(q, k_win), NEG_INF)
    s_ctx = _dot_nt(q, kx_ref[...])
    cols_loc = [s_loc[:, c * HEAD_DIM:(c + 1) * HEAD_DIM] for c in range(nk // HEAD_DIM)]
    cols_ctx = [s_ctx[:, c * HEAD_DIM:(c + 1) * HEAD_DIM] for c in range(kx_ref.shape[0] // HEAD_DIM)]
    mx = jnp.maximum(snk, jnp.max(functools.reduce(jnp.maximum, cols_loc + cols_ctx), -1, keepdims=True))
    p_loc = jnp.concatenate([jnp.exp2(c - mx).astype(BF16) for c in cols_loc], axis=1)
    p_ctx = jnp.concatenate([jnp.exp2(c - mx).astype(BF16) for c in cols_ctx], axis=1)
    pv = _dot(p_loc, v_win) + _dot(p_ctx, v_ctx)
    o = pv[:, :HEAD_DIM] / (pv[:, HEAD_DIM:] + jnp.exp2(snk - mx))
    for g in range(group):
        o_ref[:, g * HEAD_DIM:(g + 1) * HEAD_DIM] = o[g * nq:(g + 1) * nq, :].astype(o_ref.dtype)


def _window_attn(qkv, sink, n_ctx):
    m = qkv.shape[0]
    nb = (m - n_ctx) // BLK
    cb = n_ctx // BLK
    assert nb % WIN_QB == 0 and cb % WIN_QB == 0
    cq = cb // WIN_QB
    group = A_HEADS // A_KV
    kcol = A_Q // HEAD_DIM
    vcol = kcol + A_KV

    def edge_spec(col, block):
        return pl.BlockSpec((BLK, HEAD_DIM),
                            lambda h, i, s: (cb + jnp.clip((i - cq) * WIN_QB + block, 0, nb - 1), col + h))

    def own_spec(col):
        return pl.BlockSpec((WIN_QB * BLK, HEAD_DIM), lambda h, i, s: (i, col + h))

    def ctx_spec(col):
        return pl.BlockSpec((n_ctx, HEAD_DIM), lambda h, i, s: (0, col + h))

    grid_spec = pltpu.PrefetchScalarGridSpec(
        num_scalar_prefetch=1,
        grid=(A_KV, cq + nb // WIN_QB),
        in_specs=[pl.BlockSpec((WIN_QB * BLK, group * HEAD_DIM), lambda h, i, s: (i, h)),
                  edge_spec(kcol, -1), own_spec(kcol), edge_spec(kcol, WIN_QB), ctx_spec(kcol),
                  edge_spec(vcol, -1), own_spec(vcol), edge_spec(vcol, WIN_QB), ctx_spec(vcol)],
        out_specs=pl.BlockSpec((WIN_QB * BLK, group * HEAD_DIM), lambda h, i, s: (i, h)),
    )
    return pl.pallas_call(
        functools.partial(_win_kernel, group=group, ctx_steps=cq),
        grid_spec=grid_spec,
        out_shape=jax.ShapeDtypeStruct((m, A_Q), BF16),
        compiler_params=_params(2),
        name="window_attn",
    )(sink, qkv, qkv, qkv, qkv, qkv, qkv, qkv, qkv, qkv)


def _flash_kernel(q_ref, k_ref, v_ref, o_ref, va_ref, m_ref, acc_ref, *, group, tq, tk, n_ctx):
    rows = group * tq
    n_keys = k_ref.shape[0]

    @pl.when(pl.program_id(1) == 0)
    def _():
        va_ref[:, :HEAD_DIM] = v_ref[...]
        va_ref[:, HEAD_DIM:] = jnp.ones((n_keys, HEAD_DIM), BF16)

    def attend(limit, step):
        q = jnp.concatenate([q_ref[:, g * HEAD_DIM:(g + 1) * HEAD_DIM] for g in range(group)], axis=0)
        m_ref[...] = jnp.full((rows, HEAD_DIM), NEG_INF, F32)
        acc_ref[...] = jnp.zeros((rows, 2 * HEAD_DIM), F32)
        for j in range(limit // step):
            ks = slice(j * step, (j + 1) * step)
            s = _dot_nt(q, k_ref[ks, :])
            cols = [s[:, c * HEAD_DIM:(c + 1) * HEAD_DIM] for c in range(step // HEAD_DIM)]
            m_prev = m_ref[...]
            m_new = jnp.maximum(m_prev, jnp.max(functools.reduce(jnp.maximum, cols), -1, keepdims=True))
            alpha = jnp.exp2(m_prev - m_new)
            p = jnp.concatenate([jnp.exp2(c - m_new).astype(BF16) for c in cols], axis=1)
            acc_ref[...] = jnp.concatenate([alpha, alpha], axis=1) * acc_ref[...] + _dot(p, va_ref[ks, :])
            m_ref[...] = m_new
        for g in range(group):
            rs = slice(g * tq, (g + 1) * tq)
            o_ref[:, g * HEAD_DIM:(g + 1) * HEAD_DIM] = (acc_ref[rs, :HEAD_DIM] / acc_ref[rs, HEAD_DIM:]).astype(o_ref.dtype)

    ctx_tiles = n_ctx // tq
    pl.when(pl.program_id(1) < ctx_tiles)(functools.partial(attend, n_ctx, n_ctx))
    pl.when(pl.program_id(1) >= ctx_tiles)(functools.partial(attend, n_keys, tk))


def _flash_attn(qkv, q_col0, k_col0, v_col0, n_kv, group, n_ctx, tq, tk):
    m = qkv.shape[0]
    assert m % tq == 0 and n_ctx % tq == 0 and m % tk == 0
    qc0 = q_col0 // (group * HEAD_DIM)
    kc0 = k_col0 // HEAD_DIM
    vc0 = v_col0 // HEAD_DIM
    rows = group * tq
    return pl.pallas_call(
        functools.partial(_flash_kernel, group=group, tq=tq, tk=tk, n_ctx=n_ctx),
        grid=(n_kv, m // tq),
        in_specs=[pl.BlockSpec((tq, group * HEAD_DIM), lambda h, i: (i, qc0 + h)),
                  pl.BlockSpec((m, HEAD_DIM), lambda h, i: (0, kc0 + h)),
                  pl.BlockSpec((m, HEAD_DIM), lambda h, i: (0, vc0 + h))],
        out_specs=pl.BlockSpec((tq, group * HEAD_DIM), lambda h, i: (i, h)),
        out_shape=jax.ShapeDtypeStruct((m, n_kv * group * HEAD_DIM), BF16),
        scratch_shapes=[pltpu.VMEM((m, 2 * HEAD_DIM), BF16), pltpu.VMEM((rows, HEAD_DIM), F32),
                        pltpu.VMEM((rows, 2 * HEAD_DIM), F32)],
        compiler_params=_params(2),
        name="flash_attn",
    )(qkv, qkv, qkv)


def _log_sigmoid(x):
    return jnp.minimum(x, 0.0) - jnp.log(1.0 + jnp.exp(-jnp.abs(x)))


def _dot_exact(a, b):
    return jnp.dot(a, b, preferred_element_type=F32, precision=lax.Precision.HIGHEST)


def _mlstm_head(q, k, v, i_col, i_row, b_col, b_row, b_tot, mask, c_ref, n_ref, m_ref):
    m_prev = m_ref[:, 0:1]
    c_prev = c_ref[...]
    n_prev = n_ref[...]
    qs = q * (B_DK ** -0.5)
    qb = qs.astype(BF16)
    dlog = jnp.where(mask, b_col - b_row + i_row, NEG_INF)
    inter = b_col + m_prev
    m_t = jnp.maximum(inter, jnp.max(dlog, -1, keepdims=True))
    dw = jnp.exp(dlog - m_t)
    iw = jnp.exp(inter - m_t)
    sc = _dot_nt(qb, k.astype(BF16)) * dw
    num = _dot(sc.astype(BF16), v.astype(BF16)) + iw * _dot(qb, c_prev.astype(BF16))
    den = jnp.sum(sc, -1, keepdims=True) + iw * jnp.sum(qs * n_prev, -1, keepdims=True)
    h = num / jnp.maximum(jnp.abs(den), jnp.exp(-m_t))
    glog_col = b_tot - b_col + i_col
    glog_row = b_tot - b_row + i_row
    m_new = jnp.maximum(b_tot + m_prev, jnp.max(glog_row, -1, keepdims=True))
    decay = jnp.exp(b_tot + m_prev - m_new)
    wk = jnp.exp(glog_col - m_new) * k
    c_ref[...] = decay * c_prev + lax.dot_general(wk.astype(BF16), v.astype(BF16), (((0,), (0,)), ((), ())),
                                                  preferred_element_type=F32)
    n_ref[...] = decay * n_prev + jnp.sum(wk, 0, keepdims=True)
    m_ref[...] = jnp.broadcast_to(m_new, m_ref.shape)
    return h


def _mlstm_kernel(brow_ref, bcol_ref,
                  qf_ref, kf_ref, vf_ref, gf_ref, gtf_ref,
                  qb_ref, kb_ref, vb_ref, gb_ref, gtb_ref,
                  of_ref, ob_ref, c_ref, n_ref, m_ref):
    L = MLSTM_CHUNK

    @pl.when(pl.program_id(0) == 0)
    def _():
        c_ref[...] = jnp.zeros(c_ref.shape, F32)
        n_ref[...] = jnp.zeros(n_ref.shape, F32)
        m_ref[...] = jnp.zeros(m_ref.shape, F32)

    r = lax.broadcasted_iota(I32, (L, L), 0)
    cidx = lax.broadcasted_iota(I32, (L, L), 1)
    lane = lax.broadcasted_iota(I32, (L, HEAD_DIM), 1)
    sub = lax.broadcasted_iota(I32, (2 * 2 * B_HEADS, L), 0)
    for d, (q_ref, k_ref, v_ref, g_ref, gt_ref, o_ref) in enumerate(
            ((qf_ref, kf_ref, vf_ref, gf_ref, gtf_ref, of_ref), (qb_ref, kb_ref, vb_ref, gb_ref, gtb_ref, ob_ref))):
        mask = (cidx <= r) if d == 0 else (cidx >= r)
        mask_t = (r <= cidx) if d == 0 else (r >= cidx)
        edge = L - 1 if d == 0 else 0
        chunks = range(MLSTM_STEP_CHUNKS) if d == 0 else reversed(range(MLSTM_STEP_CHUNKS))
        for cc in chunks:
            rs = slice(cc * L, (cc + 1) * L)
            gc = g_ref[rs, :] + brow_ref[...]
            gc = jnp.where(lane < 2 * B_HEADS, gc, _log_sigmoid(gc))
            gr = gt_ref[:, rs] + bcol_ref[...]
            gr = jnp.where(sub < 2 * B_HEADS, gr, _log_sigmoid(gr))
            bc_all = _dot_exact(mask.astype(F32), gc)
            br_all = _dot_exact(gr, mask_t.astype(F32))
            for hd in range(B_HEADS):
                ci = d * B_HEADS + hd
                cf = 2 * B_HEADS + ci
                b_col = bc_all[:, cf:cf + 1]
                h = _mlstm_head(q_ref[rs, hd * B_DK:(hd + 1) * B_DK], k_ref[rs, hd * B_DK:(hd + 1) * B_DK],
                                v_ref[rs, hd * B_DV:(hd + 1) * B_DV],
                                gc[:, ci:ci + 1], gr[ci:ci + 1, :], b_col, br_all[cf:cf + 1, :],
                                b_col[edge:edge + 1, :], mask, c_ref.at[ci], n_ref.at[ci], m_ref.at[ci])
                o_ref[rs, hd * B_DV:(hd + 1) * B_DV] = h


def _mlstm(qkvo, gates, gates_t, bias_row, bias_col, n_ctx):
    m = qkvo.shape[0]
    L = MLSTM_STEP_CHUNKS * MLSTM_CHUNK
    assert m % L == 0 and n_ctx % L == 0
    nc = m // L
    ncc = n_ctx // L
    kq = B_Q // B_Q
    kv = (2 * B_Q) // B_V

    def fw(s):
        return s

    def bw(s):
        return jnp.where(s < ncc, ncc - 1 - s, nc - 1 + ncc - s)

    def specs(order):
        return [pl.BlockSpec((L, B_Q), lambda s: (order(s), 0)),
                pl.BlockSpec((L, B_Q), lambda s: (order(s), kq)),
                pl.BlockSpec((L, B_V), lambda s: (order(s), kv)),
                pl.BlockSpec((L, HEAD_DIM), lambda s: (order(s), 0)),
                pl.BlockSpec((4 * B_HEADS, L), lambda s: (0, order(s)))]

    nst = 2 * B_HEADS
    return pl.pallas_call(
        _mlstm_kernel,
        grid=(nc,),
        in_specs=[pl.BlockSpec((1, HEAD_DIM), lambda s: (0, 0)),
                  pl.BlockSpec((4 * B_HEADS, 1), lambda s: (0, 0))] + specs(fw) + specs(bw),
        out_specs=[pl.BlockSpec((L, B_V), lambda s: (fw(s), 0)),
                   pl.BlockSpec((L, B_V), lambda s: (bw(s), 0))],
        out_shape=[jax.ShapeDtypeStruct((m, B_V), F32), jax.ShapeDtypeStruct((m, B_V), F32)],
        scratch_shapes=[pltpu.VMEM((nst, B_DK, B_DV), F32), pltpu.VMEM((nst, 1, B_DK), F32),
                        pltpu.VMEM((nst, 1, HEAD_DIM), F32)],
        compiler_params=_params(1),
        name="mlstm",
    )(bias_row, bias_col, qkvo, qkvo, qkvo, gates, gates_t, qkvo, qkvo, qkvo, gates, gates_t)


def _bout_kernel(hf_ref, hb_ref, o_ref, g_ref, y_ref):
    for hd in range(B_HEADS):
        sl = slice(hd * B_DV, (hd + 1) * B_DV)
        hn = _rms(hf_ref[:, sl] + hb_ref[:, sl], g_ref[:, sl])
        y_ref[:, sl] = (jax.nn.sigmoid(o_ref[:, sl]) * hn).astype(y_ref.dtype)


def _b_out(hf, hb, qkvo, out_g):
    m = hf.shape[0]
    ocol = (2 * B_Q + B_V) // B_V
    row = pl.BlockSpec((TM, B_V), lambda i: (i, 0))
    return pl.pallas_call(
        _bout_kernel,
        grid=(m // TM,),
        in_specs=[row, row, pl.BlockSpec((TM, B_V), lambda i: (i, ocol)),
                  pl.BlockSpec((1, B_V), lambda i: (0, 0))],
        out_specs=row,
        out_shape=jax.ShapeDtypeStruct((m, B_V), BF16),
        compiler_params=_params(1),
        name="mlstm_out",
    )(hf, hb, qkvo, out_g)


def _ffn_kernel(ge_ref, gb_ref, gn_ref, x_ref, wg_ref, wu_ref, wd_ref, o_ref, wgb_ref, wub_ref, wdb_ref, *, n_sub):
    g = pl.program_id(0)
    j = pl.program_id(1)
    gn = gn_ref[g]
    big = 8 * FFN_UNIT
    n_big = lax.shift_right_logical(gn, 3)

    def cast_weights():
        wgb_ref[...] = wg_ref[...].astype(BF16)
        wub_ref[...] = wu_ref[...].astype(BF16)
        wdb_ref[...] = wd_ref[...].astype(BF16)

    def rows_step(off, rows, first):
        x = x_ref[pl.ds(off, rows), :]
        a = _dot(x, wgb_ref[...])
        b = _dot(x, wub_ref[...])
        y = _dot((a * jax.nn.sigmoid(a) * b).astype(BF16), wdb_ref[...])
        if first:
            o_ref[pl.ds(off, rows), :] = y
        else:
            o_ref[pl.ds(off, rows), :] += y

    def used_tiles(first):
        @pl.when(n_big > 0)
        def _():
            cast_weights()
            rows_step(0, big, first)

        @pl.when((n_big == 0) & (gn > 0))
        def _():
            cast_weights()

        def big_step(rq, c):
            rows_step(pl.multiple_of(rq * big, big), big, first)
            return c

        lax.fori_loop(1, n_big, big_step, 0)
        for units in (4, 2, 1):
            @pl.when((gn & units) == units)
            def _(units=units):
                rows = units * FFN_UNIT
                rows_step(pl.multiple_of((gn & ~(2 * units - 1)) * FFN_UNIT, rows), rows, first)

    @pl.when(j == 0)
    def _():
        used_tiles(True)

        def zero(r, c):
            off = pl.multiple_of(r * FFN_UNIT, FFN_UNIT)
            o_ref[pl.ds(off, FFN_UNIT), :] = jnp.zeros((FFN_UNIT, o_ref.shape[1]), F32)
            return c

        lax.fori_loop(gn, n_sub * (TM // FFN_UNIT), zero, 0)

    @pl.when(j > 0)
    def _():
        used_tiles(False)


def _ffn(x, wg, wu, wd, e0, grp_e, grp_b, grp_n, n_sub, tf):
    p, d = x.shape
    f = wg.shape[-1]
    rg = n_sub * TM
    assert f % tf == 0 and p % rg == 0
    once = pl.Buffered(1)
    nj = f // tf

    def slice_of(j, g, gn):
        return jnp.where(gn[g] > 0, j, nj - 1)

    grid_spec = pltpu.PrefetchScalarGridSpec(
        num_scalar_prefetch=3,
        grid=(grp_e.shape[0], nj),
        in_specs=[pl.BlockSpec((rg, d), lambda g, j, ge, gb, gn: (gb[g], 0), pipeline_mode=once),
                  pl.BlockSpec((None, d, tf), lambda g, j, ge, gb, gn: (e0 + ge[g], 0, slice_of(j, g, gn))),
                  pl.BlockSpec((None, d, tf), lambda g, j, ge, gb, gn: (e0 + ge[g], 0, slice_of(j, g, gn))),
                  pl.BlockSpec((None, tf, d), lambda g, j, ge, gb, gn: (e0 + ge[g], slice_of(j, g, gn), 0))],
        out_specs=pl.BlockSpec((rg, d), lambda g, j, ge, gb, gn: (g, 0), pipeline_mode=once),
        scratch_shapes=[pltpu.VMEM((d, tf), BF16), pltpu.VMEM((d, tf), BF16), pltpu.VMEM((tf, d), BF16)],
    )
    return pl.pallas_call(
        functools.partial(_ffn_kernel, n_sub=n_sub),
        grid_spec=grid_spec,
        out_shape=jax.ShapeDtypeStruct((p, d), F32),
        compiler_params=_params(2),
        name="ffn",
    )(grp_e, grp_b, grp_n, x, wg, wu, wd)


def _dense_groups(m):
    n_sub = next(r for r in (11, 3, 2, 1) if m % (r * TM) == 0)
    n_groups = m // (n_sub * TM)
    full = jnp.full((n_groups,), n_sub * (TM // FFN_UNIT), I32)
    return (jnp.zeros((n_groups,), I32), jnp.arange(n_groups, dtype=I32), full), n_sub


def _route_top2(h, r_ref, idx_ref, w_ref):
    logits = _dot(h.astype(BF16), r_ref[...].astype(BF16))
    lane = lax.broadcasted_iota(I32, logits.shape, 1).astype(F32)
    big = float(HEAD_DIM)
    lg = jnp.where(lane < N_EXPERTS, logits, NEG_INF)
    m1 = jnp.max(lg, -1, keepdims=True)
    i1 = jnp.min(jnp.where(lg == m1, lane, big), -1, keepdims=True)
    lg2 = jnp.where(lane == i1, NEG_INF, lg)
    m2 = jnp.max(lg2, -1, keepdims=True)
    i2 = jnp.min(jnp.where(lg2 == m2, lane, big), -1, keepdims=True)
    e2 = jnp.exp(m2 - m1)
    den = 1.0 + e2
    idx_ref[...] = jnp.where(lane == 0.0, i1, jnp.where(lane == 1.0, i2, 0.0)).astype(I32)
    w_ref[...] = jnp.where(lane == 0.0, 1.0 / den, jnp.where(lane == 1.0, e2 / den, 0.0))


def _moe_plan(eidx, ew, n_groups, n_sub):
    m = eidx.shape[0]
    rg = n_sub * TM
    e_flat = eidx.reshape(-1)
    experts = jnp.arange(N_EXPERTS, dtype=I32)
    onehot = (e_flat[:, None] == experts[None, :]).astype(I32)
    csum = jnp.cumsum(onehot, axis=0)
    rank = jnp.sum(csum * onehot, axis=1) - 1
    counts = csum[-1]
    groups_e = (counts + rg - 1) // rg
    grp_end = jnp.cumsum(groups_e)
    grp_start = grp_end - groups_e
    pos = jnp.sum(onehot * grp_start[None, :], axis=1) * rg + rank
    token = jnp.arange(2 * m, dtype=I32) // 2
    src = jnp.zeros((n_groups * rg,), I32).at[pos].set(token)
    gids = jnp.arange(n_groups, dtype=I32)
    valid = gids < grp_end[-1]
    grp_e = jnp.sum((gids[:, None] >= grp_end[None, :]).astype(I32), axis=1)
    grp_e = jnp.where(valid, grp_e, jnp.max(jnp.where(counts > 0, experts, 0)))
    grp_b = jnp.minimum(gids, grp_end[-1] - 1).astype(I32)
    mine = (grp_e[:, None] == experts[None, :]).astype(I32)
    rows_left = jnp.sum(mine * counts[None, :], axis=1) - (gids - jnp.sum(mine * grp_start[None, :], axis=1)) * rg
    upt = TM // FFN_UNIT
    grp_n = jnp.where(valid, jnp.clip((rows_left + FFN_UNIT - 1) // FFN_UNIT, 0, n_sub * upt), 0).astype(I32)
    tile_valid = ((jnp.arange(n_groups * n_sub, dtype=I32) % n_sub) * upt < jnp.repeat(grp_n, n_sub)).astype(I32)
    pos2 = pos.reshape(m, 2).astype(I32)
    return src, (grp_e.astype(I32), grp_b, grp_n), tile_valid, pos2[:, 0], pos2[:, 1]


SUBLANES = 8


def _gather_kernel(src_ref, tv_ref, h_hbm, o_ref, buf_ref, sem):
    t = pl.program_id(0)
    nt = pl.num_programs(0)

    def row_copy(row, slot, r8, s):
        return pltpu.make_async_copy(h_hbm.at[pl.ds(row, 1), :], buf_ref.at[slot, r8, pl.ds(s, 1), :], sem.at[slot])

    def fetch(tile):
        slot = tile % 2

        @pl.when(tv_ref[tile] == 1)
        def _():
            def issue(r8, c):
                for s in range(SUBLANES):
                    row_copy(src_ref[tile * TM + r8 * SUBLANES + s], slot, r8, s).start(priority=s % 2)
                return c

            lax.fori_loop(0, TM // SUBLANES, issue, 0)

    @pl.when(t == 0)
    def _():
        fetch(t)

    @pl.when(t + 1 < nt)
    def _():
        fetch(t + 1)

    @pl.when(tv_ref[t] == 1)
    def _():
        slot = t % 2

        def wait(r8, c):
            for s in range(SUBLANES):
                row_copy(0, slot, r8, s).wait()
            return c

        lax.fori_loop(0, TM // SUBLANES, wait, 0)
        o_ref[...] = buf_ref[slot].reshape(TM, o_ref.shape[1]).astype(o_ref.dtype)

    @pl.when(tv_ref[t] == 0)
    def _():
        o_ref[...] = jnp.zeros(o_ref.shape, o_ref.dtype)


def _gather_rows(h, src, tile_valid):
    d = h.shape[1]
    p = src.shape[0]
    grid_spec = pltpu.PrefetchScalarGridSpec(
        num_scalar_prefetch=2,
        grid=(p // TM,),
        in_specs=[pl.BlockSpec(memory_space=pl.ANY)],
        out_specs=pl.BlockSpec((TM, d), lambda t, s, tv: (t, 0)),
        scratch_shapes=[pltpu.VMEM((2, TM // SUBLANES, SUBLANES, d), F32), pltpu.SemaphoreType.DMA((2,))],
    )
    return pl.pallas_call(
        _gather_kernel,
        grid_spec=grid_spec,
        out_shape=jax.ShapeDtypeStruct((p, d), BF16),
        compiler_params=_params(1),
        name="moe_gather",
    )(src, tile_valid, h)


def _combine_kernel(p0_ref, p1_ref, ys_hbm, ew_ref, x_ref, gpost_ref, gpre_ref, mpost_ref, mpre_ref, xo_ref, *rest,
                    with_pre, tile0):
    if with_pre:
        h_ref, buf_ref, sem = rest
    else:
        h_ref = None
        buf_ref, sem = rest
    t = pl.program_id(0)
    nt = pl.num_programs(0)

    def row_copy(row, slot, k, r):
        return pltpu.make_async_copy(ys_hbm.at[pl.ds(row, 1), :], buf_ref.at[slot, k, pl.ds(r, 1), :], sem.at[slot])

    def fetch(tile):
        slot = tile % 2
        base = (tile + tile0) * TM

        def issue(r, c):
            row_copy(p0_ref[base + r], slot, 0, r).start(priority=0)
            row_copy(p1_ref[base + r], slot, 1, r).start(priority=1)
            return c

        lax.fori_loop(0, TM, issue, 0, unroll=SUBLANES)

    @pl.when(t == 0)
    def _():
        fetch(t)

    @pl.when(t + 1 < nt)
    def _():
        fetch(t + 1)

    slot = t % 2

    def wait(r, c):
        row_copy(0, slot, 0, r).wait()
        row_copy(0, slot, 1, r).wait()
        return c

    lax.fori_loop(0, TM, wait, 0, unroll=SUBLANES)
    f = ew_ref[:, 0:1] * buf_ref[slot, 0] + ew_ref[:, 1:2] * buf_ref[slot, 1]
    _post_pre_math(x_ref[...], f, gpost_ref, gpre_ref, mpost_ref, mpre_ref, xo_ref, h_ref, 1)


def _combine_post_pre(x, ys, pos0, pos1, ew, norm_g, mods, layer, n_ctx_tiles, h_dtype, tile0):
    m, d = x.shape
    pre_layer = min(layer + 1, norm_g.shape[0] - 1)
    with_pre = h_dtype is not None
    row_in = pl.BlockSpec((TM, d), lambda i, a, b: (i + tile0, 0))
    row_out = pl.BlockSpec((TM, d), lambda i, a, b: (i, 0))
    m_out = m - tile0 * TM

    def mod_spec(l):
        return pl.BlockSpec((None, None, 6, d), lambda i, a, b: (l, jnp.where(i + tile0 < n_ctx_tiles, 1, 0), 0, 0))

    out_shape = [jax.ShapeDtypeStruct((m_out, d), F32)]
    out_specs = [row_out]
    if with_pre:
        out_shape.append(jax.ShapeDtypeStruct((m_out, d), h_dtype))
        out_specs.append(row_out)
    grid_spec = pltpu.PrefetchScalarGridSpec(
        num_scalar_prefetch=2,
        grid=(m_out // TM,),
        in_specs=[pl.BlockSpec(memory_space=pl.ANY),
                  pl.BlockSpec((TM, HEAD_DIM), lambda i, a, b: (i + tile0, 0)), row_in,
                  pl.BlockSpec((None, 4, d), lambda i, a, b: (layer, 0, 0)),
                  pl.BlockSpec((None, 4, d), lambda i, a, b: (pre_layer, 0, 0)),
                  mod_spec(layer), mod_spec(pre_layer)],
        out_specs=out_specs,
        scratch_shapes=[pltpu.VMEM((2, 2, TM, d), F32), pltpu.SemaphoreType.DMA((2,))],
    )
    res = pl.pallas_call(
        functools.partial(_combine_kernel, with_pre=with_pre, tile0=tile0),
        grid_spec=grid_spec,
        out_shape=out_shape,
        compiler_params=_params(1),
        name="moe_combine",
    )(pos0, pos1, ys, ew, x, norm_g, norm_g, mods, mods)
    return res if with_pre else (res[0], None)


def _rope_tables(n_ctx, s):
    pos = jnp.arange(s)
    row = (pos // GRID_W).astype(F32)
    col = (pos % GRID_W).astype(F32)
    inv = ROPE_THETA ** (-jnp.arange(ROPE_FREQS, dtype=F32) / ROPE_FREQS)
    ar, ac = row[:, None] * inv, col[:, None] * inv
    cos = jnp.concatenate([jnp.cos(ar), jnp.cos(ar), jnp.cos(ac), jnp.cos(ac)], axis=-1)
    sin = jnp.concatenate([-jnp.sin(ar), jnp.sin(ar), -jnp.sin(ac), jnp.sin(ac)], axis=-1)
    cos = jnp.concatenate([jnp.ones((n_ctx, HEAD_DIM), F32), cos], axis=0)
    sin = jnp.concatenate([jnp.zeros((n_ctx, HEAD_DIM), F32), sin], axis=0)
    return cos, sin


def _layer_ab(h, j, ab_w_in, a_sink, b_ig_bias, b_fg_bias, b_norm_g, cos_t, sin_t, n_ctx):
    scale = HEAD_DIM ** -0.5 * LOG2E
    no_g = jnp.ones((2, HEAD_DIM), F32)
    a_kinds = [("rope", 0, scale)] * A_HEADS + [("rope", 0, 1.0)] * A_KV + [PLAIN] * A_KV
    a_qkv = _proj([h], ab_w_in, j, 0, a_kinds, 512, BF16, cos_t, sin_t, no_g)
    b_qkvo = _proj([h], ab_w_in, j, A_QKV, [PLAIN] * (B_QKVO // HEAD_DIM), 512, F32, cos_t, sin_t, no_g)
    ngate = 4 * B_HEADS
    w_gate = jnp.pad(ab_w_in[j, :, A_QKV + B_QKVO:], ((0, 0), (0, HEAD_DIM - ngate)))[None]
    gates = _proj([h], w_gate, 0, 0, [PLAIN], HEAD_DIM, F32, cos_t, sin_t, no_g)

    ya = _window_attn(a_qkv, a_sink[j], n_ctx)

    bias = jnp.concatenate([b_ig_bias[j].reshape(-1), b_fg_bias[j].reshape(-1)]).astype(F32)
    bias_row = jnp.pad(bias, (0, HEAD_DIM - ngate))[None, :]
    hf, hb = _mlstm(b_qkvo, gates, jnp.transpose(gates[:, :ngate]), bias_row, bias[:, None], n_ctx)
    yb = _b_out(hf, hb, b_qkvo, b_norm_g[j][None, :])
    return [ya, yb]


def _layer_c(h, j, c_w_qkv, c_qk_g, cos_t, sin_t, n_ctx):
    scale = HEAD_DIM ** -0.5 * LOG2E
    c_kinds = [("norm_rope", 0, scale)] * C_HEADS + [("norm_rope", 1, 1.0)] * C_KV + [PLAIN] * C_KV
    qkv = _proj([h], c_w_qkv, j, 0, c_kinds, 512, BF16, cos_t, sin_t, c_qk_g[j])
    m = h.shape[0]
    tk = next(t for t in (1408, 768, TM) if m % t == 0)
    return [_flash_attn(qkv, 0, C_Q, C_Q + C_KVW, C_KV, C_HEADS // C_KV, n_ctx, TM, tk)]


def kernel(x, c, ctx, c_ctx, ada_w, ada_b, norm_g, ab_w_in, ab_w_out, a_sink, b_ig_bias, b_fg_bias, b_norm_g,
           c_w_qkv, c_w_out, c_qk_g, ffn_w_gate, ffn_w_up, ffn_w_down, moe_router, moe_w_gate, moe_w_up,
           moe_w_down):
    depth = ada_w.shape[0]
    s, d = x.shape[1], x.shape[2]
    n_ctx = ctx.shape[1]
    m = n_ctx + s
    assert x.shape[0] == 1 and n_ctx % TM == 0 and s % TM == 0
    n_ctx_tiles = n_ctx // TM
    n_tiles = m // TM
    no_g = jnp.ones((2, HEAD_DIM), F32)

    mods = _ada_mods(c, c_ctx, ada_w, ada_b)
    cos_t, sin_t = _rope_tables(n_ctx, s)
    xs, h = _prenorm(ctx[0], x[0], norm_g, mods, 0)
    dense_groups, dense_sub = _dense_groups(m)
    n_exp = moe_w_gate.shape[1]
    moe_g = moe_w_gate.reshape((-1,) + moe_w_gate.shape[2:])
    moe_u = moe_w_up.reshape((-1,) + moe_w_up.shape[2:])
    moe_d = moe_w_down.reshape((-1,) + moe_w_down.shape[2:])

    for layer in range(depth):
        last = layer == depth - 1
        j = layer // 2
        even = layer % 2 == 0
        if even:
            mix = _layer_ab(h, j, ab_w_in, a_sink, b_ig_bias, b_fg_bias, b_norm_g, cos_t, sin_t, n_ctx)
            w_out = ab_w_out
        else:
            mix = _layer_c(h, j, c_w_qkv, c_qk_g, cos_t, sin_t, n_ctx)
            w_out = c_w_out
        y = _proj(mix, w_out, j, 0, [PLAIN] * (d // HEAD_DIM), 512 if d % 512 == 0 else d, F32, cos_t, sin_t, no_g)
        if even:
            xs, h = _post_pre(xs, y, norm_g, mods, layer, 0, n_ctx_tiles, BF16)
        else:
            router_p = jnp.pad(moe_router[j], ((0, 0), (0, HEAD_DIM - n_exp)))
            xs, h, eidx, ew = _post_pre(xs, y, norm_g, mods, layer, 0, n_ctx_tiles, F32, router_p)
        if even:
            f_dim = ffn_w_gate.shape[-1]
            f = _ffn(h, ffn_w_gate, ffn_w_up, ffn_w_down, j, *dense_groups, dense_sub, FFN_SLICE)
            xs, h = _post_pre(xs, f, norm_g, mods, layer, 1, n_ctx_tiles, None if last else BF16)
        else:
            n_sub = MOE_GROUP_TILES
            n_groups = (2 * m + n_exp * (n_sub * TM - 1)) // (n_sub * TM)
            src, groups, tile_valid, pos0, pos1 = _moe_plan(eidx[:, :2], ew[:, :2], n_groups, n_sub)
            x_s = _gather_rows(h, src, tile_valid)
            y_s = _ffn(x_s, moe_g, moe_u, moe_d, j * n_exp, *groups, n_sub, FFN_SLICE)
            xs, h = _combine_post_pre(xs, y_s, pos0, pos1, ew, norm_g, mods, layer, n_ctx_tiles,
                                      None if last else BF16, n_ctx_tiles if last else 0)
    return (xs if xs.shape[0] == s else xs[n_ctx:])[None]
```
